```python
import jax, jax.numpy as jnp
from jax import lax

D_MODEL = 1024
BATCH = 8
SEQ = 2048
DEPTH = 2
DEC_BATCH = 128
DEC_SEQ = 4
PAST_LEN = 16384
PAGE_SIZE = 128

N_MIXERS = 2
N_LAYERS_A = (DEPTH + N_MIXERS - 1) // N_MIXERS
N_LAYERS_B = DEPTH // N_MIXERS
D_RNN = D_MODEL
N_LRU_BLOCKS = 8
LRU_BLOCK = D_RNN // N_LRU_BLOCKS
CONV_W = 4
LRU_C = 8.0
D_POOL = D_MODEL
POOL_WINDOWS = (2, 4, 8, 16)
N_POOL_GROUPS = len(POOL_WINDOWS)
POOL_GW = D_POOL // N_POOL_GROUPS
POOL_PAST = max(POOL_WINDOWS) - 1
EPS = 1e-6

kernel_name = 'hybrid_rglru_pool_decode_step'


def rmsnorm(x, g):
    xf = x.astype(jnp.float32)
    y = xf * lax.rsqrt(jnp.mean(xf * xf, axis=-1, keepdims=True) + EPS)
    return (y * g.astype(jnp.float32)).astype(x.dtype)


def causal_conv(x, buf, w, b):
    ext = jnp.concatenate([buf.astype(x.dtype), x], axis=1)
    T = x.shape[1]
    y = ext[:, 0:T] * w[0]
    for k in range(1, CONV_W):
        y = y + ext[:, k:k + T] * w[k]
    return y + b, ext[:, -(CONV_W - 1):]


def rg_lru(xb, h0, pos, w_r, b_r, w_i, b_i, lam):
    B, T, _ = xb.shape
    xh = xb.reshape(B, T, N_LRU_BLOCKS, LRU_BLOCK)
    r = jax.nn.sigmoid((jnp.einsum('btni,nij->btnj', xh, w_r).reshape(B, T, D_RNN) + b_r).astype(jnp.float32))
    ig = jax.nn.sigmoid((jnp.einsum('btni,nij->btnj', xh, w_i).reshape(B, T, D_RNN) + b_i).astype(jnp.float32))
    log_a = -LRU_C * r * jax.nn.softplus(-lam.astype(jnp.float32))
    a = jnp.exp(log_a)
    mult = jnp.sqrt(-jnp.expm1(2.0 * log_a))
    mult = jnp.where(pos[None, :, None] == 0, 1.0, mult)
    bterm = mult * ig * xb.astype(jnp.float32)

    def step(h, ab):
        a_t, b_t = ab
        h = a_t * h + b_t
        return h, h

    hT, hs = lax.scan(step, h0.astype(jnp.float32), (jnp.swapaxes(a, 0, 1), jnp.swapaxes(bterm, 0, 1)))
    return jnp.swapaxes(hs, 0, 1), hT


def recurrent_layer(x, conv_buf, h0, pos, g, w_in, conv_w, conv_b, w_r, b_r, w_i, b_i, lam, w_out):
    u = rmsnorm(x, g)
    xb, gate = jnp.split(u @ w_in, 2, axis=-1)
    xc, new_buf = causal_conv(xb, conv_buf, conv_w, conv_b)
    hs, hT = rg_lru(xc, h0, pos, w_r, b_r, w_i, b_i, lam)
    y = (hs.astype(x.dtype) * jax.nn.silu(gate)) @ w_out
    return x + y, new_buf, hT.astype(x.dtype)


def pool_layer(x, pool_buf, pos, g, w_in, w_grp, b_grp, scale, w_out):
    B, T, _ = x.shape
    u = rmsnorm(x, g)
    xb, gate = jnp.split(u @ w_in, 2, axis=-1)
    ext = jnp.concatenate([pool_buf.astype(xb.dtype), xb], axis=1)
    cs = jnp.cumsum(ext.astype(jnp.float32), axis=1)
    cs = jnp.pad(cs, ((0, 0), (1, 0), (0, 0)))
    P = POOL_PAST
    means = []
    for gi, w in enumerate(POOL_WINDOWS):
        lo, hi = gi * POOL_GW, (gi + 1) * POOL_GW
        s = cs[:, P + 1:P + T + 1, lo:hi] - cs[:, P + 1 - w:P + T + 1 - w, lo:hi]
        cnt = jnp.minimum(pos + 1, w).astype(jnp.float32)[None, :, None]
        means.append(s / cnt)
    mean = jnp.stack(means, axis=2)
    pooled = (mean - xb.astype(jnp.float32).reshape(B, T, N_POOL_GROUPS, POOL_GW)).astype(x.dtype)
    z = jnp.einsum('btgi,gij->btgj', pooled, w_grp).reshape(B, T, D_POOL) + b_grp
    z = z * scale
    y = (z * jax.nn.silu(gate)) @ w_out
    return x + y, ext[:, -P:]


def _normal(k, shape, s):
    return jax.random.normal(k, shape, jnp.float32) * s


def setup_inputs(seed: int = 0) -> dict:
    key = jax.random.key(seed)
    ks = jax.random.split(key, 24)
    u = jax.random.uniform(ks[12], (N_LAYERS_A, D_RNN), jnp.float32, minval=0.9, maxval=0.999)
    a0 = u ** (1.0 / LRU_C)
    lam = jnp.log(a0) - jnp.log1p(-a0)
    return {
        'x_prompt': _normal(ks[0], (BATCH, SEQ, D_MODEL), 1.0),
        'x_sample': _normal(ks[1], (DEC_BATCH, DEC_SEQ, D_MODEL), 1.0),
        'state_conv': _normal(ks[2], (N_LAYERS_A, DEC_BATCH, CONV_W - 1, D_RNN), 1.0),
        'state_lru': _normal(ks[3], (N_LAYERS_A, DEC_BATCH, D_RNN), 0.3),
        'state_pool': _normal(ks[4], (N_LAYERS_B, DEC_BATCH, POOL_PAST, D_POOL), 1.0),
        'a_norm': 1.0 + _normal(ks[5], (N_LAYERS_A, D_MODEL), 0.05),
        'a_w_in': _normal(ks[6], (N_LAYERS_A, D_MODEL, 2 * D_RNN), D_MODEL ** -0.5),
        'a_conv_w': _normal(ks[7], (N_LAYERS_A, CONV_W, D_RNN), CONV_W ** -0.5),
        'a_conv_b': _normal(ks[8], (N_LAYERS_A, D_RNN), 0.01),
        'a_w_r': _normal(ks[9], (N_LAYERS_A, N_LRU_BLOCKS, LRU_BLOCK, LRU_BLOCK), LRU_BLOCK ** -0.5),
        'a_b_r': _normal(ks[10], (N_LAYERS_A, D_RNN), 0.01),
        'a_w_i': _normal(ks[11], (N_LAYERS_A, N_LRU_BLOCKS, LRU_BLOCK, LRU_BLOCK), LRU_BLOCK ** -0.5),
        'a_b_i': _normal(ks[13], (N_LAYERS_A, D_RNN), 0.01),
        'a_lam': lam,
        'a_w_out': _normal(ks[14], (N_LAYERS_A, D_RNN, D_MODEL), D_RNN ** -0.5),
        'b_norm': 1.0 + _normal(ks[15], (N_LAYERS_B, D_MODEL), 0.05),
        'b_w_in': _normal(ks[16], (N_LAYERS_B, D_MODEL, 2 * D_POOL), D_MODEL ** -0.5),
        'b_w_grp': _normal(ks[17], (N_LAYERS_B, N_POOL_GROUPS, POOL_GW, POOL_GW), POOL_GW ** -0.5),
        'b_b_grp': _normal(ks[18], (N_LAYERS_B, D_POOL), 0.01),
        'b_scale': 1.0 + _normal(ks[19], (N_LAYERS_B, D_POOL), 0.1),
        'b_w_out': _normal(ks[20], (N_LAYERS_B, D_POOL, D_MODEL), D_POOL ** -0.5),
        'final_norm': 1.0 + _normal(ks[21], (D_MODEL,), 0.05),
    }


def reference(x_prompt, x_sample, state_conv, state_lru, state_pool,
              a_norm, a_w_in, a_conv_w, a_conv_b, a_w_r, a_b_r, a_w_i, a_b_i, a_lam, a_w_out,
              b_norm, b_w_in, b_w_grp, b_b_grp, b_scale, b_w_out, final_norm):
    Bp, Tp, _ = x_prompt.shape
    Ts = x_sample.shape[1]
    pos_p = jnp.arange(Tp, dtype=jnp.int32)
    pos_s = PAST_LEN + jnp.arange(Ts, dtype=jnp.int32)
    hp, hs = x_prompt, x_sample
    conv_p, lru_p, pool_p = [], [], []
    conv_s, lru_s, pool_s = [], [], []
    for layer in range(DEPTH):
        j = layer // N_MIXERS
        if layer % N_MIXERS == 0:
            wa = (a_norm[j], a_w_in[j], a_conv_w[j], a_conv_b[j], a_w_r[j], a_b_r[j],
                  a_w_i[j], a_b_i[j], a_lam[j], a_w_out[j])
            zc = jnp.zeros((Bp, CONV_W - 1, D_RNN), x_prompt.dtype)
            zh = jnp.zeros((Bp, D_RNN), jnp.float32)
            hp, c, h = recurrent_layer(hp, zc, zh, pos_p, *wa)
            conv_p.append(c)
            lru_p.append(h)
            hs, c, h = recurrent_layer(hs, state_conv[j], state_lru[j], pos_s, *wa)
            conv_s.append(c)
            lru_s.append(h)
        else:
            wb = (b_norm[j], b_w_in[j], b_w_grp[j], b_b_grp[j], b_scale[j], b_w_out[j])
            zp = jnp.zeros((Bp, POOL_PAST, D_POOL), x_prompt.dtype)
            hp, pb = pool_layer(hp, zp, pos_p, *wb)
            pool_p.append(pb)
            hs, pb = pool_layer(hs, state_pool[j], pos_s, *wb)
            pool_s.append(pb)
    y_prompt = rmsnorm(hp, final_norm)
    y_sample = rmsnorm(hs, final_norm)
    return (y_prompt, y_sample,
            jnp.stack(conv_p), jnp.stack(lru_p), jnp.stack(pool_p),
            jnp.stack(conv_s), jnp.stack(lru_s), jnp.stack(pool_s))
```

```python
import jax
import jax.numpy as jnp
from jax import lax
from jax.experimental import pallas as pl
from jax.experimental.pallas import tpu as pltpu

PAST_LEN = 16384
N_LRU_BLOCKS = 8
CONV_W = 4
LRU_C = 8.0
POOL_WINDOWS = (2, 4, 8, 16)
POOL_PAST = max(POOL_WINDOWS) - 1
EPS = 1e-6

SUBLANES = 8
CONV_PAD = 8
POOL_PAD = 16
PROMPT_BLOCK_ROWS = 256
VMEM_LIMIT_BYTES = 48 * 1024 * 1024

F32 = jnp.float32
BF16 = jnp.bfloat16


def _rmsnorm(x, g):
    return x * lax.rsqrt(jnp.mean(x * x, axis=-1, keepdims=True) + EPS) * g


def _silu(x):
    return x * jax.nn.sigmoid(x)


def _neg_c_softplus_neg(lam):
    z = -lam
    return -LRU_C * (jnp.maximum(z, 0.0) + jnp.log1p(jnp.exp(-jnp.abs(z))))


def _lru_coeffs(xc, pre_r, pre_i, nclam):
    log_a = jax.nn.sigmoid(pre_r) * nclam
    a = jnp.exp(log_a)
    mult = jnp.sqrt(jnp.tanh(log_a) * (-1.0 - a * a))
    gated = jax.nn.sigmoid(pre_i) * xc
    return a, mult, gated


def _gate_preacts(xc_bf16, wri_ref, br, bi):
    lb = xc_bf16.shape[1] // N_LRU_BLOCKS
    pr, pi = [], []
    for n in range(N_LRU_BLOCKS):
        z = jnp.dot(xc_bf16[:, n * lb:(n + 1) * lb], wri_ref[n], preferred_element_type=F32)
        pr.append(z[:, :lb])
        pi.append(z[:, lb:])
    return jnp.concatenate(pr, axis=1) + br, jnp.concatenate(pi, axis=1) + bi


def _group_proj(pooled_bf16, wgrp_ref):
    n_groups = wgrp_ref.shape[0]
    gw = pooled_bf16.shape[1] // n_groups
    return jnp.concatenate(
        [jnp.dot(pooled_bf16[:, g * gw:(g + 1) * gw], wgrp_ref[g], preferred_element_type=F32)
         for g in range(n_groups)], axis=1)


def _prompt_kernel(x_ref, an_ref, awin_ref, cw_ref, cb_ref, wri_ref, br_ref, bi_ref, lam_ref,
                   awout_ref, bn_ref, bwin_ref, wgrp_ref, bgrp_ref, bscale_ref, bwout_ref, fn_ref,
                   y_ref, conv_ref, lru_ref, pool_ref,
                   ext_s, h_s, ext2_s):
    t = pl.program_id(1)
    tb = x_ref.shape[1]
    d = x_ref.shape[2]
    gw = d // len(POOL_WINDOWS)

    @pl.when(t == 0)
    def _():
        ext_s[0:CONV_PAD, :] = jnp.zeros((CONV_PAD, d), F32)
        ext2_s[0:POOL_PAD, :] = jnp.zeros((POOL_PAD, d), F32)
        h_s[...] = jnp.zeros_like(h_s)

    row = lax.broadcasted_iota(jnp.int32, (tb, 1), 0)
    pos = t * tb + row

    x = x_ref[0]
    u = _rmsnorm(x, an_ref[...]).astype(BF16)
    proj = jnp.dot(u, awin_ref[...], preferred_element_type=F32)
    gate = proj[:, d:]
    ext_s[CONV_PAD:CONV_PAD + tb, :] = proj[:, :d]
    xc = cb_ref[...]
    for k in range(CONV_W):
        xc = xc + ext_s[pl.ds(CONV_PAD - (CONV_W - 1) + k, tb), :] * cw_ref[k:k + 1, :]
    pre_r, pre_i = _gate_preacts(xc.astype(BF16), wri_ref, br_ref[...], bi_ref[...])
    a, mult, gated = _lru_coeffs(xc, pre_r, pre_i, _neg_c_softplus_neg(lam_ref[...]))
    bterm = jnp.where(pos == 0, 1.0, mult) * gated

    sub = lax.broadcasted_iota(jnp.int32, (SUBLANES, 1), 0)
    h = h_s[...]
    hs_tiles = []
    for i in range(tb // SUBLANES):
        at = a[i * SUBLANES:(i + 1) * SUBLANES]
        bt = bterm[i * SUBLANES:(i + 1) * SUBLANES]
        shift = 1
        while shift < SUBLANES:
            keep = sub >= shift
            a_prev = jnp.where(keep, pltpu.roll(at, shift, 0), 1.0)
            b_prev = jnp.where(keep, pltpu.roll(bt, shift, 0), 0.0)
            bt = bt + at * b_prev
            at = at * a_prev
            shift *= 2
        ht = at * h + bt
        hs_tiles.append(ht)
        h = ht[SUBLANES - 1:SUBLANES, :]
    h_s[...] = h
    hs = jnp.concatenate(hs_tiles, axis=0)
    y0 = jnp.dot((hs * _silu(gate)).astype(BF16), awout_ref[...], preferred_element_type=F32)
    h1 = x + y0

    u1 = _rmsnorm(h1, bn_ref[...]).astype(BF16)
    proj1 = jnp.dot(u1, bwin_ref[...], preferred_element_type=F32)
    gate1 = proj1[:, d:]
    xb1 = proj1[:, :d]
    ext2_s[POOL_PAD:POOL_PAD + tb, :] = xb1
    means = []
    for gi, w in enumerate(POOL_WINDOWS):
        lanes = slice(gi * gw, (gi + 1) * gw)
        s = xb1[:, lanes]
        for j in range(1, w):
            s = s + ext2_s[pl.ds(POOL_PAD - j, tb), lanes]
        inv_cnt = 1.0 / jnp.minimum(pos + 1, w).astype(F32)
        means.append(s * inv_cnt)
    pooled = jnp.concatenate(means, axis=1) - xb1
    z = (_group_proj(pooled.astype(BF16), wgrp_ref) + bgrp_ref[...]) * bscale_ref[...]
    y1 = jnp.dot((z * _silu(gate1)).astype(BF16), bwout_ref[...], preferred_element_type=F32)
    h2 = h1 + y1
    y_ref[0] = _rmsnorm(h2, fn_ref[...])

    @pl.when(t == pl.num_programs(1) - 1)
    def _():
        conv_ref[0] = ext_s[tb + CONV_PAD - (CONV_W - 1):tb + CONV_PAD, :]
        lru_ref[0] = h
        pool_ref[0] = ext2_s[tb + POOL_PAD - POOL_PAST:tb + POOL_PAD, :]

    ext_s[0:CONV_PAD, :] = ext_s[tb:tb + CONV_PAD, :]
    ext2_s[0:POOL_PAD, :] = ext2_s[tb:tb + POOL_PAD, :]


def _sample_kernel(x_ref, sconv_ref, slru_ref, spool_ref,
                   an_ref, awin_ref, cw_ref, cb_ref, wri_ref, br_ref, bi_ref, lam_ref,
                   awout_ref, bn_ref, bwin_ref, wgrp_ref, bgrp_ref, bscale_ref, bwout_ref, fn_ref,
                   y_ref, conv_ref, lru_ref, pool_ref):
    ts, nb, d = x_ref.shape
    gw = d // len(POOL_WINDOWS)
    x = x_ref[...].reshape(ts * nb, d)

    def slab(v, i):
        return v[i * nb:(i + 1) * nb]

    u = _rmsnorm(x, an_ref[...]).astype(BF16)
    proj = jnp.dot(u, awin_ref[...], preferred_element_type=F32)
    xb = proj[:, :d]
    gate = proj[:, d:]
    ext = [sconv_ref[k] for k in range(CONV_W - 1)] + [slab(xb, i) for i in range(ts)]
    xc_slabs = []
    for i in range(ts):
        acc = cb_ref[...]
        for k in range(CONV_W):
            acc = acc + ext[i + k] * cw_ref[k:k + 1, :]
        xc_slabs.append(acc)
    xc = jnp.concatenate(xc_slabs, axis=0)
    pre_r, pre_i = _gate_preacts(xc.astype(BF16), wri_ref, br_ref[...], bi_ref[...])
    a, mult, gated = _lru_coeffs(xc, pre_r, pre_i, _neg_c_softplus_neg(lam_ref[...]))
    h = slru_ref[...]
    hs_slabs = []
    for i in range(ts):
        m = slab(mult, i)
        if PAST_LEN + i == 0:
            m = jnp.ones_like(m)
        h = slab(a, i) * h + m * slab(gated, i)
        hs_slabs.append(h)
    lru_ref[...] = h
    hs = jnp.concatenate(hs_slabs, axis=0)
    y0 = jnp.dot((hs * _silu(gate)).astype(BF16), awout_ref[...], preferred_element_type=F32)
    h1 = x + y0
    for k in range(CONV_W - 1):
        conv_ref[k] = ext[ts + k]

    u1 = _rmsnorm(h1, bn_ref[...]).astype(BF16)
    proj1 = jnp.dot(u1, bwin_ref[...], preferred_element_type=F32)
    xb1 = proj1[:, :d]
    gate1 = proj1[:, d:]
    ext2 = [spool_ref[k] for k in range(POOL_PAST)] + [slab(xb1, i) for i in range(ts)]
    pooled_slabs = []
    for i in range(ts):
        means = []
        for gi, w in enumerate(POOL_WINDOWS):
            lanes = slice(gi * gw, (gi + 1) * gw)
            s = ext2[POOL_PAST + i][:, lanes]
            for j in range(1, w):
                s = s + ext2[POOL_PAST + i - j][:, lanes]
            means.append(s * (1.0 / min(PAST_LEN + i + 1, w)))
        pooled_slabs.append(jnp.concatenate(means, axis=1) - ext2[POOL_PAST + i])
    pooled = jnp.concatenate(pooled_slabs, axis=0)
    z = (_group_proj(pooled.astype(BF16), wgrp_ref) + bgrp_ref[...]) * bscale_ref[...]
    y1 = jnp.dot((z * _silu(gate1)).astype(BF16), bwout_ref[...], preferred_element_type=F32)
    h2 = h1 + y1
    y_ref[...] = _rmsnorm(h2, fn_ref[...]).reshape(ts, nb, d)
    for k in range(POOL_PAST):
        pool_ref[k] = ext2[ts + k]


def _whole(shape):
    return pl.BlockSpec(shape, lambda *_: (0,) * len(shape), pipeline_mode=pl.Buffered(1))


def _whole_out(shape):
    return pl.BlockSpec(shape, lambda *_: (0,) * len(shape))


def kernel(x_prompt, x_sample, state_conv, state_lru, state_pool, a_norm, a_w_in, a_conv_w, a_conv_b, a_w_r, a_b_r, a_w_i, a_b_i, a_lam, a_w_out, b_norm, b_w_in, b_w_grp, b_b_grp, b_scale, b_w_out, final_norm):
    bp, tp, d = x_prompt.shape
    bs, ts, _ = x_sample.shape
    assert a_norm.shape[0] == 1 and b_norm.shape[0] == 1, "one layer of each mixer type"
    assert tp % PROMPT_BLOCK_ROWS == 0 and PROMPT_BLOCK_ROWS >= POOL_PAD
    assert ts >= CONV_W - 1 and ts <= POOL_PAST

    row = lambda v: v.reshape(1, d)
    wri = jnp.concatenate([a_w_r[0], a_w_i[0]], axis=-1).astype(BF16)
    weights = (row(a_norm[0]), a_w_in[0].astype(BF16), a_conv_w[0], row(a_conv_b[0]), wri,
               row(a_b_r[0]), row(a_b_i[0]), row(a_lam[0]), a_w_out[0].astype(BF16),
               row(b_norm[0]), b_w_in[0].astype(BF16), b_w_grp[0].astype(BF16), row(b_b_grp[0]),
               row(b_scale[0]), b_w_out[0].astype(BF16), row(final_norm))
    weight_specs = [_whole(w.shape) for w in weights]
    params = pltpu.CompilerParams(vmem_limit_bytes=VMEM_LIMIT_BYTES)

    tb = PROMPT_BLOCK_ROWS
    y_p, conv_p, lru_p, pool_p = pl.pallas_call(
        _prompt_kernel,
        grid=(bp, tp // tb),
        in_specs=[pl.BlockSpec((1, tb, d), lambda b, t: (b, t, 0))] + weight_specs,
        out_specs=[pl.BlockSpec((1, tb, d), lambda b, t: (b, t, 0)),
                   pl.BlockSpec((1, CONV_W - 1, d), lambda b, t: (b, 0, 0)),
                   pl.BlockSpec((1, 1, d), lambda b, t: (b, 0, 0)),
                   pl.BlockSpec((1, POOL_PAST, d), lambda b, t: (b, 0, 0))],
        out_shape=[jax.ShapeDtypeStruct((bp, tp, d), F32),
                   jax.ShapeDtypeStruct((bp, CONV_W - 1, d), F32),
                   jax.ShapeDtypeStruct((bp, 1, d), F32),
                   jax.ShapeDtypeStruct((bp, POOL_PAST, d), F32)],
        scratch_shapes=[pltpu.VMEM((CONV_PAD + tb, d), F32),
                        pltpu.VMEM((1, d), F32),
                        pltpu.VMEM((POOL_PAD + tb, d), F32)],
        compiler_params=pltpu.CompilerParams(
            vmem_limit_bytes=VMEM_LIMIT_BYTES,
            dimension_semantics=("arbitrary", "arbitrary")),
        name="prompt_step",
    )(x_prompt, *weights)

    xs = jnp.swapaxes(x_sample, 0, 1)
    sconv = jnp.swapaxes(state_conv[0], 0, 1)
    spool = jnp.swapaxes(state_pool[0], 0, 1)
    y_s, conv_s, lru_s, pool_s = pl.pallas_call(
        _sample_kernel,
        grid=(1,),
        in_specs=[_whole(xs.shape), _whole(sconv.shape), _whole((bs, d)), _whole(spool.shape)]
        + weight_specs,
        out_specs=[_whole_out(xs.shape), _whole_out(sconv.shape), _whole_out((bs, d)),
                   _whole_out(spool.shape)],
        out_shape=[jax.ShapeDtypeStruct(xs.shape, F32),
                   jax.ShapeDtypeStruct(sconv.shape, F32),
                   jax.ShapeDtypeStruct((bs, d), F32),
                   jax.ShapeDtypeStruct(spool.shape, F32)],
        compiler_params=params,
        name="sample_step",
    )(xs, sconv, state_lru[0], spool, *weights)

    return (y_p, jnp.swapaxes(y_s, 0, 1),
            conv_p[None], lru_p.reshape(1, bp, d), pool_p[None],
            jnp.swapaxes(conv_s, 0, 1)[None], lru_s[None], jnp.swapaxes(pool_s, 0, 1)[None])
```

```python
import jax
import jax.numpy as jnp
from jax import lax
from jax.experimental import pallas as pl
from jax.experimental.pallas import tpu as pltpu

PAST_LEN = 16384
N_LRU_BLOCKS = 8
CONV_W = 4
LRU_C = 8.0
POOL_WINDOWS = (2, 4, 8, 16)
POOL_PAST = max(POOL_WINDOWS) - 1
EPS = 1e-6

LANES = 128
SUBLANES = 8
CONV_PAD = SUBLANES
POOL_LEVELS = max(POOL_WINDOWS).bit_length() - 1
POOL_PAD = POOL_LEVELS * SUBLANES
PROMPT_BLOCK_ROWS = 256
VMEM_LIMIT_BYTES = 48 * 1024 * 1024

F32 = jnp.float32
BF16 = jnp.bfloat16


def _rmsnorm(x, g):
    return x * lax.rsqrt(jnp.mean(x * x, axis=-1, keepdims=True) + EPS) * g


def _silu_from_half(hg):
    return hg + hg * jnp.tanh(hg)


def _half_neg_c_softplus_neg(lam):
    z = -lam
    return (-0.5 * LRU_C) * (jnp.maximum(z, 0.0) + jnp.log1p(jnp.exp(-jnp.abs(z))))


def _lru_terms(hx, zr, zi, hbr, hbi, hn):
    log_a = hn + hn * jnp.tanh(zr + hbr)
    a = jnp.exp(log_a)
    y = jnp.tanh(log_a) * (-1.0 - a * a)
    mult = jnp.where(y > 0.0, y * lax.rsqrt(y), 0.0)
    gated = hx + hx * jnp.tanh(zi + hbi)
    return a, mult, gated


def _lane_slab(ref, c):
    return ref[:, c * LANES:(c + 1) * LANES]


def _prompt_kernel(x_ref, an_ref, awin_ref, cw_ref, cb_ref, wri_ref, hbr_ref, hbi_ref, lam_ref,
                   awout_ref, bn_ref, bwin_ref, wgrp_ref, bgrp_ref, bscale_ref, bwout_ref, fn_ref,
                   y_ref, conv_ref, lru_ref, pool_ref,
                   ext_s, sa_s, sb_s, sh_s, h_s, px_s, pt_s):
    t = pl.program_id(1)
    last_t = pl.num_programs(1) - 1
    tb = x_ref.shape[1]
    d = x_ref.shape[2]
    n_slabs = d // LANES
    seg = tb // SUBLANES
    pitch = seg + 4
    slabs_per_group = n_slabs // len(POOL_WINDOWS)

    @pl.when(t == 0)
    def _():
        ext_s[:, 0:CONV_PAD, :] = jnp.zeros((n_slabs, CONV_PAD, LANES), F32)
        px_s[:, 0:POOL_PAD, :] = jnp.zeros((n_slabs, POOL_PAD, LANES), F32)
        h_s[...] = jnp.zeros_like(h_s)

    sub = lax.broadcasted_iota(jnp.int32, (SUBLANES, 1), 0)
    first_token = jnp.logical_and(t == 0, sub == 0)

    x = x_ref[0]
    u = _rmsnorm(x, an_ref[...]).astype(BF16)
    proj = jnp.dot(u, awin_ref[...], preferred_element_type=F32)
    hn_row = _half_neg_c_softplus_neg(lam_ref[...])

    carries = []
    for c in range(n_slabs):
        xb = proj[:, c * LANES:(c + 1) * LANES]
        ext_s[c, CONV_PAD:CONV_PAD + tb, :] = xb
        hx = xb * _lane_slab(cw_ref, c)[CONV_W - 1:CONV_W] + _lane_slab(cb_ref, c)
        for k in range(CONV_W - 1):
            hx = hx + ext_s[c, pl.ds(CONV_PAD - (CONV_W - 1) + k, tb), :] * _lane_slab(cw_ref, c)[k:k + 1]
        z = jnp.dot(hx.astype(BF16), wri_ref[c], preferred_element_type=F32)
        a, mult, gated = _lru_terms(hx, z[:, :LANES], z[:, LANES:], _lane_slab(hbr_ref, c),
                                    _lane_slab(hbi_ref, c), hn_row[:, c * LANES:(c + 1) * LANES])
        mult = jnp.concatenate(
            [jnp.where(first_token, 1.0, mult[0:SUBLANES]), mult[SUBLANES:]], axis=0)
        bterm = mult * gated
        for j in range(SUBLANES):
            sa_s[c, pl.ds(j * pitch, seg), :] = a[j * seg:(j + 1) * seg]
            sb_s[c, pl.ds(j * pitch, seg), :] = bterm[j * seg:(j + 1) * seg]

        hl = sb_s[c, pl.ds(0, SUBLANES, stride=pitch), :]
        pp = sa_s[c, pl.ds(0, SUBLANES, stride=pitch), :]
        for i in range(1, seg):
            av = sa_s[c, pl.ds(i, SUBLANES, stride=pitch), :]
            hl = av * hl + sb_s[c, pl.ds(i, SUBLANES, stride=pitch), :]
            pp = av * pp
        h_in = jnp.broadcast_to(_lane_slab(h_s, c), (SUBLANES, LANES))
        cin = h_in
        for _ in range(SUBLANES - 1):
            cin = jnp.where(sub == 0, h_in, pltpu.roll(hl + pp * cin, 1, 0))
        carries.append((hl + pp * cin)[SUBLANES - 1:SUBLANES])
        h = cin
        for i in range(seg):
            h = sa_s[c, pl.ds(i, SUBLANES, stride=pitch), :] * h + sb_s[c, pl.ds(i, SUBLANES, stride=pitch), :]
            sh_s[c, pl.ds(i, SUBLANES, stride=pitch), :] = h

    h_last = jnp.concatenate(carries, axis=1)
    h_s[...] = h_last
    hs = jnp.concatenate(
        [jnp.concatenate([sh_s[c, pl.ds(j * pitch, seg), :] for j in range(SUBLANES)], axis=0)
         for c in range(n_slabs)], axis=1)
    y0 = jnp.dot((hs * _silu_from_half(proj[:, d:])).astype(BF16), awout_ref[...],
                 preferred_element_type=F32)
    h1 = x + y0

    u1 = _rmsnorm(h1, bn_ref[...]).astype(BF16)
    proj1 = jnp.dot(u1, bwin_ref[...], preferred_element_type=F32)
    row16 = lax.broadcasted_iota(jnp.int32, (2 * SUBLANES, 1), 0)
    pos16 = t * tb + row16
    pooled_slabs = []
    for c in range(n_slabs):
        w = POOL_WINDOWS[c // slabs_per_group]
        levels = w.bit_length() - 1
        xb1 = proj1[:, c * LANES:(c + 1) * LANES]
        px_s[c, POOL_PAD:POOL_PAD + tb, :] = xb1
        src = px_s.at[c]
        s = None
        for k in range(levels):
            back = (levels - 1 - k) * SUBLANES
            rows = tb + back
            s = (src[pl.ds(POOL_PAD - back, rows), :]
                 + src[pl.ds(POOL_PAD - back - (1 << k), rows), :])
            if k < levels - 1:
                pt_s[k, c, POOL_PAD - back:POOL_PAD + tb, :] = s
                src = pt_s.at[k, c]
        inv_cnt = 1.0 / jnp.minimum(pos16 + 1, w).astype(F32)
        mean = jnp.concatenate([s[0:2 * SUBLANES] * inv_cnt, s[2 * SUBLANES:] * (1.0 / w)], axis=0)
        pooled_slabs.append(mean - xb1)
    pooled = jnp.concatenate(pooled_slabs, axis=1).astype(BF16)
    gw = d // len(POOL_WINDOWS)
    z1 = jnp.concatenate(
        [jnp.dot(pooled[:, g * gw:(g + 1) * gw], wgrp_ref[g], preferred_element_type=F32)
         for g in range(len(POOL_WINDOWS))], axis=1)
    z1 = (z1 + bgrp_ref[...]) * bscale_ref[...]
    y1 = jnp.dot((z1 * _silu_from_half(proj1[:, d:])).astype(BF16), bwout_ref[...],
                 preferred_element_type=F32)
    h2 = h1 + y1
    y_ref[0] = _rmsnorm(h2, fn_ref[...])

    @pl.when(t == last_t)
    def _():
        for c in range(n_slabs):
            conv_ref[0, :, c * LANES:(c + 1) * LANES] = ext_s[c, tb + CONV_PAD - (CONV_W - 1):tb + CONV_PAD, :]
            pool_ref[0, :, c * LANES:(c + 1) * LANES] = px_s[c, tb + POOL_PAD - POOL_PAST:tb + POOL_PAD, :]
        lru_ref[0] = h_last

    ext_s[:, 0:CONV_PAD, :] = ext_s[:, tb:tb + CONV_PAD, :]
    px_s[:, 0:POOL_PAD, :] = px_s[:, tb:tb + POOL_PAD, :]


def _sample_kernel(x_ref, sconv_ref, slru_ref, spool_ref,
                   an_ref, awin_ref, cw_ref, cb_ref, wri_ref, hbr_ref, hbi_ref, lam_ref,
                   awout_ref, bn_ref, bwin_ref, wgrp_ref, bgrp_ref, bscale_ref, bwout_ref, fn_ref,
                   y_ref, conv_ref, lru_ref, pool_ref):
    ts, nb, d = x_ref.shape
    lb = d // N_LRU_BLOCKS
    gw = d // len(POOL_WINDOWS)
    x = x_ref[...].reshape(ts * nb, d)

    def slab(v, i):
        return v[i * nb:(i + 1) * nb]

    u = _rmsnorm(x, an_ref[...]).astype(BF16)
    proj = jnp.dot(u, awin_ref[...], preferred_element_type=F32)
    xb = proj[:, :d]
    ext = [sconv_ref[k] for k in range(CONV_W - 1)] + [slab(xb, i) for i in range(ts)]
    hx_slabs = []
    for i in range(ts):
        acc = cb_ref[...]
        for k in range(CONV_W):
            acc = acc + ext[i + k] * cw_ref[k:k + 1, :]
        hx_slabs.append(acc)
    hx = jnp.concatenate(hx_slabs, axis=0)
    hx_bf16 = hx.astype(BF16)
    zr, zi = [], []
    for n in range(N_LRU_BLOCKS):
        z = jnp.dot(hx_bf16[:, n * lb:(n + 1) * lb], wri_ref[n], preferred_element_type=F32)
        zr.append(z[:, :lb])
        zi.append(z[:, lb:])
    a, mult, gated = _lru_terms(hx, jnp.concatenate(zr, axis=1), jnp.concatenate(zi, axis=1),
                                hbr_ref[...], hbi_ref[...], _half_neg_c_softplus_neg(lam_ref[...]))
    h = slru_ref[...]
    hs_slabs = []
    for i in range(ts):
        m = slab(mult, i)
        if PAST_LEN + i == 0:
            m = jnp.ones_like(m)
        h = slab(a, i) * h + m * slab(gated, i)
        hs_slabs.append(h)
    lru_ref[...] = h
    hs = jnp.concatenate(hs_slabs, axis=0)
    y0 = jnp.dot((hs * _silu_from_half(proj[:, d:])).astype(BF16), awout_ref[...],
                 preferred_element_type=F32)
    h1 = x + y0
    for k in range(CONV_W - 1):
        conv_ref[k] = ext[ts + k]

    u1 = _rmsnorm(h1, bn_ref[...]).astype(BF16)
    proj1 = jnp.dot(u1, bwin_ref[...], preferred_element_type=F32)
    xb1 = proj1[:, :d]
    ext2 = [spool_ref[k] for k in range(POOL_PAST)] + [slab(xb1, i) for i in range(ts)]
    pooled_slabs = []
    for i in range(ts):
        means = []
        for gi, w in enumerate(POOL_WINDOWS):
            lanes = slice(gi * gw, (gi + 1) * gw)
            s = ext2[POOL_PAST + i][:, lanes]
            for j in range(1, w):
                s = s + ext2[POOL_PAST + i - j][:, lanes]
            means.append(s * (1.0 / min(PAST_LEN + i + 1, w)))
        pooled_slabs.append(jnp.concatenate(means, axis=1) - ext2[POOL_PAST + i])
    pooled = jnp.concatenate(pooled_slabs, axis=0).astype(BF16)
    z1 = jnp.concatenate(
        [jnp.dot(pooled[:, g * gw:(g + 1) * gw], wgrp_ref[g], preferred_element_type=F32)
         for g in range(len(POOL_WINDOWS))], axis=1)
    z1 = (z1 + bgrp_ref[...]) * bscale_ref[...]
    y1 = jnp.dot((z1 * _silu_from_half(proj1[:, d:])).astype(BF16), bwout_ref[...],
                 preferred_element_type=F32)
    h2 = h1 + y1
    y_ref[...] = _rmsnorm(h2, fn_ref[...]).reshape(ts, nb, d)
    for k in range(POOL_PAST):
        pool_ref[k] = ext2[ts + k]


def _whole(shape):
    return pl.BlockSpec(shape, lambda *_: (0,) * len(shape), pipeline_mode=pl.Buffered(1))


def _whole_out(shape):
    return pl.BlockSpec(shape, lambda *_: (0,) * len(shape))


def _half_gate_columns(w_in, d):
    return jnp.concatenate([w_in[:, :d], 0.5 * w_in[:, d:]], axis=1).astype(BF16)


def kernel(x_prompt, x_sample, state_conv, state_lru, state_pool, a_norm, a_w_in, a_conv_w, a_conv_b, a_w_r, a_b_r, a_w_i, a_b_i, a_lam, a_w_out, b_norm, b_w_in, b_w_grp, b_b_grp, b_scale, b_w_out, final_norm):
    bp, tp, d = x_prompt.shape
    bs, ts, _ = x_sample.shape
    tb = PROMPT_BLOCK_ROWS
    assert a_norm.shape[0] == 1 and b_norm.shape[0] == 1, "one layer of each mixer type"
    assert d % LANES == 0 and d // N_LRU_BLOCKS == LANES
    assert tp % tb == 0 and tb % (SUBLANES * SUBLANES) == 0 and tb >= 2 * SUBLANES
    assert CONV_W - 1 <= ts <= POOL_PAST

    row = lambda v: v.reshape(1, d)
    wri = jnp.concatenate([a_w_r[0], a_w_i[0]], axis=-1).astype(BF16)
    weights = (row(a_norm[0]), _half_gate_columns(a_w_in[0], d), 0.5 * a_conv_w[0],
               row(0.5 * a_conv_b[0]), wri, row(0.5 * a_b_r[0]), row(0.5 * a_b_i[0]), row(a_lam[0]),
               a_w_out[0].astype(BF16), row(b_norm[0]), _half_gate_columns(b_w_in[0], d),
               b_w_grp[0].astype(BF16), row(b_b_grp[0]), row(b_scale[0]), b_w_out[0].astype(BF16),
               row(final_norm))
    weight_specs = [_whole(w.shape) for w in weights]

    n_slabs = d // LANES
    seg_rows = SUBLANES * (tb // SUBLANES + 4)
    y_p, conv_p, lru_p, pool_p = pl.pallas_call(
        _prompt_kernel,
        grid=(bp, tp // tb),
        in_specs=[pl.BlockSpec((1, tb, d), lambda b, t: (b, t, 0))] + weight_specs,
        out_specs=[pl.BlockSpec((1, tb, d), lambda b, t: (b, t, 0)),
                   pl.BlockSpec((1, CONV_W - 1, d), lambda b, t: (b, 0, 0)),
                   pl.BlockSpec((1, 1, d), lambda b, t: (b, 0, 0)),
                   pl.BlockSpec((1, POOL_PAST, d), lambda b, t: (b, 0, 0))],
        out_shape=[jax.ShapeDtypeStruct((bp, tp, d), F32),
                   jax.ShapeDtypeStruct((bp, CONV_W - 1, d), F32),
                   jax.ShapeDtypeStruct((bp, 1, d), F32),
                   jax.ShapeDtypeStruct((bp, POOL_PAST, d), F32)],
        scratch_shapes=[pltpu.VMEM((n_slabs, CONV_PAD + tb, LANES), F32),
                        pltpu.VMEM((n_slabs, seg_rows, LANES), F32),
                        pltpu.VMEM((n_slabs, seg_rows, LANES), F32),
                        pltpu.VMEM((n_slabs, seg_rows, LANES), F32),
                        pltpu.VMEM((1, d), F32),
                        pltpu.VMEM((n_slabs, POOL_PAD + tb, LANES), F32),
                        pltpu.VMEM((POOL_LEVELS - 1, n_slabs, POOL_PAD + tb, LANES), F32)],
        compiler_params=pltpu.CompilerParams(
            vmem_limit_bytes=VMEM_LIMIT_BYTES,
            dimension_semantics=("arbitrary", "arbitrary")),
        name="prompt_step",
    )(x_prompt, *weights)

    xs = jnp.swapaxes(x_sample, 0, 1)
    sconv = jnp.swapaxes(state_conv[0], 0, 1)
    spool = jnp.swapaxes(state_pool[0], 0, 1)
    y_s, conv_s, lru_s, pool_s = pl.pallas_call(
        _sample_kernel,
        grid=(1,),
        in_specs=[_whole(xs.shape), _whole(sconv.shape), _whole((bs, d)), _whole(spool.shape)]
        + weight_specs,
        out_specs=[_whole_out(xs.shape), _whole_out(sconv.shape), _whole_out((bs, d)),
                   _whole_out(spool.shape)],
        out_shape=[jax.ShapeDtypeStruct(xs.shape, F32),
                   jax.ShapeDtypeStruct(sconv.shape, F32),
                   jax.ShapeDtypeStruct((bs, d), F32),
                   jax.ShapeDtypeStruct(spool.shape, F32)],
        compiler_params=pltpu.CompilerParams(vmem_limit_bytes=VMEM_LIMIT_BYTES),
        name="sample_step",
    )(xs, sconv, state_lru[0], spool, *weights)

    return (y_p, jnp.swapaxes(y_s, 0, 1),
            conv_p[None], lru_p.reshape(1, bp, d), pool_p[None],
            jnp.swapaxes(conv_s, 0, 1)[None], lru_s[None], jnp.swapaxes(pool_s, 0, 1)[None])
```

```python
import jax
import jax.numpy as jnp
from jax import lax
from jax.experimental import pallas as pl
from jax.experimental.pallas import tpu as pltpu

PAST_LEN = 16384
N_LRU_BLOCKS = 8
CONV_W = 4
LRU_C = 8.0
POOL_WINDOWS = (2, 4, 8, 16)
POOL_PAST = max(POOL_WINDOWS) - 1
EPS = 1e-6

LANES = 128
SUBLANES = 8
CONV_PAD = SUBLANES
POOL_LEVELS = max(POOL_WINDOWS).bit_length() - 1
POOL_PAD = POOL_LEVELS * SUBLANES
PROMPT_BLOCK_ROWS = 512
VMEM_LIMIT_BYTES = 48 * 1024 * 1024

F32 = jnp.float32
BF16 = jnp.bfloat16


def _rmsnorm(x, g):
    return x * lax.rsqrt(jnp.mean(x * x, axis=-1, keepdims=True) + EPS) * g


def _silu_from_half(hg):
    return hg + hg * jnp.tanh(hg)


def _half_neg_c_softplus_neg(lam):
    z = -lam
    return (-0.5 * LRU_C) * (jnp.maximum(z, 0.0) + jnp.log1p(jnp.exp(-jnp.abs(z))))


def _lru_terms(hx, zr, zi, hbr, hbi, hn):
    log_a = hn + hn * jnp.tanh(zr + hbr)
    a = jnp.exp(log_a)
    y = jnp.tanh(log_a) * (-1.0 - a * a)
    mult = jnp.where(y > 0.0, y * lax.rsqrt(y), 0.0)
    gated = hx + hx * jnp.tanh(zi + hbi)
    return a, mult, gated


def _lane_slab(ref, c):
    return ref[:, c * LANES:(c + 1) * LANES]


def _prompt_kernel(x_ref, an_ref, awin_ref, cw_ref, cb_ref, wri_ref, hbr_ref, hbi_ref, lam_ref,
                   awout_ref, bn_ref, bwin_ref, wgrp_ref, bgrp_ref, bscale_ref, bwout_ref, fn_ref,
                   y_ref, conv_ref, lru_ref, pool_ref,
                   ext_s, sa_s, sb_s, sh_s, h_s, px_s, pt_s):
    t = pl.program_id(1)
    last_t = pl.num_programs(1) - 1
    tb = x_ref.shape[1]
    d = x_ref.shape[2]
    n_slabs = d // LANES
    seg = tb // SUBLANES
    pitch = seg + 4
    slabs_per_group = n_slabs // len(POOL_WINDOWS)

    @pl.when(t == 0)
    def _():
        ext_s[:, 0:CONV_PAD, :] = jnp.zeros((n_slabs, CONV_PAD, LANES), F32)
        px_s[:, 0:POOL_PAD, :] = jnp.zeros((n_slabs, POOL_PAD, LANES), F32)
        h_s[...] = jnp.zeros_like(h_s)

    sub = lax.broadcasted_iota(jnp.int32, (SUBLANES, 1), 0)
    first_token = jnp.logical_and(t == 0, sub == 0)

    x = x_ref[0]
    u = _rmsnorm(x, an_ref[...]).astype(BF16)
    proj = jnp.dot(u, awin_ref[...], preferred_element_type=F32)
    hn_row = _half_neg_c_softplus_neg(lam_ref[...])

    carries = []
    for c in range(n_slabs):
        xb = proj[:, c * LANES:(c + 1) * LANES]
        ext_s[c, CONV_PAD:CONV_PAD + tb, :] = xb
        hx = xb * _lane_slab(cw_ref, c)[CONV_W - 1:CONV_W] + _lane_slab(cb_ref, c)
        for k in range(CONV_W - 1):
            hx = hx + ext_s[c, pl.ds(CONV_PAD - (CONV_W - 1) + k, tb), :] * _lane_slab(cw_ref, c)[k:k + 1]
        z = jnp.dot(hx.astype(BF16), wri_ref[c], preferred_element_type=F32)
        a, mult, gated = _lru_terms(hx, z[:, :LANES], z[:, LANES:], _lane_slab(hbr_ref, c),
                                    _lane_slab(hbi_ref, c), hn_row[:, c * LANES:(c + 1) * LANES])
        mult = jnp.concatenate(
            [jnp.where(first_token, 1.0, mult[0:SUBLANES]), mult[SUBLANES:]], axis=0)
        bterm = mult * gated
        for j in range(SUBLANES):
            sa_s[c, pl.ds(j * pitch, seg), :] = a[j * seg:(j + 1) * seg]
            sb_s[c, pl.ds(j * pitch, seg), :] = bterm[j * seg:(j + 1) * seg]

        hl = sb_s[c, pl.ds(0, SUBLANES, stride=pitch), :]
        pp = sa_s[c, pl.ds(0, SUBLANES, stride=pitch), :]
        for i in range(1, seg):
            av = sa_s[c, pl.ds(i, SUBLANES, stride=pitch), :]
            hl = av * hl + sb_s[c, pl.ds(i, SUBLANES, stride=pitch), :]
            pp = av * pp
        h_in = jnp.broadcast_to(_lane_slab(h_s, c), (SUBLANES, LANES))
        cin = h_in
        for _ in range(SUBLANES - 1):
            cin = jnp.where(sub == 0, h_in, pltpu.roll(hl + pp * cin, 1, 0))
        carries.append((hl + pp * cin)[SUBLANES - 1:SUBLANES])
        h = cin
        for i in range(seg):
            h = sa_s[c, pl.ds(i, SUBLANES, stride=pitch), :] * h + sb_s[c, pl.ds(i, SUBLANES, stride=pitch), :]
            sh_s[c, pl.ds(i, SUBLANES, stride=pitch), :] = h

    h_last = jnp.concatenate(carries, axis=1)
    h_s[...] = h_last
    hs = jnp.concatenate(
        [jnp.concatenate([sh_s[c, pl.ds(j * pitch, seg), :] for j in range(SUBLANES)], axis=0)
         for c in range(n_slabs)], axis=1)
    y0 = jnp.dot((hs * _silu_from_half(proj[:, d:])).astype(BF16), awout_ref[...],
                 preferred_element_type=F32)
    h1 = x + y0

    u1 = _rmsnorm(h1, bn_ref[...]).astype(BF16)
    proj1 = jnp.dot(u1, bwin_ref[...], preferred_element_type=F32)
    row16 = lax.broadcasted_iota(jnp.int32, (2 * SUBLANES, 1), 0)
    pos16 = t * tb + row16
    pooled_slabs = []
    for c in range(n_slabs):
        w = POOL_WINDOWS[c // slabs_per_group]
        levels = w.bit_length() - 1
        xb1 = proj1[:, c * LANES:(c + 1) * LANES]
        px_s[c, POOL_PAD:POOL_PAD + tb, :] = xb1
        src = px_s.at[c]
        s = None
        for k in range(levels):
            back = (levels - 1 - k) * SUBLANES
            rows = tb + back
            s = (src[pl.ds(POOL_PAD - back, rows), :]
                 + src[pl.ds(POOL_PAD - back - (1 << k), rows), :])
            if k < levels - 1:
                pt_s[k, c, POOL_PAD - back:POOL_PAD + tb, :] = s
                src = pt_s.at[k, c]
        inv_cnt = 1.0 / jnp.minimum(pos16 + 1, w).astype(F32)
        mean = jnp.concatenate([s[0:2 * SUBLANES] * inv_cnt, s[2 * SUBLANES:] * (1.0 / w)], axis=0)
        pooled_slabs.append(mean - xb1)
    pooled = jnp.concatenate(pooled_slabs, axis=1).astype(BF16)
    gw = d // len(POOL_WINDOWS)
    z1 = jnp.concatenate(
        [jnp.dot(pooled[:, g * gw:(g + 1) * gw], wgrp_ref[g], preferred_element_type=F32)
         for g in range(len(POOL_WINDOWS))], axis=1)
    z1 = (z1 + bgrp_ref[...]) * bscale_ref[...]
    y1 = jnp.dot((z1 * _silu_from_half(proj1[:, d:])).astype(BF16), bwout_ref[...],
                 preferred_element_type=F32)
    h2 = h1 + y1
    y_ref[0] = _rmsnorm(h2, fn_ref[...])

    @pl.when(t == last_t)
    def _():
        for c in range(n_slabs):
            conv_ref[0, :, c * LANES:(c + 1) * LANES] = ext_s[c, tb + CONV_PAD - (CONV_W - 1):tb + CONV_PAD, :]
            pool_ref[0, :, c * LANES:(c + 1) * LANES] = px_s[c, tb + POOL_PAD - POOL_PAST:tb + POOL_PAD, :]
        lru_ref[0] = h_last

    ext_s[:, 0:CONV_PAD, :] = ext_s[:, tb:tb + CONV_PAD, :]
    px_s[:, 0:POOL_PAD, :] = px_s[:, tb:tb + POOL_PAD, :]


def _sample_kernel(x_ref, sconv_ref, slru_ref, spool_ref,
                   an_ref, awin_ref, cw_ref, cb_ref, wri_ref, hbr_ref, hbi_ref, lam_ref,
                   awout_ref, bn_ref, bwin_ref, wgrp_ref, bgrp_ref, bscale_ref, bwout_ref, fn_ref,
                   y_ref, conv_ref, lru_ref, pool_ref):
    ts, nb, d = x_ref.shape
    lb = d // N_LRU_BLOCKS
    gw = d // len(POOL_WINDOWS)
    x = x_ref[...].reshape(ts * nb, d)

    def slab(v, i):
        return v[i * nb:(i + 1) * nb]

    u = _rmsnorm(x, an_ref[...]).astype(BF16)
    proj = jnp.dot(u, awin_ref[...], preferred_element_type=F32)
    xb = proj[:, :d]
    ext = [sconv_ref[k] for k in range(CONV_W - 1)] + [slab(xb, i) for i in range(ts)]
    hx_slabs = []
    for i in range(ts):
        acc = cb_ref[...]
        for k in range(CONV_W):
            acc = acc + ext[i + k] * cw_ref[k:k + 1, :]
        hx_slabs.append(acc)
    hx = jnp.concatenate(hx_slabs, axis=0)
    hx_bf16 = hx.astype(BF16)
    zr, zi = [], []
    for n in range(N_LRU_BLOCKS):
        z = jnp.dot(hx_bf16[:, n * lb:(n + 1) * lb], wri_ref[n], preferred_element_type=F32)
        zr.append(z[:, :lb])
        zi.append(z[:, lb:])
    a, mult, gated = _lru_terms(hx, jnp.concatenate(zr, axis=1), jnp.concatenate(zi, axis=1),
                                hbr_ref[...], hbi_ref[...], _half_neg_c_softplus_neg(lam_ref[...]))
    h = slru_ref[...]
    hs_slabs = []
    for i in range(ts):
        m = slab(mult, i)
        if PAST_LEN + i == 0:
            m = jnp.ones_like(m)
        h = slab(a, i) * h + m * slab(gated, i)
        hs_slabs.append(h)
    lru_ref[...] = h
    hs = jnp.concatenate(hs_slabs, axis=0)
    y0 = jnp.dot((hs * _silu_from_half(proj[:, d:])).astype(BF16), awout_ref[...],
                 preferred_element_type=F32)
    h1 = x + y0
    for k in range(CONV_W - 1):
        conv_ref[k] = ext[ts + k]

    u1 = _rmsnorm(h1, bn_ref[...]).astype(BF16)
    proj1 = jnp.dot(u1, bwin_ref[...], preferred_element_type=F32)
    xb1 = proj1[:, :d]
    ext2 = [spool_ref[k] for k in range(POOL_PAST)] + [slab(xb1, i) for i in range(ts)]
    pooled_slabs = []
    for i in range(ts):
        means = []
        for gi, w in enumerate(POOL_WINDOWS):
            lanes = slice(gi * gw, (gi + 1) * gw)
            s = ext2[POOL_PAST + i][:, lanes]
            for j in range(1, w):
                s = s + ext2[POOL_PAST + i - j][:, lanes]
            means.append(s * (1.0 / min(PAST_LEN + i + 1, w)))
        pooled_slabs.append(jnp.concatenate(means, axis=1) - ext2[POOL_PAST + i])
    pooled = jnp.concatenate(pooled_slabs, axis=0).astype(BF16)
    z1 = jnp.concatenate(
        [jnp.dot(pooled[:, g * gw:(g + 1) * gw], wgrp_ref[g], preferred_element_type=F32)
         for g in range(len(POOL_WINDOWS))], axis=1)
    z1 = (z1 + bgrp_ref[...]) * bscale_ref[...]
    y1 = jnp.dot((z1 * _silu_from_half(proj1[:, d:])).astype(BF16), bwout_ref[...],
                 preferred_element_type=F32)
    h2 = h1 + y1
    y_ref[...] = _rmsnorm(h2, fn_ref[...]).reshape(ts, nb, d)
    for k in range(POOL_PAST):
        pool_ref[k] = ext2[ts + k]


def _whole(shape):
    return pl.BlockSpec(shape, lambda *_: (0,) * len(shape), pipeline_mode=pl.Buffered(1))


def _whole_out(shape):
    return pl.BlockSpec(shape, lambda *_: (0,) * len(shape))


def _half_gate_columns(w_in, d):
    return jnp.concatenate([w_in[:, :d], 0.5 * w_in[:, d:]], axis=1).astype(BF16)


def kernel(x_prompt, x_sample, state_conv, state_lru, state_pool, a_norm, a_w_in, a_conv_w, a_conv_b, a_w_r, a_b_r, a_w_i, a_b_i, a_lam, a_w_out, b_norm, b_w_in, b_w_grp, b_b_grp, b_scale, b_w_out, final_norm):
    bp, tp, d = x_prompt.shape
    bs, ts, _ = x_sample.shape
    tb = PROMPT_BLOCK_ROWS
    assert a_norm.shape[0] == 1 and b_norm.shape[0] == 1, "one layer of each mixer type"
    assert d % LANES == 0 and d // N_LRU_BLOCKS == LANES
    assert tp % tb == 0 and tb % (SUBLANES * SUBLANES) == 0 and tb >= 2 * SUBLANES
    assert CONV_W - 1 <= ts <= POOL_PAST

    row = lambda v: v.reshape(1, d)
    wri = jnp.concatenate([a_w_r[0], a_w_i[0]], axis=-1).astype(BF16)
    weights = (row(a_norm[0]), _half_gate_columns(a_w_in[0], d), 0.5 * a_conv_w[0],
               row(0.5 * a_conv_b[0]), wri, row(0.5 * a_b_r[0]), row(0.5 * a_b_i[0]), row(a_lam[0]),
               a_w_out[0].astype(BF16), row(b_norm[0]), _half_gate_columns(b_w_in[0], d),
               b_w_grp[0].astype(BF16), row(b_b_grp[0]), row(b_scale[0]), b_w_out[0].astype(BF16),
               row(final_norm))
    weight_specs = [_whole(w.shape) for w in weights]

    n_slabs = d // LANES
    seg_rows = SUBLANES * (tb // SUBLANES + 4)
    y_p, conv_p, lru_p, pool_p = pl.pallas_call(
        _prompt_kernel,
        grid=(bp, tp // tb),
        in_specs=[pl.BlockSpec((1, tb, d), lambda b, t: (b, t, 0))] + weight_specs,
        out_specs=[pl.BlockSpec((1, tb, d), lambda b, t: (b, t, 0)),
                   pl.BlockSpec((1, CONV_W - 1, d), lambda b, t: (b, 0, 0)),
                   pl.BlockSpec((1, 1, d), lambda b, t: (b, 0, 0)),
                   pl.BlockSpec((1, POOL_PAST, d), lambda b, t: (b, 0, 0))],
        out_shape=[jax.ShapeDtypeStruct((bp, tp, d), F32),
                   jax.ShapeDtypeStruct((bp, CONV_W - 1, d), F32),
                   jax.ShapeDtypeStruct((bp, 1, d), F32),
                   jax.ShapeDtypeStruct((bp, POOL_PAST, d), F32)],
        scratch_shapes=[pltpu.VMEM((n_slabs, CONV_PAD + tb, LANES), F32),
                        pltpu.VMEM((n_slabs, seg_rows, LANES), F32),
                        pltpu.VMEM((n_slabs, seg_rows, LANES), F32),
                        pltpu.VMEM((n_slabs, seg_rows, LANES), F32),
                        pltpu.VMEM((1, d), F32),
                        pltpu.VMEM((n_slabs, POOL_PAD + tb, LANES), F32),
                        pltpu.VMEM((POOL_LEVELS - 1, n_slabs, POOL_PAD + tb, LANES), F32)],
        compiler_params=pltpu.CompilerParams(
            vmem_limit_bytes=VMEM_LIMIT_BYTES,
            dimension_semantics=("arbitrary", "arbitrary")),
        name="prompt_step",
    )(x_prompt, *weights)

    xs = jnp.swapaxes(x_sample, 0, 1)
    sconv = jnp.swapaxes(state_conv[0], 0, 1)
    spool = jnp.swapaxes(state_pool[0], 0, 1)
    y_s, conv_s, lru_s, pool_s = pl.pallas_call(
        _sample_kernel,
        grid=(1,),
        in_specs=[_whole(xs.shape), _whole(sconv.shape), _whole((bs, d)), _whole(spool.shape)]
        + weight_specs,
        out_specs=[_whole_out(xs.shape), _whole_out(sconv.shape), _whole_out((bs, d)),
                   _whole_out(spool.shape)],
        out_shape=[jax.ShapeDtypeStruct(xs.shape, F32),
                   jax.ShapeDtypeStruct(sconv.shape, F32),
                   jax.ShapeDtypeStruct((bs, d), F32),
                   jax.ShapeDtypeStruct(spool.shape, F32)],
        compiler_params=pltpu.CompilerParams(vmem_limit_bytes=VMEM_LIMIT_BYTES),
        name="sample_step",
    )(xs, sconv, state_lru[0], spool, *weights)

    return (y_p, jnp.swapaxes(y_s, 0, 1),
            conv_p[None], lru_p.reshape(1, bp, d), pool_p[None],
            jnp.swapaxes(conv_s, 0, 1)[None], lru_s[None], jnp.swapaxes(pool_s, 0, 1)[None])
```

```python
import functools

import jax
import jax.numpy as jnp
from jax import lax
from jax.experimental import pallas as pl
from jax.experimental.pallas import tpu as pltpu

PAST_LEN = 16384
N_LRU_BLOCKS = 8
CONV_W = 4
LRU_C = 8.0
POOL_WINDOWS = (2, 4, 8, 16)
POOL_PAST = max(POOL_WINDOWS) - 1
EPS = 1e-6

LANES = 128
SUBLANES = 8
MXU_COLS = 256
CONV_PAD = SUBLANES
POOL_LEVELS = max(POOL_WINDOWS).bit_length() - 1
POOL_PAD = POOL_LEVELS * SUBLANES
PROMPT_BLOCK_ROWS = 512
VMEM_LIMIT_BYTES = 56 * 1024 * 1024

F32 = jnp.float32
BF16 = jnp.bfloat16


def _rmsnorm(x, g):
    return x * lax.rsqrt(jnp.mean(x * x, axis=-1, keepdims=True) + EPS) * g


def _silu_from_half(hg):
    return hg + hg * jnp.tanh(hg)


def _half_neg_c_softplus_neg(lam):
    z = -lam
    return (-0.5 * LRU_C) * (jnp.maximum(z, 0.0) + jnp.log1p(jnp.exp(-jnp.abs(z))))


def _lru_terms(hx, zr, zi, hbr, hbi, hn):
    log_a = hn + hn * jnp.tanh(zr + hbr)
    a = jnp.exp(log_a)
    y = jnp.tanh(log_a) * (-1.0 - a * a)
    mult = jnp.where(y > 0.0, y * lax.rsqrt(y), 0.0)
    gated = hx + hx * jnp.tanh(zi + hbi)
    return a, mult, gated


def _lane_slab(ref, c):
    return ref[:, c * LANES:(c + 1) * LANES]


def _cols(q):
    return slice(q * MXU_COLS, (q + 1) * MXU_COLS)


def _prompt_stage_order(n_chunks, slabs_per_chunk):
    assert (n_chunks, slabs_per_chunk) == (4, 2)
    return ("aN aX0 bN aX1 aC0 bX0 aC1 aC2 aC3 aR0 aR1 aK0 aX2 aS0 aK1 bX1 aS1 bP0 aR2 aC4 aR3 bM0 aC5 aX3 "
            "aK2 aS2 bX2 aK3 aS3 aR4 bP1 aR5 bM1 aC6 aG0 aC7 aK4 bX3 aS4 aK5 aR6 aR7 aS5 bP2 bM2 aK6 aG1 aS6 "
            "aK7 bG0 aS7 bP3 bM3 aH0 aG2 aH1 bZ0 bG1 aH2 aG3 bZ1 bG2 aH3 bG3 bZ2 bZ3 bO0 bO1 bO2 bO3 aO0 bF "
            "aO1 aO2 aO3").split()


def _prompt_kernel(x_ref, an_ref, awin_ref, cw_ref, cb_ref, wri_ref, hbr_ref, hbi_ref, lam_ref,
                   awout_ref, bn_ref, bwin_ref, wgrp_ref, bgrp_ref, bscale_ref, bwout_ref, fn_ref,
                   y_ref, conv_ref, lru_ref, pool_ref,
                   ext_s, sa_s, sb_s, sh_s, h_s, h1_s, px_s, pt_s,
                   ua_s, zr_s, ga_s, hsg_s, ub_s, pooled_s, z_s, gq_s, zg_s, *, blocks_per_seq):
    n = pl.program_id(0)
    n_blocks = pl.num_programs(0) - 1
    ta = jnp.minimum(n, n_blocks - 1) % blocks_per_seq
    tbk = jnp.maximum(n - 1, 0) % blocks_per_seq
    tb, d = x_ref.shape[1], x_ref.shape[2]
    n_slabs = d // LANES
    n_chunks = d // MXU_COLS
    slabs_per_chunk = MXU_COLS // LANES
    assert d // len(POOL_WINDOWS) == MXU_COLS, "one pooling group per matmul column chunk"
    seg = tb // SUBLANES
    pitch = seg + 4

    @pl.when(n == 0)
    def _():
        h1_s[...] = jnp.zeros_like(h1_s)

    @pl.when(ta == 0)
    def _():
        ext_s[:, 0:CONV_PAD, :] = jnp.zeros((n_slabs, CONV_PAD, LANES), F32)
        h_s[...] = jnp.zeros_like(h_s)

    @pl.when(tbk == 0)
    def _():
        px_s[:, 0:POOL_PAD, :] = jnp.zeros((n_slabs, POOL_PAD, LANES), F32)

    h1_in = h1_s.at[(n + 1) % 2]
    h1_out = h1_s.at[n % 2]
    sub = lax.broadcasted_iota(jnp.int32, (SUBLANES, 1), 0)
    first_token = jnp.logical_and(ta == 0, sub == 0)
    row16 = lax.broadcasted_iota(jnp.int32, (2 * SUBLANES, 1), 0)
    pos16 = tbk * tb + row16
    hn_row = _half_neg_c_softplus_neg(lam_ref[...])
    held = {}
    carries = [None] * n_slabs
    stages = {}

    def a_norm():
        ua_s[...] = _rmsnorm(x_ref[0], an_ref[...]).astype(BF16)

    def a_xb_chunk(q):
        pq = jnp.dot(ua_s[...], awin_ref[:, _cols(q)], preferred_element_type=F32)
        for i in range(slabs_per_chunk):
            ext_s[q * slabs_per_chunk + i, CONV_PAD:CONV_PAD + tb, :] = pq[:, i * LANES:(i + 1) * LANES]

    def a_gate_chunk(q):
        ga_s[q] = jnp.dot(ua_s[...], awin_ref[:, _cols(n_chunks + q)], preferred_element_type=F32)

    def a_conv(c):
        hx = (ext_s[c, CONV_PAD:CONV_PAD + tb, :] * _lane_slab(cw_ref, c)[CONV_W - 1:CONV_W]
              + _lane_slab(cb_ref, c))
        for k in range(CONV_W - 1):
            hx = hx + ext_s[c, pl.ds(CONV_PAD - (CONV_W - 1) + k, tb), :] * _lane_slab(cw_ref, c)[k:k + 1]
        held["hx", c] = hx
        held["hxb", c] = hx.astype(BF16)

    def a_gate_proj(c):
        zr_s[c % 2] = jnp.dot(held.pop(("hxb", c)), wri_ref[c], preferred_element_type=F32)

    def a_coeffs(c):
        a, mult, gated = _lru_terms(held.pop(("hx", c)), zr_s[c % 2, :, :LANES], zr_s[c % 2, :, LANES:],
                                    _lane_slab(hbr_ref, c), _lane_slab(hbi_ref, c),
                                    hn_row[:, c * LANES:(c + 1) * LANES])
        mult = jnp.concatenate(
            [jnp.where(first_token, 1.0, mult[0:SUBLANES]), mult[SUBLANES:]], axis=0)
        bterm = mult * gated
        for j in range(SUBLANES):
            sa_s[c, pl.ds(j * pitch, seg), :] = a[j * seg:(j + 1) * seg]
            sb_s[c, pl.ds(j * pitch, seg), :] = bterm[j * seg:(j + 1) * seg]

    def a_scan(c):
        hl = sb_s[c, pl.ds(0, SUBLANES, stride=pitch), :]
        pp = sa_s[c, pl.ds(0, SUBLANES, stride=pitch), :]
        for i in range(1, seg):
            av = sa_s[c, pl.ds(i, SUBLANES, stride=pitch), :]
            hl = av * hl + sb_s[c, pl.ds(i, SUBLANES, stride=pitch), :]
            pp = av * pp
        h_in = jnp.broadcast_to(_lane_slab(h_s, c), (SUBLANES, LANES))
        cin = h_in
        for _ in range(SUBLANES - 1):
            cin = jnp.where(sub == 0, h_in, pltpu.roll(hl + pp * cin, 1, 0))
        carries[c] = (hl + pp * cin)[SUBLANES - 1:SUBLANES]
        h = cin
        for i in range(seg):
            h = sa_s[c, pl.ds(i, SUBLANES, stride=pitch), :] * h + sb_s[c, pl.ds(i, SUBLANES, stride=pitch), :]
            sh_s[c, pl.ds(i, SUBLANES, stride=pitch), :] = h

    def a_gated_out(q):
        hs = jnp.concatenate(
            [jnp.concatenate([sh_s[c, pl.ds(j * pitch, seg), :] for j in range(SUBLANES)], axis=0)
             for c in range(q * slabs_per_chunk, (q + 1) * slabs_per_chunk)], axis=1)
        hsg_s[:, _cols(q)] = (hs * _silu_from_half(ga_s[q])).astype(BF16)

    def a_out_proj(q):
        y0 = jnp.dot(hsg_s[...], awout_ref[:, _cols(q)], preferred_element_type=F32)
        h1_out[:, _cols(q)] = x_ref[0, :, _cols(q)] + y0

    def b_norm():
        ub_s[...] = _rmsnorm(h1_in[...], bn_ref[...]).astype(BF16)

    def b_xb_chunk(q):
        pq = jnp.dot(ub_s[...], bwin_ref[:, _cols(q)], preferred_element_type=F32)
        for i in range(slabs_per_chunk):
            px_s[q * slabs_per_chunk + i, POOL_PAD:POOL_PAD + tb, :] = pq[:, i * LANES:(i + 1) * LANES]

    def pool_slab(c, w):
        levels = w.bit_length() - 1
        tmp = c % slabs_per_chunk
        src = px_s.at[c]
        s = None
        for k in range(levels):
            back = (levels - 1 - k) * SUBLANES
            rows = tb + back
            s = (src[pl.ds(POOL_PAD - back, rows), :]
                 + src[pl.ds(POOL_PAD - back - (1 << k), rows), :])
            if k < levels - 1:
                pt_s[k % 2, tmp, POOL_PAD - back:POOL_PAD + tb, :] = s
                src = pt_s.at[k % 2, tmp]
        inv_cnt = 1.0 / jnp.minimum(pos16 + 1, w).astype(F32)
        mean = jnp.concatenate([s[0:2 * SUBLANES] * inv_cnt, s[2 * SUBLANES:] * (1.0 / w)], axis=0)
        return mean - px_s[c, POOL_PAD:POOL_PAD + tb, :]

    def b_pool(q):
        pooled_s[:, _cols(q)] = jnp.concatenate(
            [pool_slab(q * slabs_per_chunk + i, POOL_WINDOWS[q]) for i in range(slabs_per_chunk)],
            axis=1).astype(BF16)

    def b_group_proj(q):
        z_s[:, _cols(q)] = jnp.dot(pooled_s[:, _cols(q)], wgrp_ref[q], preferred_element_type=F32)

    def b_gate_chunk(q):
        gq_s[q] = jnp.dot(ub_s[...], bwin_ref[:, _cols(n_chunks + q)], preferred_element_type=F32)

    def b_gated_proj(q):
        z1 = (z_s[:, _cols(q)] + bgrp_ref[:, _cols(q)]) * bscale_ref[:, _cols(q)]
        zg_s[:, _cols(q)] = (z1 * _silu_from_half(gq_s[q])).astype(BF16)

    def b_out_proj(q):
        y1 = jnp.dot(zg_s[...], bwout_ref[:, _cols(q)], preferred_element_type=F32)
        y_ref[0, :, _cols(q)] = h1_in[:, _cols(q)] + y1

    def b_final_norm():
        y_ref[0] = _rmsnorm(y_ref[0], fn_ref[...])

    stages["aN"] = a_norm
    stages["bN"] = b_norm
    stages["bF"] = b_final_norm
    for q in range(n_chunks):
        for name, fn in (("aX", a_xb_chunk), ("aG", a_gate_chunk), ("aH", a_gated_out),
                         ("aO", a_out_proj), ("bX", b_xb_chunk), ("bP", b_pool),
                         ("bM", b_group_proj), ("bG", b_gate_chunk), ("bZ", b_gated_proj),
                         ("bO", b_out_proj)):
            stages[f"{name}{q}"] = functools.partial(fn, q)
    for c in range(n_slabs):
        for name, fn in (("aC", a_conv), ("aR", a_gate_proj), ("aK", a_coeffs), ("aS", a_scan)):
            stages[f"{name}{c}"] = functools.partial(fn, c)

    order = _prompt_stage_order(n_chunks, slabs_per_chunk)
    assert sorted(order) == sorted(stages), "every stage is issued exactly once"
    for name in order:
        stages[name]()
    h_s[...] = jnp.concatenate(carries, axis=1)

    @pl.when(jnp.logical_and(ta == blocks_per_seq - 1, n < n_blocks))
    def _():
        for c in range(n_slabs):
            conv_ref[0, :, c * LANES:(c + 1) * LANES] = ext_s[c, tb + CONV_PAD - (CONV_W - 1):tb + CONV_PAD, :]
        lru_ref[0] = h_s[...]

    @pl.when(jnp.logical_and(tbk == blocks_per_seq - 1, n > 0))
    def _():
        for c in range(n_slabs):
            pool_ref[0, :, c * LANES:(c + 1) * LANES] = px_s[c, tb + POOL_PAD - POOL_PAST:tb + POOL_PAD, :]

    ext_s[:, 0:CONV_PAD, :] = ext_s[:, tb:tb + CONV_PAD, :]
    px_s[:, 0:POOL_PAD, :] = px_s[:, tb:tb + POOL_PAD, :]


def _sample_kernel(x_ref, sconv_ref, slru_ref, spool_ref,
                   an_ref, awin_ref, cw_ref, cb_ref, wri_ref, hbr_ref, hbi_ref, lam_ref,
                   awout_ref, bn_ref, bwin_ref, wgrp_ref, bgrp_ref, bscale_ref, bwout_ref, fn_ref,
                   y_ref, conv_ref, lru_ref, pool_ref):
    nb, d = slru_ref.shape
    ts = x_ref.shape[1] // d
    lb = d // N_LRU_BLOCKS
    gw = d // len(POOL_WINDOWS)

    def step(ref, k):
        return ref[:, k * d:(k + 1) * d]

    def slab(v, i):
        return v[i * nb:(i + 1) * nb]

    x = jnp.concatenate([step(x_ref, i) for i in range(ts)], axis=0)

    u = _rmsnorm(x, an_ref[...]).astype(BF16)
    proj = jnp.dot(u, awin_ref[...], preferred_element_type=F32)
    xb = proj[:, :d]
    ext = [step(sconv_ref, k) for k in range(CONV_W - 1)] + [slab(xb, i) for i in range(ts)]
    hx_slabs = []
    for i in range(ts):
        acc = cb_ref[...]
        for k in range(CONV_W):
            acc = acc + ext[i + k] * cw_ref[k:k + 1, :]
        hx_slabs.append(acc)
    hx = jnp.concatenate(hx_slabs, axis=0)
    hx_bf16 = hx.astype(BF16)
    zr, zi = [], []
    for n in range(N_LRU_BLOCKS):
        z = jnp.dot(hx_bf16[:, n * lb:(n + 1) * lb], wri_ref[n], preferred_element_type=F32)
        zr.append(z[:, :lb])
        zi.append(z[:, lb:])
    a, mult, gated = _lru_terms(hx, jnp.concatenate(zr, axis=1), jnp.concatenate(zi, axis=1),
                                hbr_ref[...], hbi_ref[...], _half_neg_c_softplus_neg(lam_ref[...]))
    h = slru_ref[...]
    hs_slabs = []
    for i in range(ts):
        m = slab(mult, i)
        if PAST_LEN + i == 0:
            m = jnp.ones_like(m)
        h = slab(a, i) * h + m * slab(gated, i)
        hs_slabs.append(h)
    lru_ref[...] = h
    hs = jnp.concatenate(hs_slabs, axis=0)
    y0 = jnp.dot((hs * _silu_from_half(proj[:, d:])).astype(BF16), awout_ref[...],
                 preferred_element_type=F32)
    h1 = x + y0
    for k in range(CONV_W - 1):
        conv_ref[:, k * d:(k + 1) * d] = ext[ts + k]

    u1 = _rmsnorm(h1, bn_ref[...]).astype(BF16)
    proj1 = jnp.dot(u1, bwin_ref[...], preferred_element_type=F32)
    xb1 = proj1[:, :d]
    ext2 = [step(spool_ref, k) for k in range(POOL_PAST)] + [slab(xb1, i) for i in range(ts)]
    pooled_slabs = []
    for i in range(ts):
        means = []
        for gi, w in enumerate(POOL_WINDOWS):
            lanes = slice(gi * gw, (gi + 1) * gw)
            s = ext2[POOL_PAST + i][:, lanes]
            for j in range(1, w):
                s = s + ext2[POOL_PAST + i - j][:, lanes]
            means.append(s * (1.0 / min(PAST_LEN + i + 1, w)))
        pooled_slabs.append(jnp.concatenate(means, axis=1) - ext2[POOL_PAST + i])
    pooled = jnp.concatenate(pooled_slabs, axis=0).astype(BF16)
    z1 = jnp.concatenate(
        [jnp.dot(pooled[:, g * gw:(g + 1) * gw], wgrp_ref[g], preferred_element_type=F32)
         for g in range(len(POOL_WINDOWS))], axis=1)
    z1 = (z1 + bgrp_ref[...]) * bscale_ref[...]
    y1 = jnp.dot((z1 * _silu_from_half(proj1[:, d:])).astype(BF16), bwout_ref[...],
                 preferred_element_type=F32)
    h2 = h1 + y1
    y = _rmsnorm(h2, fn_ref[...])
    for i in range(ts):
        y_ref[:, i * d:(i + 1) * d] = slab(y, i)
    for k in range(POOL_PAST):
        pool_ref[:, k * d:(k + 1) * d] = ext2[ts + k]


def _whole(shape):
    return pl.BlockSpec(shape, lambda *_: (0,) * len(shape), pipeline_mode=pl.Buffered(1))


def _whole_out(shape):
    return pl.BlockSpec(shape, lambda *_: (0,) * len(shape))


def _half_gate_columns(w_in, d):
    return jnp.concatenate([w_in[:, :d], 0.5 * w_in[:, d:]], axis=1).astype(BF16)


def kernel(x_prompt, x_sample, state_conv, state_lru, state_pool, a_norm, a_w_in, a_conv_w, a_conv_b, a_w_r, a_b_r, a_w_i, a_b_i, a_lam, a_w_out, b_norm, b_w_in, b_w_grp, b_b_grp, b_scale, b_w_out, final_norm):
    bp, tp, d = x_prompt.shape
    bs, ts, _ = x_sample.shape
    tb = PROMPT_BLOCK_ROWS
    assert a_norm.shape[0] == 1 and b_norm.shape[0] == 1, "one layer of each mixer type"
    assert d % MXU_COLS == 0 and d // N_LRU_BLOCKS == LANES
    assert tp % tb == 0 and tb % (SUBLANES * SUBLANES) == 0 and tb >= 2 * SUBLANES
    assert CONV_W - 1 <= ts <= POOL_PAST

    row = lambda v: v.reshape(1, d)
    wri = jnp.concatenate([a_w_r[0], a_w_i[0]], axis=-1).astype(BF16)
    weights = (row(a_norm[0]), _half_gate_columns(a_w_in[0], d), 0.5 * a_conv_w[0],
               row(0.5 * a_conv_b[0]), wri, row(0.5 * a_b_r[0]), row(0.5 * a_b_i[0]), row(a_lam[0]),
               a_w_out[0].astype(BF16), row(b_norm[0]), _half_gate_columns(b_w_in[0], d),
               b_w_grp[0].astype(BF16), row(b_b_grp[0]), row(b_scale[0]), b_w_out[0].astype(BF16),
               row(final_norm))
    weight_specs = [_whole(w.shape) for w in weights]

    n_slabs = d // LANES
    n_chunks = d // MXU_COLS
    seg_rows = SUBLANES * (tb // SUBLANES + 4)
    nt = tp // tb
    n_blocks = bp * nt
    blk0 = lambda n: jnp.minimum(n, n_blocks - 1)
    blk1 = lambda n: jnp.maximum(n - 1, 0)
    y_p, conv_p, lru_p, pool_p = pl.pallas_call(
        functools.partial(_prompt_kernel, blocks_per_seq=nt),
        grid=(n_blocks + 1,),
        in_specs=[pl.BlockSpec((1, tb, d), lambda n: (blk0(n) // nt, blk0(n) % nt, 0))] + weight_specs,
        out_specs=[pl.BlockSpec((1, tb, d), lambda n: (blk1(n) // nt, blk1(n) % nt, 0)),
                   pl.BlockSpec((1, CONV_W - 1, d), lambda n: (blk0(n) // nt, 0, 0)),
                   pl.BlockSpec((1, 1, d), lambda n: (blk0(n) // nt, 0, 0)),
                   pl.BlockSpec((1, POOL_PAST, d), lambda n: (blk1(n) // nt, 0, 0))],
        out_shape=[jax.ShapeDtypeStruct((bp, tp, d), F32),
                   jax.ShapeDtypeStruct((bp, CONV_W - 1, d), F32),
                   jax.ShapeDtypeStruct((bp, 1, d), F32),
                   jax.ShapeDtypeStruct((bp, POOL_PAST, d), F32)],
        scratch_shapes=[pltpu.VMEM((n_slabs, CONV_PAD + tb, LANES), F32),
                        pltpu.VMEM((n_slabs, seg_rows, LANES), F32),
                        pltpu.VMEM((n_slabs, seg_rows, LANES), F32),
                        pltpu.VMEM((n_slabs, seg_rows, LANES), F32),
                        pltpu.VMEM((1, d), F32),
                        pltpu.VMEM((2, tb, d), F32),
                        pltpu.VMEM((n_slabs, POOL_PAD + tb, LANES), F32),
                        pltpu.VMEM((2, MXU_COLS // LANES, POOL_PAD + tb, LANES), F32),
                        pltpu.VMEM((tb, d), BF16),
                        pltpu.VMEM((2, tb, 2 * LANES), F32),
                        pltpu.VMEM((n_chunks, tb, MXU_COLS), F32),
                        pltpu.VMEM((tb, d), BF16),
                        pltpu.VMEM((tb, d), BF16),
                        pltpu.VMEM((tb, d), BF16),
                        pltpu.VMEM((tb, d), F32),
                        pltpu.VMEM((n_chunks, tb, MXU_COLS), F32),
                        pltpu.VMEM((tb, d), BF16)],
        compiler_params=pltpu.CompilerParams(
            vmem_limit_bytes=VMEM_LIMIT_BYTES,
            dimension_semantics=("arbitrary",)),
        name="prompt_step",
    )(x_prompt, *weights)

    xs = x_sample.reshape(bs, ts * d)
    sconv = state_conv[0].reshape(bs, (CONV_W - 1) * d)
    spool = state_pool[0].reshape(bs, POOL_PAST * d)
    y_s, conv_s, lru_s, pool_s = pl.pallas_call(
        _sample_kernel,
        grid=(1,),
        in_specs=[_whole(xs.shape), _whole(sconv.shape), _whole((bs, d)), _whole(spool.shape)]
        + weight_specs,
        out_specs=[_whole_out(xs.shape), _whole_out(sconv.shape), _whole_out((bs, d)),
                   _whole_out(spool.shape)],
        out_shape=[jax.ShapeDtypeStruct(xs.shape, F32),
                   jax.ShapeDtypeStruct(sconv.shape, F32),
                   jax.ShapeDtypeStruct((bs, d), F32),
                   jax.ShapeDtypeStruct(spool.shape, F32)],
        compiler_params=pltpu.CompilerParams(vmem_limit_bytes=VMEM_LIMIT_BYTES),
        name="sample_step",
    )(xs, sconv, state_lru[0], spool, *weights)

    return (y_p, y_s.reshape(bs, ts, d),
            conv_p[None], lru_p.reshape(1, bp, d), pool_p[None],
            conv_s.reshape(1, bs, CONV_W - 1, d), lru_s[None], pool_s.reshape(1, bs, POOL_PAST, d))
```

```python
import functools

import jax
import jax.numpy as jnp
from jax import lax
from jax.experimental import pallas as pl
from jax.experimental.pallas import tpu as pltpu

PAST_LEN = 16384
N_LRU_BLOCKS = 8
CONV_W = 4
LRU_C = 8.0
POOL_WINDOWS = (2, 4, 8, 16)
POOL_PAST = max(POOL_WINDOWS) - 1
EPS = 1e-6

LANES = 128
SUBLANES = 8
MXU_COLS = 256
CONV_PAD = SUBLANES
POOL_LEVELS = max(POOL_WINDOWS).bit_length() - 1
POOL_PAD = POOL_LEVELS * SUBLANES
PROMPT_BLOCK_ROWS = 512
VMEM_LIMIT_BYTES = 56 * 1024 * 1024

F32 = jnp.float32
BF16 = jnp.bfloat16


def _rmsnorm(x, g):
    return x * lax.rsqrt(jnp.mean(x * x, axis=-1, keepdims=True) + EPS) * g


def _silu_from_half(hg):
    return hg + hg * jnp.tanh(hg)


def _half_neg_c_softplus_neg(lam):
    z = -lam
    return (-0.5 * LRU_C) * (jnp.maximum(z, 0.0) + jnp.log1p(jnp.exp(-jnp.abs(z))))


def _lru_terms(hx, zr, zi, hbr, hbi, hn):
    log_a = hn + hn * jnp.tanh(zr + hbr)
    a = jnp.exp(log_a)
    y = jnp.tanh(log_a) * (-1.0 - a * a)
    mult = jnp.where(y > 0.0, y * lax.rsqrt(y), 0.0)
    gated = hx + hx * jnp.tanh(zi + hbi)
    return a, mult, gated


def _lane_slab(ref, c):
    return ref[:, c * LANES:(c + 1) * LANES]


def _cols(q):
    return slice(q * MXU_COLS, (q + 1) * MXU_COLS)


def _prompt_stage_order(n_chunks, slabs_per_chunk):
    assert (n_chunks, slabs_per_chunk) == (4, 2)
    return ("aX0 bN aX1 aC0 bX0 aC1 aC2 aC3 aR0 aR1 aK0 aX2 aS0 aK1 bX1 aS1 bP0 aR2 aC4 aR3 bM0 aC5 aX3 "
            "aK2 aS2 bX2 aK3 aS3 aR4 bP1 aR5 bM1 aC6 aG0 aC7 aK4 bX3 aS4 aK5 aR6 aR7 aS5 bP2 bM2 aK6 aG1 aS6 "
            "aK7 bG0 aS7 bP3 bM3 aH0 aG2 aH1 bZ0 bG1 aH2 aG3 bZ1 bG2 aH3 bG3 bZ2 aO0 bZ3 aO1 aN bO0 bO1 bO2 "
            "bO3 aO2 bF aO3").split()


def _prompt_kernel(x_ref, xnext_ref, an_ref, awin_ref, cw_ref, cb_ref, wri_ref, hbr_ref, hbi_ref, lam_ref,
                   awout_ref, bn_ref, bwin_ref, wgrp_ref, bgrp_ref, bscale_ref, bwout_ref, fn_ref,
                   y_ref, conv_ref, lru_ref, pool_ref,
                   ext_s, sa_s, sb_s, sh_s, h_s, h1_s, px_s, pt_s,
                   ua_s, zr_s, ga_s, hsg_s, ub_s, pooled_s, z_s, gq_s, zg_s, *, blocks_per_seq):
    n = pl.program_id(0)
    n_blocks = pl.num_programs(0) - 1
    ta = jnp.minimum(n, n_blocks - 1) % blocks_per_seq
    tbk = jnp.maximum(n - 1, 0) % blocks_per_seq
    tb, d = x_ref.shape[1], x_ref.shape[2]
    n_slabs = d // LANES
    n_chunks = d // MXU_COLS
    slabs_per_chunk = MXU_COLS // LANES
    assert d // len(POOL_WINDOWS) == MXU_COLS, "one pooling group per matmul column chunk"
    seg = tb // SUBLANES
    pitch = seg + 4

    ua_cur = ua_s.at[n % 2]
    ua_next = ua_s.at[(n + 1) % 2]

    @pl.when(n == 0)
    def _():
        h1_s[...] = jnp.zeros_like(h1_s)
        ua_cur[...] = _rmsnorm(x_ref[0], an_ref[...]).astype(BF16)

    @pl.when(ta == 0)
    def _():
        ext_s[:, 0:CONV_PAD, :] = jnp.zeros((n_slabs, CONV_PAD, LANES), F32)
        h_s[...] = jnp.zeros_like(h_s)

    @pl.when(tbk == 0)
    def _():
        px_s[:, 0:POOL_PAD, :] = jnp.zeros((n_slabs, POOL_PAD, LANES), F32)

    h1_in = h1_s.at[(n + 1) % 2]
    h1_out = h1_s.at[n % 2]
    sub = lax.broadcasted_iota(jnp.int32, (SUBLANES, 1), 0)
    first_token = jnp.logical_and(ta == 0, sub == 0)
    row16 = lax.broadcasted_iota(jnp.int32, (2 * SUBLANES, 1), 0)
    pos16 = tbk * tb + row16
    hn_row = _half_neg_c_softplus_neg(lam_ref[...])
    held = {}
    carries = [None] * n_slabs
    stages = {}

    def a_norm():
        ua_next[...] = _rmsnorm(xnext_ref[0], an_ref[...]).astype(BF16)

    def a_xb_chunk(q):
        pq = jnp.dot(ua_cur[...], awin_ref[:, _cols(q)], preferred_element_type=F32)
        for i in range(slabs_per_chunk):
            ext_s[q * slabs_per_chunk + i, CONV_PAD:CONV_PAD + tb, :] = pq[:, i * LANES:(i + 1) * LANES]

    def a_gate_chunk(q):
        ga_s[q] = jnp.dot(ua_cur[...], awin_ref[:, _cols(n_chunks + q)], preferred_element_type=F32)

    def a_conv(c):
        hx = (ext_s[c, CONV_PAD:CONV_PAD + tb, :] * _lane_slab(cw_ref, c)[CONV_W - 1:CONV_W]
              + _lane_slab(cb_ref, c))
        for k in range(CONV_W - 1):
            hx = hx + ext_s[c, pl.ds(CONV_PAD - (CONV_W - 1) + k, tb), :] * _lane_slab(cw_ref, c)[k:k + 1]
        held["hx", c] = hx
        held["hxb", c] = hx.astype(BF16)

    def a_gate_proj(c):
        zr_s[c % 2] = jnp.dot(held.pop(("hxb", c)), wri_ref[c], preferred_element_type=F32)

    def a_coeffs(c):
        a, mult, gated = _lru_terms(held.pop(("hx", c)), zr_s[c % 2, :, :LANES], zr_s[c % 2, :, LANES:],
                                    _lane_slab(hbr_ref, c), _lane_slab(hbi_ref, c),
                                    hn_row[:, c * LANES:(c + 1) * LANES])
        mult = jnp.concatenate(
            [jnp.where(first_token, 1.0, mult[0:SUBLANES]), mult[SUBLANES:]], axis=0)
        bterm = mult * gated
        for j in range(SUBLANES):
            sa_s[c, pl.ds(j * pitch, seg), :] = a[j * seg:(j + 1) * seg]
            sb_s[c, pl.ds(j * pitch, seg), :] = bterm[j * seg:(j + 1) * seg]

    def a_scan(c):
        hl = sb_s[c, pl.ds(0, SUBLANES, stride=pitch), :]
        pp = sa_s[c, pl.ds(0, SUBLANES, stride=pitch), :]
        for i in range(1, seg):
            av = sa_s[c, pl.ds(i, SUBLANES, stride=pitch), :]
            hl = av * hl + sb_s[c, pl.ds(i, SUBLANES, stride=pitch), :]
            pp = av * pp
        h_in = jnp.broadcast_to(_lane_slab(h_s, c), (SUBLANES, LANES))
        cin = h_in
        for _ in range(SUBLANES - 1):
            cin = jnp.where(sub == 0, h_in, pltpu.roll(hl + pp * cin, 1, 0))
        carries[c] = (hl + pp * cin)[SUBLANES - 1:SUBLANES]
        h = cin
        for i in range(seg):
            h = sa_s[c, pl.ds(i, SUBLANES, stride=pitch), :] * h + sb_s[c, pl.ds(i, SUBLANES, stride=pitch), :]
            sh_s[c, pl.ds(i, SUBLANES, stride=pitch), :] = h

    def a_gated_out(q):
        hs = jnp.concatenate(
            [jnp.concatenate([sh_s[c, pl.ds(j * pitch, seg), :] for j in range(SUBLANES)], axis=0)
             for c in range(q * slabs_per_chunk, (q + 1) * slabs_per_chunk)], axis=1)
        hsg_s[:, _cols(q)] = (hs * _silu_from_half(ga_s[q])).astype(BF16)

    def a_out_proj(q):
        y0 = jnp.dot(hsg_s[...], awout_ref[:, _cols(q)], preferred_element_type=F32)
        h1_out[:, _cols(q)] = x_ref[0, :, _cols(q)] + y0

    def b_norm():
        ub_s[...] = _rmsnorm(h1_in[...], bn_ref[...]).astype(BF16)

    def b_xb_chunk(q):
        pq = jnp.dot(ub_s[...], bwin_ref[:, _cols(q)], preferred_element_type=F32)
        for i in range(slabs_per_chunk):
            px_s[q * slabs_per_chunk + i, POOL_PAD:POOL_PAD + tb, :] = pq[:, i * LANES:(i + 1) * LANES]

    def pool_slab(c, w):
        levels = w.bit_length() - 1
        tmp = c % slabs_per_chunk
        src = px_s.at[c]
        s = None
        for k in range(levels):
            back = (levels - 1 - k) * SUBLANES
            rows = tb + back
            s = (src[pl.ds(POOL_PAD - back, rows), :]
                 + src[pl.ds(POOL_PAD - back - (1 << k), rows), :])
            if k < levels - 1:
                pt_s[k % 2, tmp, POOL_PAD - back:POOL_PAD + tb, :] = s
                src = pt_s.at[k % 2, tmp]
        inv_cnt = 1.0 / jnp.minimum(pos16 + 1, w).astype(F32)
        mean = jnp.concatenate([s[0:2 * SUBLANES] * inv_cnt, s[2 * SUBLANES:] * (1.0 / w)], axis=0)
        return mean - px_s[c, POOL_PAD:POOL_PAD + tb, :]

    def b_pool(q):
        pooled_s[:, _cols(q)] = jnp.concatenate(
            [pool_slab(q * slabs_per_chunk + i, POOL_WINDOWS[q]) for i in range(slabs_per_chunk)],
            axis=1).astype(BF16)

    def b_group_proj(q):
        z_s[:, _cols(q)] = jnp.dot(pooled_s[:, _cols(q)], wgrp_ref[q], preferred_element_type=F32)

    def b_gate_chunk(q):
        gq_s[q] = jnp.dot(ub_s[...], bwin_ref[:, _cols(n_chunks + q)], preferred_element_type=F32)

    def b_gated_proj(q):
        z1 = (z_s[:, _cols(q)] + bgrp_ref[:, _cols(q)]) * bscale_ref[:, _cols(q)]
        zg_s[:, _cols(q)] = (z1 * _silu_from_half(gq_s[q])).astype(BF16)

    def b_out_proj(q):
        y1 = jnp.dot(zg_s[...], bwout_ref[:, _cols(q)], preferred_element_type=F32)
        y_ref[0, :, _cols(q)] = h1_in[:, _cols(q)] + y1

    def b_final_norm():
        y_ref[0] = _rmsnorm(y_ref[0], fn_ref[...])

    stages["aN"] = a_norm
    stages["bN"] = b_norm
    stages["bF"] = b_final_norm
    for q in range(n_chunks):
        for name, fn in (("aX", a_xb_chunk), ("aG", a_gate_chunk), ("aH", a_gated_out),
                         ("aO", a_out_proj), ("bX", b_xb_chunk), ("bP", b_pool),
                         ("bM", b_group_proj), ("bG", b_gate_chunk), ("bZ", b_gated_proj),
                         ("bO", b_out_proj)):
            stages[f"{name}{q}"] = functools.partial(fn, q)
    for c in range(n_slabs):
        for name, fn in (("aC", a_conv), ("aR", a_gate_proj), ("aK", a_coeffs), ("aS", a_scan)):
            stages[f"{name}{c}"] = functools.partial(fn, c)

    order = _prompt_stage_order(n_chunks, slabs_per_chunk)
    assert sorted(order) == sorted(stages), "every stage is issued exactly once"
    for name in order:
        stages[name]()
    h_s[...] = jnp.concatenate(carries, axis=1)

    @pl.when(jnp.logical_and(ta == blocks_per_seq - 1, n < n_blocks))
    def _():
        for c in range(n_slabs):
            conv_ref[0, :, c * LANES:(c + 1) * LANES] = ext_s[c, tb + CONV_PAD - (CONV_W - 1):tb + CONV_PAD, :]
        lru_ref[0] = h_s[...]

    @pl.when(jnp.logical_and(tbk == blocks_per_seq - 1, n > 0))
    def _():
        for c in range(n_slabs):
            pool_ref[0, :, c * LANES:(c + 1) * LANES] = px_s[c, tb + POOL_PAD - POOL_PAST:tb + POOL_PAD, :]

    ext_s[:, 0:CONV_PAD, :] = ext_s[:, tb:tb + CONV_PAD, :]
    px_s[:, 0:POOL_PAD, :] = px_s[:, tb:tb + POOL_PAD, :]


def _sample_kernel(x_ref, sconv_ref, slru_ref, spool_ref,
                   an_ref, awin_ref, cw_ref, cb_ref, wri_ref, hbr_ref, hbi_ref, lam_ref,
                   awout_ref, bn_ref, bwin_ref, wgrp_ref, bgrp_ref, bscale_ref, bwout_ref, fn_ref,
                   y_ref, conv_ref, lru_ref, pool_ref):
    ts, nb, d = x_ref.shape
    lb = d // N_LRU_BLOCKS
    gw = d // len(POOL_WINDOWS)

    def step(ref, k):
        return ref[k]

    def slab(v, i):
        return v[i * nb:(i + 1) * nb]

    x = x_ref[...].reshape(ts * nb, d)

    u = _rmsnorm(x, an_ref[...]).astype(BF16)
    proj = jnp.dot(u, awin_ref[...], preferred_element_type=F32)
    xb = proj[:, :d]
    ext = [step(sconv_ref, k) for k in range(CONV_W - 1)] + [slab(xb, i) for i in range(ts)]
    hx_slabs = []
    for i in range(ts):
        acc = cb_ref[...]
        for k in range(CONV_W):
            acc = acc + ext[i + k] * cw_ref[k:k + 1, :]
        hx_slabs.append(acc)
    hx = jnp.concatenate(hx_slabs, axis=0)
    hx_bf16 = hx.astype(BF16)
    zr, zi = [], []
    for n in range(N_LRU_BLOCKS):
        z = jnp.dot(hx_bf16[:, n * lb:(n + 1) * lb], wri_ref[n], preferred_element_type=F32)
        zr.append(z[:, :lb])
        zi.append(z[:, lb:])
    a, mult, gated = _lru_terms(hx, jnp.concatenate(zr, axis=1), jnp.concatenate(zi, axis=1),
                                hbr_ref[...], hbi_ref[...], _half_neg_c_softplus_neg(lam_ref[...]))
    h = slru_ref[...]
    hs_slabs = []
    for i in range(ts):
        m = slab(mult, i)
        if PAST_LEN + i == 0:
            m = jnp.ones_like(m)
        h = slab(a, i) * h + m * slab(gated, i)
        hs_slabs.append(h)
    lru_ref[...] = h
    hs = jnp.concatenate(hs_slabs, axis=0)
    y0 = jnp.dot((hs * _silu_from_half(proj[:, d:])).astype(BF16), awout_ref[...],
                 preferred_element_type=F32)
    h1 = x + y0
    for k in range(CONV_W - 1):
        conv_ref[k] = ext[ts + k]

    u1 = _rmsnorm(h1, bn_ref[...]).astype(BF16)
    proj1 = jnp.dot(u1, bwin_ref[...], preferred_element_type=F32)
    xb1 = proj1[:, :d]
    ext2 = [step(spool_ref, k) for k in range(POOL_PAST)] + [slab(xb1, i) for i in range(ts)]
    pooled_slabs = []
    for i in range(ts):
        means = []
        for gi, w in enumerate(POOL_WINDOWS):
            lanes = slice(gi * gw, (gi + 1) * gw)
            s = ext2[POOL_PAST + i][:, lanes]
            for j in range(1, w):
                s = s + ext2[POOL_PAST + i - j][:, lanes]
            means.append(s * (1.0 / min(PAST_LEN + i + 1, w)))
        pooled_slabs.append(jnp.concatenate(means, axis=1) - ext2[POOL_PAST + i])
    pooled = jnp.concatenate(pooled_slabs, axis=0).astype(BF16)
    z1 = jnp.concatenate(
        [jnp.dot(pooled[:, g * gw:(g + 1) * gw], wgrp_ref[g], preferred_element_type=F32)
         for g in range(len(POOL_WINDOWS))], axis=1)
    z1 = (z1 + bgrp_ref[...]) * bscale_ref[...]
    y1 = jnp.dot((z1 * _silu_from_half(proj1[:, d:])).astype(BF16), bwout_ref[...],
                 preferred_element_type=F32)
    h2 = h1 + y1
    y_ref[...] = _rmsnorm(h2, fn_ref[...]).reshape(ts, nb, d)
    for k in range(POOL_PAST):
        pool_ref[k] = ext2[ts + k]


def _whole(shape):
    return pl.BlockSpec(shape, lambda *_: (0,) * len(shape), pipeline_mode=pl.Buffered(1))


def _whole_out(shape):
    return pl.BlockSpec(shape, lambda *_: (0,) * len(shape))


def _half_gate_columns(w_in, d):
    col = lax.broadcasted_iota(jnp.int32, (1, w_in.shape[1]), 1)
    return (w_in * jnp.where(col < d, 1.0, 0.5)).astype(BF16)


def kernel(x_prompt, x_sample, state_conv, state_lru, state_pool, a_norm, a_w_in, a_conv_w, a_conv_b, a_w_r, a_b_r, a_w_i, a_b_i, a_lam, a_w_out, b_norm, b_w_in, b_w_grp, b_b_grp, b_scale, b_w_out, final_norm):
    bp, tp, d = x_prompt.shape
    bs, ts, _ = x_sample.shape
    tb = PROMPT_BLOCK_ROWS
    assert a_norm.shape[0] == 1 and b_norm.shape[0] == 1, "one layer of each mixer type"
    assert d % MXU_COLS == 0 and d // N_LRU_BLOCKS == LANES
    assert tp % tb == 0 and tb % (SUBLANES * SUBLANES) == 0 and tb >= 2 * SUBLANES
    assert CONV_W - 1 <= ts <= POOL_PAST

    row = lambda v: v.reshape(1, d)
    wri = jnp.concatenate([a_w_r[0], a_w_i[0]], axis=-1).astype(BF16)
    weights = (row(a_norm[0]), _half_gate_columns(a_w_in[0], d), 0.5 * a_conv_w[0],
               row(0.5 * a_conv_b[0]), wri, row(0.5 * a_b_r[0]), row(0.5 * a_b_i[0]), row(a_lam[0]),
               a_w_out[0].astype(BF16), row(b_norm[0]), _half_gate_columns(b_w_in[0], d),
               b_w_grp[0].astype(BF16), row(b_b_grp[0]), row(b_scale[0]), b_w_out[0].astype(BF16),
               row(final_norm))
    weight_specs = [_whole(w.shape) for w in weights]

    n_slabs = d // LANES
    n_chunks = d // MXU_COLS
    seg_rows = SUBLANES * (tb // SUBLANES + 4)
    nt = tp // tb
    n_blocks = bp * nt
    blk0 = lambda n: jnp.minimum(n, n_blocks - 1)
    blk1 = lambda n: jnp.maximum(n - 1, 0)
    y_p, conv_p, lru_p, pool_p = pl.pallas_call(
        functools.partial(_prompt_kernel, blocks_per_seq=nt),
        grid=(n_blocks + 1,),
        in_specs=[pl.BlockSpec((1, tb, d), lambda n: (blk0(n) // nt, blk0(n) % nt, 0)),
                  pl.BlockSpec((1, tb, d), lambda n: (blk0(n + 1) // nt, blk0(n + 1) % nt, 0))]
        + weight_specs,
        out_specs=[pl.BlockSpec((1, tb, d), lambda n: (blk1(n) // nt, blk1(n) % nt, 0)),
                   pl.BlockSpec((1, CONV_W - 1, d), lambda n: (blk0(n) // nt, 0, 0)),
                   pl.BlockSpec((1, 1, d), lambda n: (blk0(n) // nt, 0, 0)),
                   pl.BlockSpec((1, POOL_PAST, d), lambda n: (blk1(n) // nt, 0, 0))],
        out_shape=[jax.ShapeDtypeStruct((bp, tp, d), F32),
                   jax.ShapeDtypeStruct((bp, CONV_W - 1, d), F32),
                   jax.ShapeDtypeStruct((bp, 1, d), F32),
                   jax.ShapeDtypeStruct((bp, POOL_PAST, d), F32)],
        scratch_shapes=[pltpu.VMEM((n_slabs, CONV_PAD + tb, LANES), F32),
                        pltpu.VMEM((n_slabs, seg_rows, LANES), F32),
                        pltpu.VMEM((n_slabs, seg_rows, LANES), F32),
                        pltpu.VMEM((n_slabs, seg_rows, LANES), F32),
                        pltpu.VMEM((1, d), F32),
                        pltpu.VMEM((2, tb, d), F32),
                        pltpu.VMEM((n_slabs, POOL_PAD + tb, LANES), F32),
                        pltpu.VMEM((2, MXU_COLS // LANES, POOL_PAD + tb, LANES), F32),
                        pltpu.VMEM((2, tb, d), BF16),
                        pltpu.VMEM((2, tb, 2 * LANES), F32),
                        pltpu.VMEM((n_chunks, tb, MXU_COLS), F32),
                        pltpu.VMEM((tb, d), BF16),
                        pltpu.VMEM((tb, d), BF16),
                        pltpu.VMEM((tb, d), BF16),
                        pltpu.VMEM((tb, d), F32),
                        pltpu.VMEM((n_chunks, tb, MXU_COLS), F32),
                        pltpu.VMEM((tb, d), BF16)],
        compiler_params=pltpu.CompilerParams(
            vmem_limit_bytes=VMEM_LIMIT_BYTES,
            dimension_semantics=("arbitrary",)),
        name="prompt_step",
    )(x_prompt, x_prompt, *weights)

    xs = jnp.swapaxes(x_sample, 0, 1)
    sconv = jnp.swapaxes(state_conv[0], 0, 1)
    spool = jnp.swapaxes(state_pool[0], 0, 1)
    y_s, conv_s, lru_s, pool_s = pl.pallas_call(
        _sample_kernel,
        grid=(1,),
        in_specs=[_whole(xs.shape), _whole(sconv.shape), _whole((bs, d)), _whole(spool.shape)]
        + weight_specs,
        out_specs=[_whole_out(xs.shape), _whole_out(sconv.shape), _whole_out((bs, d)),
                   _whole_out(spool.shape)],
        out_shape=[jax.ShapeDtypeStruct(xs.shape, F32),
                   jax.ShapeDtypeStruct(sconv.shape, F32),
                   jax.ShapeDtypeStruct((bs, d), F32),
                   jax.ShapeDtypeStruct(spool.shape, F32)],
        compiler_params=pltpu.CompilerParams(vmem_limit_bytes=VMEM_LIMIT_BYTES),
        name="sample_step",
    )(xs, sconv, state_lru[0], spool, *weights)

    return (y_p, jnp.swapaxes(y_s, 0, 1),
            conv_p[None], lru_p.reshape(1, bp, d), pool_p[None],
            jnp.swapaxes(conv_s, 0, 1)[None], lru_s[None], jnp.swapaxes(pool_s, 0, 1)[None])
```

```python
import functools

import jax
import jax.numpy as jnp
from jax import lax
from jax.experimental import pallas as pl
from jax.experimental.pallas import tpu as pltpu

PAST_LEN = 16384
N_LRU_BLOCKS = 8
CONV_W = 4
LRU_C = 8.0
POOL_WINDOWS = (2, 4, 8, 16)
POOL_PAST = max(POOL_WINDOWS) - 1
EPS = 1e-6

LANES = 128
SUBLANES = 8
MXU_COLS = 256
CONV_PAD = SUBLANES
POOL_LEVELS = max(POOL_WINDOWS).bit_length() - 1
POOL_PAD = POOL_LEVELS * SUBLANES
PROMPT_BLOCK_ROWS = 512
VMEM_LIMIT_BYTES = 56 * 1024 * 1024

F32 = jnp.float32
BF16 = jnp.bfloat16


def _rmsnorm(x, g):
    return x * lax.rsqrt(jnp.mean(x * x, axis=-1, keepdims=True) + EPS) * g


def _silu_from_half(hg):
    return hg + hg * jnp.tanh(hg)


def _half_neg_c_softplus_neg(lam):
    z = -lam
    return (-0.5 * LRU_C) * (jnp.maximum(z, 0.0) + jnp.log1p(jnp.exp(-jnp.abs(z))))


def _lru_terms(hx, zr, zi, hbr, hbi, hn):
    log_a = hn + hn * jnp.tanh(zr + hbr)
    a = jnp.exp(log_a)
    y = jnp.tanh(log_a) * (-1.0 - a * a)
    mult = jnp.where(y > 0.0, y * lax.rsqrt(y), 0.0)
    gated = hx + hx * jnp.tanh(zi + hbi)
    return a, mult, gated


def _lane_slab(ref, c):
    return ref[:, c * LANES:(c + 1) * LANES]


def _cols(q):
    return slice(q * MXU_COLS, (q + 1) * MXU_COLS)


def _prompt_stage_order(n_chunks, slabs_per_chunk):
    assert (n_chunks, slabs_per_chunk) == (4, 2)
    return ("aN aX0 bN aX1 aC0 bX0 aC1 aC2 aC3 aR0 aR1 aK0 aX2 aS0 aK1 bX1 aS1 bP0 aR2 aC4 aR3 bM0 aC5 aX3 "
            "aK2 aS2 bX2 aK3 aS3 aR4 bP1 aR5 bM1 aC6 aG0 aC7 aK4 bX3 aS4 aK5 aR6 aR7 aS5 bP2 bM2 aK6 aG1 aS6 "
            "aK7 bG0 aS7 bP3 bM3 aH0 aG2 aH1 bZ0 bG1 aH2 aG3 bZ1 bG2 aH3 bG3 bZ2 bZ3 bO0 bO1 bO2 bO3 aO0 bF "
            "aO1 aO2 aO3").split()


def _prompt_kernel(x_ref, an_ref, awin_ref, cw_ref, cb_ref, wri_ref, hbr_ref, hbi_ref, lam_ref,
                   awout_ref, bn_ref, bwin_ref, wgrp_ref, bgrp_ref, bscale_ref, bwout_ref, fn_ref,
                   y_ref, conv_ref, lru_ref, pool_ref,
                   ext_s, sa_s, sb_s, sh_s, h_s, h1_s, px_s, pt_s,
                   ua_s, zr_s, ga_s, hsg_s, ub_s, pooled_s, z_s, gq_s, zg_s, *, blocks_per_seq):
    n = pl.program_id(0)
    n_blocks = pl.num_programs(0) - 1
    ta = jnp.minimum(n, n_blocks - 1) % blocks_per_seq
    tbk = jnp.maximum(n - 1, 0) % blocks_per_seq
    tb, d = x_ref.shape[1], x_ref.shape[2]
    n_slabs = d // LANES
    n_chunks = d // MXU_COLS
    slabs_per_chunk = MXU_COLS // LANES
    assert d // len(POOL_WINDOWS) == MXU_COLS, "one pooling group per matmul column chunk"
    seg = tb // SUBLANES
    pitch = seg + 4

    @pl.when(n == 0)
    def _():
        h1_s[...] = jnp.zeros_like(h1_s)

    @pl.when(ta == 0)
    def _():
        ext_s[:, 0:CONV_PAD, :] = jnp.zeros((n_slabs, CONV_PAD, LANES), F32)
        h_s[...] = jnp.zeros_like(h_s)

    @pl.when(tbk == 0)
    def _():
        px_s[:, 0:POOL_PAD, :] = jnp.zeros((n_slabs, POOL_PAD, LANES), F32)

    h1_in = h1_s.at[(n + 1) % 2]
    h1_out = h1_s.at[n % 2]
    sub = lax.broadcasted_iota(jnp.int32, (SUBLANES, 1), 0)
    first_token = jnp.logical_and(ta == 0, sub == 0)
    row16 = lax.broadcasted_iota(jnp.int32, (2 * SUBLANES, 1), 0)
    pos16 = tbk * tb + row16
    hn_row = _half_neg_c_softplus_neg(lam_ref[...])
    held = {}
    carries = [None] * n_slabs
    stages = {}

    def a_norm():
        ua_s[...] = _rmsnorm(x_ref[0], an_ref[...]).astype(BF16)

    def a_xb_chunk(q):
        pq = jnp.dot(ua_s[...], awin_ref[:, _cols(q)], preferred_element_type=F32)
        for i in range(slabs_per_chunk):
            ext_s[q * slabs_per_chunk + i, CONV_PAD:CONV_PAD + tb, :] = pq[:, i * LANES:(i + 1) * LANES]

    def a_gate_chunk(q):
        ga_s[q] = jnp.dot(ua_s[...], awin_ref[:, _cols(n_chunks + q)], preferred_element_type=F32)

    def a_conv(c):
        hx = (ext_s[c, CONV_PAD:CONV_PAD + tb, :] * _lane_slab(cw_ref, c)[CONV_W - 1:CONV_W]
              + _lane_slab(cb_ref, c))
        for k in range(CONV_W - 1):
            hx = hx + ext_s[c, pl.ds(CONV_PAD - (CONV_W - 1) + k, tb), :] * _lane_slab(cw_ref, c)[k:k + 1]
        held["hx", c] = hx
        held["hxb", c] = hx.astype(BF16)

    def a_gate_proj(c):
        zr_s[c % 2] = jnp.dot(held.pop(("hxb", c)), wri_ref[c], preferred_element_type=F32)

    def a_coeffs(c):
        a, mult, gated = _lru_terms(held.pop(("hx", c)), zr_s[c % 2, :, :LANES], zr_s[c % 2, :, LANES:],
                                    _lane_slab(hbr_ref, c), _lane_slab(hbi_ref, c),
                                    hn_row[:, c * LANES:(c + 1) * LANES])
        mult = jnp.concatenate(
            [jnp.where(first_token, 1.0, mult[0:SUBLANES]), mult[SUBLANES:]], axis=0)
        bterm = mult * gated
        for j in range(SUBLANES):
            sa_s[c, pl.ds(j * pitch, seg), :] = a[j * seg:(j + 1) * seg]
            sb_s[c, pl.ds(j * pitch, seg), :] = bterm[j * seg:(j + 1) * seg]

    def a_scan(c):
        hl = sb_s[c, pl.ds(0, SUBLANES, stride=pitch), :]
        pp = sa_s[c, pl.ds(0, SUBLANES, stride=pitch), :]
        for i in range(1, seg):
            av = sa_s[c, pl.ds(i, SUBLANES, stride=pitch), :]
            hl = av * hl + sb_s[c, pl.ds(i, SUBLANES, stride=pitch), :]
            pp = av * pp
        h_in = jnp.broadcast_to(_lane_slab(h_s, c), (SUBLANES, LANES))
        cin = h_in
        for _ in range(SUBLANES - 1):
            cin = jnp.where(sub == 0, h_in, pltpu.roll(hl + pp * cin, 1, 0))
        carries[c] = (hl + pp * cin)[SUBLANES - 1:SUBLANES]
        h = cin
        for i in range(seg):
            h = sa_s[c, pl.ds(i, SUBLANES, stride=pitch), :] * h + sb_s[c, pl.ds(i, SUBLANES, stride=pitch), :]
            sh_s[c, pl.ds(i, SUBLANES, stride=pitch), :] = h

    def a_gated_out(q):
        hs = jnp.concatenate(
            [jnp.concatenate([sh_s[c, pl.ds(j * pitch, seg), :] for j in range(SUBLANES)], axis=0)
             for c in range(q * slabs_per_chunk, (q + 1) * slabs_per_chunk)], axis=1)
        hsg_s[:, _cols(q)] = (hs * _silu_from_half(ga_s[q])).astype(BF16)

    def a_out_proj(q):
        y0 = jnp.dot(hsg_s[...], awout_ref[:, _cols(q)], preferred_element_type=F32)
        h1_out[:, _cols(q)] = x_ref[0, :, _cols(q)] + y0

    def b_norm():
        ub_s[...] = _rmsnorm(h1_in[...], bn_ref[...]).astype(BF16)

    def b_xb_chunk(q):
        pq = jnp.dot(ub_s[...], bwin_ref[:, _cols(q)], preferred_element_type=F32)
        for i in range(slabs_per_chunk):
            px_s[q * slabs_per_chunk + i, POOL_PAD:POOL_PAD + tb, :] = pq[:, i * LANES:(i + 1) * LANES]

    def pool_slab(c, w):
        levels = w.bit_length() - 1
        tmp = c % slabs_per_chunk
        src = px_s.at[c]
        s = None
        for k in range(levels):
            back = (levels - 1 - k) * SUBLANES
            rows = tb + back
            s = (src[pl.ds(POOL_PAD - back, rows), :]
                 + src[pl.ds(POOL_PAD - back - (1 << k), rows), :])
            if k < levels - 1:
                pt_s[k % 2, tmp, POOL_PAD - back:POOL_PAD + tb, :] = s
                src = pt_s.at[k % 2, tmp]
        inv_cnt = 1.0 / jnp.minimum(pos16 + 1, w).astype(F32)
        mean = jnp.concatenate([s[0:2 * SUBLANES] * inv_cnt, s[2 * SUBLANES:] * (1.0 / w)], axis=0)
        return mean - px_s[c, POOL_PAD:POOL_PAD + tb, :]

    def b_pool(q):
        pooled_s[:, _cols(q)] = jnp.concatenate(
            [pool_slab(q * slabs_per_chunk + i, POOL_WINDOWS[q]) for i in range(slabs_per_chunk)],
            axis=1).astype(BF16)

    def b_group_proj(q):
        z_s[:, _cols(q)] = jnp.dot(pooled_s[:, _cols(q)], wgrp_ref[q], preferred_element_type=F32)

    def b_gate_chunk(q):
        gq_s[q] = jnp.dot(ub_s[...], bwin_ref[:, _cols(n_chunks + q)], preferred_element_type=F32)

    def b_gated_proj(q):
        z1 = (z_s[:, _cols(q)] + bgrp_ref[:, _cols(q)]) * bscale_ref[:, _cols(q)]
        zg_s[:, _cols(q)] = (z1 * _silu_from_half(gq_s[q])).astype(BF16)

    def b_out_proj(q):
        y1 = jnp.dot(zg_s[...], bwout_ref[:, _cols(q)], preferred_element_type=F32)
        y_ref[0, :, _cols(q)] = h1_in[:, _cols(q)] + y1

    def b_final_norm():
        y_ref[0] = _rmsnorm(y_ref[0], fn_ref[...])

    stages["aN"] = a_norm
    stages["bN"] = b_norm
    stages["bF"] = b_final_norm
    for q in range(n_chunks):
        for name, fn in (("aX", a_xb_chunk), ("aG", a_gate_chunk), ("aH", a_gated_out),
                         ("aO", a_out_proj), ("bX", b_xb_chunk), ("bP", b_pool),
                         ("bM", b_group_proj), ("bG", b_gate_chunk), ("bZ", b_gated_proj),
                         ("bO", b_out_proj)):
            stages[f"{name}{q}"] = functools.partial(fn, q)
    for c in range(n_slabs):
        for name, fn in (("aC", a_conv), ("aR", a_gate_proj), ("aK", a_coeffs), ("aS", a_scan)):
            stages[f"{name}{c}"] = functools.partial(fn, c)

    order = _prompt_stage_order(n_chunks, slabs_per_chunk)
    assert sorted(order) == sorted(stages), "every stage is issued exactly once"
    for name in order:
        stages[name]()
    h_s[...] = jnp.concatenate(carries, axis=1)

    @pl.when(jnp.logical_and(ta == blocks_per_seq - 1, n < n_blocks))
    def _():
        for c in range(n_slabs):
            conv_ref[0, :, c * LANES:(c + 1) * LANES] = ext_s[c, tb + CONV_PAD - (CONV_W - 1):tb + CONV_PAD, :]
        lru_ref[0] = h_s[...]

    @pl.when(jnp.logical_and(tbk == blocks_per_seq - 1, n > 0))
    def _():
        for c in range(n_slabs):
            pool_ref[0, :, c * LANES:(c + 1) * LANES] = px_s[c, tb + POOL_PAD - POOL_PAST:tb + POOL_PAD, :]

    ext_s[:, 0:CONV_PAD, :] = ext_s[:, tb:tb + CONV_PAD, :]
    px_s[:, 0:POOL_PAD, :] = px_s[:, tb:tb + POOL_PAD, :]


def _sample_kernel(x_ref, sconv_ref, slru_ref, spool_ref,
                   an_ref, awin_ref, cw_ref, cb_ref, wri_ref, hbr_ref, hbi_ref, lam_ref,
                   awout_ref, bn_ref, bwin_ref, wgrp_ref, bgrp_ref, bscale_ref, bwout_ref, fn_ref,
                   y_ref, conv_ref, lru_ref, pool_ref):
    ts, nb, d = x_ref.shape
    lb = d // N_LRU_BLOCKS
    gw = d // len(POOL_WINDOWS)

    def step(ref, k):
        return ref[k]

    def slab(v, i):
        return v[i * nb:(i + 1) * nb]

    x = x_ref[...].reshape(ts * nb, d)

    u = _rmsnorm(x, an_ref[...]).astype(BF16)
    proj = jnp.dot(u, awin_ref[...], preferred_element_type=F32)
    xb = proj[:, :d]
    ext = [step(sconv_ref, k) for k in range(CONV_W - 1)] + [slab(xb, i) for i in range(ts)]
    hx_slabs = []
    for i in range(ts):
        acc = cb_ref[...]
        for k in range(CONV_W):
            acc = acc + ext[i + k] * cw_ref[k:k + 1, :]
        hx_slabs.append(acc)
    hx = jnp.concatenate(hx_slabs, axis=0)
    hx_bf16 = hx.astype(BF16)
    zr, zi = [], []
    for n in range(N_LRU_BLOCKS):
        z = jnp.dot(hx_bf16[:, n * lb:(n + 1) * lb], wri_ref[n], preferred_element_type=F32)
        zr.append(z[:, :lb])
        zi.append(z[:, lb:])
    a, mult, gated = _lru_terms(hx, jnp.concatenate(zr, axis=1), jnp.concatenate(zi, axis=1),
                                hbr_ref[...], hbi_ref[...], _half_neg_c_softplus_neg(lam_ref[...]))
    h = slru_ref[...]
    hs_slabs = []
    for i in range(ts):
        m = slab(mult, i)
        if PAST_LEN + i == 0:
            m = jnp.ones_like(m)
        h = slab(a, i) * h + m * slab(gated, i)
        hs_slabs.append(h)
    lru_ref[...] = h
    hs = jnp.concatenate(hs_slabs, axis=0)
    y0 = jnp.dot((hs * _silu_from_half(proj[:, d:])).astype(BF16), awout_ref[...],
                 preferred_element_type=F32)
    h1 = x + y0
    for k in range(CONV_W - 1):
        conv_ref[k] = ext[ts + k]

    u1 = _rmsnorm(h1, bn_ref[...]).astype(BF16)
    proj1 = jnp.dot(u1, bwin_ref[...], preferred_element_type=F32)
    xb1 = proj1[:, :d]
    ext2 = [step(spool_ref, k) for k in range(POOL_PAST)] + [slab(xb1, i) for i in range(ts)]
    pooled_slabs = []
    for i in range(ts):
        means = []
        for gi, w in enumerate(POOL_WINDOWS):
            lanes = slice(gi * gw, (gi + 1) * gw)
            s = ext2[POOL_PAST + i][:, lanes]
            for j in range(1, w):
                s = s + ext2[POOL_PAST + i - j][:, lanes]
            means.append(s * (1.0 / min(PAST_LEN + i + 1, w)))
        pooled_slabs.append(jnp.concatenate(means, axis=1) - ext2[POOL_PAST + i])
    pooled = jnp.concatenate(pooled_slabs, axis=0).astype(BF16)
    z1 = jnp.concatenate(
        [jnp.dot(pooled[:, g * gw:(g + 1) * gw], wgrp_ref[g], preferred_element_type=F32)
         for g in range(len(POOL_WINDOWS))], axis=1)
    z1 = (z1 + bgrp_ref[...]) * bscale_ref[...]
    y1 = jnp.dot((z1 * _silu_from_half(proj1[:, d:])).astype(BF16), bwout_ref[...],
                 preferred_element_type=F32)
    h2 = h1 + y1
    y_ref[...] = _rmsnorm(h2, fn_ref[...]).reshape(ts, nb, d)
    for k in range(POOL_PAST):
        pool_ref[k] = ext2[ts + k]


def _whole(shape):
    return pl.BlockSpec(shape, lambda *_: (0,) * len(shape), pipeline_mode=pl.Buffered(1))


def _whole_out(shape):
    return pl.BlockSpec(shape, lambda *_: (0,) * len(shape))


def _half_gate_columns(w_in, d):
    return jnp.concatenate([w_in[:, :d], 0.5 * w_in[:, d:]], axis=1).astype(BF16)


def kernel(x_prompt, x_sample, state_conv, state_lru, state_pool, a_norm, a_w_in, a_conv_w, a_conv_b, a_w_r, a_b_r, a_w_i, a_b_i, a_lam, a_w_out, b_norm, b_w_in, b_w_grp, b_b_grp, b_scale, b_w_out, final_norm):
    bp, tp, d = x_prompt.shape
    bs, ts, _ = x_sample.shape
    tb = PROMPT_BLOCK_ROWS
    assert a_norm.shape[0] == 1 and b_norm.shape[0] == 1, "one layer of each mixer type"
    assert d % MXU_COLS == 0 and d // N_LRU_BLOCKS == LANES
    assert tp % tb == 0 and tb % (SUBLANES * SUBLANES) == 0 and tb >= 2 * SUBLANES
    assert CONV_W - 1 <= ts <= POOL_PAST

    row = lambda v: v.reshape(1, d)
    wri = jnp.concatenate([a_w_r[0], a_w_i[0]], axis=-1).astype(BF16)
    weights = (row(a_norm[0]), _half_gate_columns(a_w_in[0], d), 0.5 * a_conv_w[0],
               row(0.5 * a_conv_b[0]), wri, row(0.5 * a_b_r[0]), row(0.5 * a_b_i[0]), row(a_lam[0]),
               a_w_out[0].astype(BF16), row(b_norm[0]), _half_gate_columns(b_w_in[0], d),
               b_w_grp[0].astype(BF16), row(b_b_grp[0]), row(b_scale[0]), b_w_out[0].astype(BF16),
               row(final_norm))
    weight_specs = [_whole(w.shape) for w in weights]

    n_slabs = d // LANES
    n_chunks = d // MXU_COLS
    seg_rows = SUBLANES * (tb // SUBLANES + 4)
    nt = tp // tb
    n_blocks = bp * nt
    blk0 = lambda n: jnp.minimum(n, n_blocks - 1)
    blk1 = lambda n: jnp.maximum(n - 1, 0)
    y_p, conv_p, lru_p, pool_p = pl.pallas_call(
        functools.partial(_prompt_kernel, blocks_per_seq=nt),
        grid=(n_blocks + 1,),
        in_specs=[pl.BlockSpec((1, tb, d), lambda n: (blk0(n) // nt, blk0(n) % nt, 0))] + weight_specs,
        out_specs=[pl.BlockSpec((1, tb, d), lambda n: (blk1(n) // nt, blk1(n) % nt, 0)),
                   pl.BlockSpec((1, CONV_W - 1, d), lambda n: (blk0(n) // nt, 0, 0)),
                   pl.BlockSpec((1, 1, d), lambda n: (blk0(n) // nt, 0, 0)),
                   pl.BlockSpec((1, POOL_PAST, d), lambda n: (blk1(n) // nt, 0, 0))],
        out_shape=[jax.ShapeDtypeStruct((bp, tp, d), F32),
                   jax.ShapeDtypeStruct((bp, CONV_W - 1, d), F32),
                   jax.ShapeDtypeStruct((bp, 1, d), F32),
                   jax.ShapeDtypeStruct((bp, POOL_PAST, d), F32)],
        scratch_shapes=[pltpu.VMEM((n_slabs, CONV_PAD + tb, LANES), F32),
                        pltpu.VMEM((n_slabs, seg_rows, LANES), F32),
                        pltpu.VMEM((n_slabs, seg_rows, LANES), F32),
                        pltpu.VMEM((n_slabs, seg_rows, LANES), F32),
                        pltpu.VMEM((1, d), F32),
                        pltpu.VMEM((2, tb, d), F32),
                        pltpu.VMEM((n_slabs, POOL_PAD + tb, LANES), F32),
                        pltpu.VMEM((2, MXU_COLS // LANES, POOL_PAD + tb, LANES), F32),
                        pltpu.VMEM((tb, d), BF16),
                        pltpu.VMEM((2, tb, 2 * LANES), F32),
                        pltpu.VMEM((n_chunks, tb, MXU_COLS), F32),
                        pltpu.VMEM((tb, d), BF16),
                        pltpu.VMEM((tb, d), BF16),
                        pltpu.VMEM((tb, d), BF16),
                        pltpu.VMEM((tb, d), F32),
                        pltpu.VMEM((n_chunks, tb, MXU_COLS), F32),
                        pltpu.VMEM((tb, d), BF16)],
        compiler_params=pltpu.CompilerParams(
            vmem_limit_bytes=VMEM_LIMIT_BYTES,
            dimension_semantics=("arbitrary",)),
        name="prompt_step",
    )(x_prompt, *weights)

    xs = jnp.swapaxes(x_sample, 0, 1)
    sconv = jnp.swapaxes(state_conv[0], 0, 1)
    spool = jnp.swapaxes(state_pool[0], 0, 1)
    y_s, conv_s, lru_s, pool_s = pl.pallas_call(
        _sample_kernel,
        grid=(1,),
        in_specs=[_whole(xs.shape), _whole(sconv.shape), _whole((bs, d)), _whole(spool.shape)]
        + weight_specs,
        out_specs=[_whole_out(xs.shape), _whole_out(sconv.shape), _whole_out((bs, d)),
                   _whole_out(spool.shape)],
        out_shape=[jax.ShapeDtypeStruct(xs.shape, F32),
                   jax.ShapeDtypeStruct(sconv.shape, F32),
                   jax.ShapeDtypeStruct((bs, d), F32),
                   jax.ShapeDtypeStruct(spool.shape, F32)],
        compiler_params=pltpu.CompilerParams(vmem_limit_bytes=VMEM_LIMIT_BYTES),
        name="sample_step",
    )(xs, sconv, state_lru[0], spool, *weights)

    return (y_p, jnp.swapaxes(y_s, 0, 1),
            conv_p[None], lru_p.reshape(1, bp, d), pool_p[None],
            jnp.swapaxes(conv_s, 0, 1)[None], lru_s[None], jnp.swapaxes(pool_s, 0, 1)[None])
```

```python
import functools

import jax
import jax.numpy as jnp
from jax import lax
from jax.experimental import pallas as pl
from jax.experimental.pallas import tpu as pltpu

PAST_LEN = 16384
N_LRU_BLOCKS = 8
CONV_W = 4
LRU_C = 8.0
POOL_WINDOWS = (2, 4, 8, 16)
POOL_PAST = max(POOL_WINDOWS) - 1
EPS = 1e-6

LANES = 128
SUBLANES = 8
MXU_COLS = 256
CONV_PAD = SUBLANES
POOL_LEVELS = max(POOL_WINDOWS).bit_length() - 1
POOL_PAD = POOL_LEVELS * SUBLANES
PROMPT_BLOCK_ROWS = 512
VMEM_LIMIT_BYTES = 56 * 1024 * 1024

F32 = jnp.float32
BF16 = jnp.bfloat16


def _rmsnorm(x, g):
    return x * lax.rsqrt(jnp.mean(x * x, axis=-1, keepdims=True) + EPS) * g


def _silu_from_half(hg):
    return hg + hg * jnp.tanh(hg)


def _half_neg_c_softplus_neg(lam):
    z = -lam
    return (-0.5 * LRU_C) * (jnp.maximum(z, 0.0) + jnp.log1p(jnp.exp(-jnp.abs(z))))


def _lru_terms(hx, zr, zi, hbr, hbi, hn):
    log_a = hn + hn * jnp.tanh(zr + hbr)
    a = jnp.exp(log_a)
    y = jnp.tanh(log_a) * (-1.0 - a * a)
    mult = jnp.where(y > 0.0, y * lax.rsqrt(y), 0.0)
    gated = hx + hx * jnp.tanh(zi + hbi)
    return a, mult, gated


def _lane_slab(ref, c):
    return ref[:, c * LANES:(c + 1) * LANES]


def _cols(q):
    return slice(q * MXU_COLS, (q + 1) * MXU_COLS)


def _prompt_stage_order(n_chunks, slabs_per_chunk):
    assert (n_chunks, slabs_per_chunk) == (4, 2)
    return ("aN aX0 bN aX1 aC0 bX0 aC1 aC2 aC3 aR0 aR1 aK0 aX2 aS0 aK1 bX1 aS1 bP0 aR2 aC4 aR3 bM0 aC5 aX3 "
            "aK2 aS2 bX2 aK3 aS3 aR4 bP1 aR5 bM1 aC6 aG0 aC7 aK4 bX3 aS4 aK5 aR6 aR7 aS5 bP2 bM2 aK6 aG1 aS6 "
            "aK7 bG0 aS7 bP3 bM3 aH0 aG2 aH1 bZ0 bG1 aH2 aG3 bZ1 bG2 aH3 bG3 bZ2 bZ3 bO0 bO1 bO2 bO3 aO0 bF "
            "aO1 aO2 aO3").split()


def _prompt_kernel(x_ref, an_ref, awin_ref, cw_ref, cb_ref, wri_ref, hbr_ref, hbi_ref, lam_ref,
                   awout_ref, bn_ref, bwin_ref, wgrp_ref, bgrp_ref, bscale_ref, bwout_ref, fn_ref,
                   y_ref, conv_ref, lru_ref, pool_ref,
                   ext_s, sa_s, sb_s, sh_s, h_s, h1_s, px_s, pt_s,
                   ua_s, zr_s, ga_s, hsg_s, ub_s, pooled_s, z_s, gq_s, zg_s, *, blocks_per_seq):
    n = pl.program_id(0)
    n_blocks = pl.num_programs(0) - 1
    ta = jnp.minimum(n, n_blocks - 1) % blocks_per_seq
    tbk = jnp.maximum(n - 1, 0) % blocks_per_seq
    tb, d = x_ref.shape[1], x_ref.shape[2]
    n_slabs = d // LANES
    n_chunks = d // MXU_COLS
    slabs_per_chunk = MXU_COLS // LANES
    assert d // len(POOL_WINDOWS) == MXU_COLS, "one pooling group per matmul column chunk"
    seg = tb // SUBLANES
    pitch = seg + 4

    @pl.when(ta == 0)
    def _():
        ext_s[:, 0:CONV_PAD, :] = jnp.zeros((n_slabs, CONV_PAD, LANES), F32)
        h_s[...] = jnp.zeros_like(h_s)

    @pl.when(tbk == 0)
    def _():
        px_s[:, 0:POOL_PAD, :] = jnp.zeros((n_slabs, POOL_PAD, LANES), F32)

    h1_in = h1_s.at[(n + 1) % 2]
    h1_out = h1_s.at[n % 2]
    sub = lax.broadcasted_iota(jnp.int32, (SUBLANES, 1), 0)
    first_token = jnp.logical_and(ta == 0, sub == 0)
    row16 = lax.broadcasted_iota(jnp.int32, (2 * SUBLANES, 1), 0)
    pos16 = tbk * tb + row16
    hn_row = _half_neg_c_softplus_neg(lam_ref[...])
    held = {}
    carries = [None] * n_slabs
    stages = {}

    def a_norm():
        ua_s[...] = _rmsnorm(x_ref[0], an_ref[...]).astype(BF16)

    def a_xb_chunk(q):
        pq = jnp.dot(ua_s[...], awin_ref[:, _cols(q)], preferred_element_type=F32)
        for i in range(slabs_per_chunk):
            ext_s[q * slabs_per_chunk + i, CONV_PAD:CONV_PAD + tb, :] = pq[:, i * LANES:(i + 1) * LANES]

    def a_gate_chunk(q):
        ga_s[q] = jnp.dot(ua_s[...], awin_ref[:, _cols(n_chunks + q)], preferred_element_type=F32)

    def a_conv(c):
        hx = (ext_s[c, CONV_PAD:CONV_PAD + tb, :] * _lane_slab(cw_ref, c)[CONV_W - 1:CONV_W]
              + _lane_slab(cb_ref, c))
        for k in range(CONV_W - 1):
            hx = hx + ext_s[c, pl.ds(CONV_PAD - (CONV_W - 1) + k, tb), :] * _lane_slab(cw_ref, c)[k:k + 1]
        held["hx", c] = hx
        held["hxb", c] = hx.astype(BF16)

    def a_gate_proj(c):
        zr_s[c % 2] = jnp.dot(held.pop(("hxb", c)), wri_ref[c], preferred_element_type=F32)

    def a_coeffs(c):
        a, mult, gated = _lru_terms(held.pop(("hx", c)), zr_s[c % 2, :, :LANES], zr_s[c % 2, :, LANES:],
                                    _lane_slab(hbr_ref, c), _lane_slab(hbi_ref, c),
                                    hn_row[:, c * LANES:(c + 1) * LANES])
        mult = jnp.concatenate(
            [jnp.where(first_token, 1.0, mult[0:SUBLANES]), mult[SUBLANES:]], axis=0)
        bterm = mult * gated
        for j in range(SUBLANES):
            sa_s[c, pl.ds(j * pitch, seg), :] = a[j * seg:(j + 1) * seg]
            sb_s[c, pl.ds(j * pitch, seg), :] = bterm[j * seg:(j + 1) * seg]

    def a_scan(c):
        hl = sb_s[c, pl.ds(0, SUBLANES, stride=pitch), :]
        pp = sa_s[c, pl.ds(0, SUBLANES, stride=pitch), :]
        for i in range(1, seg):
            av = sa_s[c, pl.ds(i, SUBLANES, stride=pitch), :]
            hl = av * hl + sb_s[c, pl.ds(i, SUBLANES, stride=pitch), :]
            pp = av * pp
        h_in = jnp.broadcast_to(_lane_slab(h_s, c), (SUBLANES, LANES))
        cin = h_in
        for _ in range(SUBLANES - 1):
            cin = jnp.where(sub == 0, h_in, pltpu.roll(hl + pp * cin, 1, 0))
        carries[c] = (hl + pp * cin)[SUBLANES - 1:SUBLANES]
        h = cin
        for i in range(seg):
            h = sa_s[c, pl.ds(i, SUBLANES, stride=pitch), :] * h + sb_s[c, pl.ds(i, SUBLANES, stride=pitch), :]
            sh_s[c, pl.ds(i, SUBLANES, stride=pitch), :] = h

    def a_gated_out(q):
        hs = jnp.concatenate(
            [jnp.concatenate([sh_s[c, pl.ds(j * pitch, seg), :] for j in range(SUBLANES)], axis=0)
             for c in range(q * slabs_per_chunk, (q + 1) * slabs_per_chunk)], axis=1)
        hsg_s[:, _cols(q)] = (hs * _silu_from_half(ga_s[q])).astype(BF16)

    def a_out_proj(q):
        y0 = jnp.dot(hsg_s[...], awout_ref[:, _cols(q)], preferred_element_type=F32)
        h1_out[:, _cols(q)] = x_ref[0, :, _cols(q)] + y0

    def b_norm():
        ub_s[...] = _rmsnorm(h1_in[...], bn_ref[...]).astype(BF16)

    def b_xb_chunk(q):
        pq = jnp.dot(ub_s[...], bwin_ref[:, _cols(q)], preferred_element_type=F32)
        for i in range(slabs_per_chunk):
            px_s[q * slabs_per_chunk + i, POOL_PAD:POOL_PAD + tb, :] = pq[:, i * LANES:(i + 1) * LANES]

    def pool_slab(c, w):
        levels = w.bit_length() - 1
        tmp = c % slabs_per_chunk
        src = px_s.at[c]
        s = None
        for k in range(levels):
            back = (levels - 1 - k) * SUBLANES
            rows = tb + back
            s = (src[pl.ds(POOL_PAD - back, rows), :]
                 + src[pl.ds(POOL_PAD - back - (1 << k), rows), :])
            if k < levels - 1:
                pt_s[k % 2, tmp, POOL_PAD - back:POOL_PAD + tb, :] = s
                src = pt_s.at[k % 2, tmp]
        inv_cnt = 1.0 / jnp.minimum(pos16 + 1, w).astype(F32)
        mean = jnp.concatenate([s[0:2 * SUBLANES] * inv_cnt, s[2 * SUBLANES:] * (1.0 / w)], axis=0)
        return mean - px_s[c, POOL_PAD:POOL_PAD + tb, :]

    def b_pool(q):
        pooled_s[:, _cols(q)] = jnp.concatenate(
            [pool_slab(q * slabs_per_chunk + i, POOL_WINDOWS[q]) for i in range(slabs_per_chunk)],
            axis=1).astype(BF16)

    def b_group_proj(q):
        z_s[:, _cols(q)] = jnp.dot(pooled_s[:, _cols(q)], wgrp_ref[q], preferred_element_type=F32)

    def b_gate_chunk(q):
        gq_s[q] = jnp.dot(ub_s[...], bwin_ref[:, _cols(n_chunks + q)], preferred_element_type=F32)

    def b_gated_proj(q):
        z1 = (z_s[:, _cols(q)] + bgrp_ref[:, _cols(q)]) * bscale_ref[:, _cols(q)]
        zg_s[:, _cols(q)] = (z1 * _silu_from_half(gq_s[q])).astype(BF16)

    def b_out_proj(q):
        y1 = jnp.dot(zg_s[...], bwout_ref[:, _cols(q)], preferred_element_type=F32)
        y_ref[0, :, _cols(q)] = h1_in[:, _cols(q)] + y1

    def b_final_norm():
        y_ref[0] = _rmsnorm(y_ref[0], fn_ref[...])

    stages["aN"] = a_norm
    stages["bN"] = b_norm
    stages["bF"] = b_final_norm
    for q in range(n_chunks):
        for name, fn in (("aX", a_xb_chunk), ("aG", a_gate_chunk), ("aH", a_gated_out),
                         ("aO", a_out_proj), ("bX", b_xb_chunk), ("bP", b_pool),
                         ("bM", b_group_proj), ("bG", b_gate_chunk), ("bZ", b_gated_proj),
                         ("bO", b_out_proj)):
            stages[f"{name}{q}"] = functools.partial(fn, q)
    for c in range(n_slabs):
        for name, fn in (("aC", a_conv), ("aR", a_gate_proj), ("aK", a_coeffs), ("aS", a_scan)):
            stages[f"{name}{c}"] = functools.partial(fn, c)

    order = _prompt_stage_order(n_chunks, slabs_per_chunk)
    assert sorted(order) == sorted(stages), "every stage is issued exactly once"

    def run(names):
        held.clear()
        for name in names:
            stages[name]()
        if any(name.startswith("a") for name in names):
            h_s[...] = jnp.concatenate(carries, axis=1)

    @pl.when(n == 0)
    def _():
        run([name for name in order if name.startswith("a")])

    @pl.when(jnp.logical_and(n > 0, n < n_blocks))
    def _():
        run(order)

    @pl.when(n == n_blocks)
    def _():
        run([name for name in order if name.startswith("b")])

    @pl.when(jnp.logical_and(ta == blocks_per_seq - 1, n < n_blocks))
    def _():
        for c in range(n_slabs):
            conv_ref[0, :, c * LANES:(c + 1) * LANES] = ext_s[c, tb + CONV_PAD - (CONV_W - 1):tb + CONV_PAD, :]
        lru_ref[0] = h_s[...]

    @pl.when(jnp.logical_and(tbk == blocks_per_seq - 1, n > 0))
    def _():
        for c in range(n_slabs):
            pool_ref[0, :, c * LANES:(c + 1) * LANES] = px_s[c, tb + POOL_PAD - POOL_PAST:tb + POOL_PAD, :]

    ext_s[:, 0:CONV_PAD, :] = ext_s[:, tb:tb + CONV_PAD, :]
    px_s[:, 0:POOL_PAD, :] = px_s[:, tb:tb + POOL_PAD, :]


def _sample_kernel(x_ref, sconv_ref, slru_ref, spool_ref,
                   an_ref, awin_ref, cw_ref, cb_ref, wri_ref, hbr_ref, hbi_ref, lam_ref,
                   awout_ref, bn_ref, bwin_ref, wgrp_ref, bgrp_ref, bscale_ref, bwout_ref, fn_ref,
                   y_ref, conv_ref, lru_ref, pool_ref):
    ts, nb, d = x_ref.shape
    lb = d // N_LRU_BLOCKS
    gw = d // len(POOL_WINDOWS)

    def step(ref, k):
        return ref[k]

    def slab(v, i):
        return v[i * nb:(i + 1) * nb]

    x = x_ref[...].reshape(ts * nb, d)

    u = _rmsnorm(x, an_ref[...]).astype(BF16)
    proj = jnp.dot(u, awin_ref[...], preferred_element_type=F32)
    xb = proj[:, :d]
    ext = [step(sconv_ref, k) for k in range(CONV_W - 1)] + [slab(xb, i) for i in range(ts)]
    hx_slabs = []
    for i in range(ts):
        acc = cb_ref[...]
        for k in range(CONV_W):
            acc = acc + ext[i + k] * cw_ref[k:k + 1, :]
        hx_slabs.append(acc)
    hx = jnp.concatenate(hx_slabs, axis=0)
    hx_bf16 = hx.astype(BF16)
    zr, zi = [], []
    for n in range(N_LRU_BLOCKS):
        z = jnp.dot(hx_bf16[:, n * lb:(n + 1) * lb], wri_ref[n], preferred_element_type=F32)
        zr.append(z[:, :lb])
        zi.append(z[:, lb:])
    a, mult, gated = _lru_terms(hx, jnp.concatenate(zr, axis=1), jnp.concatenate(zi, axis=1),
                                hbr_ref[...], hbi_ref[...], _half_neg_c_softplus_neg(lam_ref[...]))
    h = slru_ref[...]
    hs_slabs = []
    for i in range(ts):
        m = slab(mult, i)
        if PAST_LEN + i == 0:
            m = jnp.ones_like(m)
        h = slab(a, i) * h + m * slab(gated, i)
        hs_slabs.append(h)
    lru_ref[...] = h
    hs = jnp.concatenate(hs_slabs, axis=0)
    y0 = jnp.dot((hs * _silu_from_half(proj[:, d:])).astype(BF16), awout_ref[...],
                 preferred_element_type=F32)
    h1 = x + y0
    for k in range(CONV_W - 1):
        conv_ref[k] = ext[ts + k]

    u1 = _rmsnorm(h1, bn_ref[...]).astype(BF16)
    proj1 = jnp.dot(u1, bwin_ref[...], preferred_element_type=F32)
    xb1 = proj1[:, :d]
    ext2 = [step(spool_ref, k) for k in range(POOL_PAST)] + [slab(xb1, i) for i in range(ts)]
    pooled_slabs = []
    for i in range(ts):
        means = []
        for gi, w in enumerate(POOL_WINDOWS):
            lanes = slice(gi * gw, (gi + 1) * gw)
            s = ext2[POOL_PAST + i][:, lanes]
            for j in range(1, w):
                s = s + ext2[POOL_PAST + i - j][:, lanes]
            means.append(s * (1.0 / min(PAST_LEN + i + 1, w)))
        pooled_slabs.append(jnp.concatenate(means, axis=1) - ext2[POOL_PAST + i])
    pooled = jnp.concatenate(pooled_slabs, axis=0).astype(BF16)
    z1 = jnp.concatenate(
        [jnp.dot(pooled[:, g * gw:(g + 1) * gw], wgrp_ref[g], preferred_element_type=F32)
         for g in range(len(POOL_WINDOWS))], axis=1)
    z1 = (z1 + bgrp_ref[...]) * bscale_ref[...]
    y1 = jnp.dot((z1 * _silu_from_half(proj1[:, d:])).astype(BF16), bwout_ref[...],
                 preferred_element_type=F32)
    h2 = h1 + y1
    y_ref[...] = _rmsnorm(h2, fn_ref[...]).reshape(ts, nb, d)
    for k in range(POOL_PAST):
        pool_ref[k] = ext2[ts + k]


def _whole(shape):
    return pl.BlockSpec(shape, lambda *_: (0,) * len(shape), pipeline_mode=pl.Buffered(1))


def _whole_out(shape):
    return pl.BlockSpec(shape, lambda *_: (0,) * len(shape))


def _half_gate_columns(w_in, d):
    return jnp.concatenate([w_in[:, :d], 0.5 * w_in[:, d:]], axis=1).astype(BF16)


def kernel(x_prompt, x_sample, state_conv, state_lru, state_pool, a_norm, a_w_in, a_conv_w, a_conv_b, a_w_r, a_b_r, a_w_i, a_b_i, a_lam, a_w_out, b_norm, b_w_in, b_w_grp, b_b_grp, b_scale, b_w_out, final_norm):
    bp, tp, d = x_prompt.shape
    bs, ts, _ = x_sample.shape
    tb = PROMPT_BLOCK_ROWS
    assert a_norm.shape[0] == 1 and b_norm.shape[0] == 1, "one layer of each mixer type"
    assert d % MXU_COLS == 0 and d // N_LRU_BLOCKS == LANES
    assert tp % tb == 0 and tb % (SUBLANES * SUBLANES) == 0 and tb >= 2 * SUBLANES
    assert CONV_W - 1 <= ts <= POOL_PAST

    row = lambda v: v.reshape(1, d)
    wri = jnp.concatenate([a_w_r[0], a_w_i[0]], axis=-1).astype(BF16)
    weights = (row(a_norm[0]), _half_gate_columns(a_w_in[0], d), 0.5 * a_conv_w[0],
               row(0.5 * a_conv_b[0]), wri, row(0.5 * a_b_r[0]), row(0.5 * a_b_i[0]), row(a_lam[0]),
               a_w_out[0].astype(BF16), row(b_norm[0]), _half_gate_columns(b_w_in[0], d),
               b_w_grp[0].astype(BF16), row(b_b_grp[0]), row(b_scale[0]), b_w_out[0].astype(BF16),
               row(final_norm))
    weight_specs = [_whole(w.shape) for w in weights]

    n_slabs = d // LANES
    n_chunks = d // MXU_COLS
    seg_rows = SUBLANES * (tb // SUBLANES + 4)
    nt = tp // tb
    n_blocks = bp * nt
    blk0 = lambda n: jnp.minimum(n, n_blocks - 1)
    blk1 = lambda n: jnp.maximum(n - 1, 0)
    y_p, conv_p, lru_p, pool_p = pl.pallas_call(
        functools.partial(_prompt_kernel, blocks_per_seq=nt),
        grid=(n_blocks + 1,),
        in_specs=[pl.BlockSpec((1, tb, d), lambda n: (blk0(n) // nt, blk0(n) % nt, 0))] + weight_specs,
        out_specs=[pl.BlockSpec((1, tb, d), lambda n: (blk1(n) // nt, blk1(n) % nt, 0)),
                   pl.BlockSpec((1, CONV_W - 1, d), lambda n: (blk0(n) // nt, 0, 0)),
                   pl.BlockSpec((1, 1, d), lambda n: (blk0(n) // nt, 0, 0)),
                   pl.BlockSpec((1, POOL_PAST, d), lambda n: (blk1(n) // nt, 0, 0))],
        out_shape=[jax.ShapeDtypeStruct((bp, tp, d), F32),
                   jax.ShapeDtypeStruct((bp, CONV_W - 1, d), F32),
                   jax.ShapeDtypeStruct((bp, 1, d), F32),
                   jax.ShapeDtypeStruct((bp, POOL_PAST, d), F32)],
        scratch_shapes=[pltpu.VMEM((n_slabs, CONV_PAD + tb, LANES), F32),
                        pltpu.VMEM((n_slabs, seg_rows, LANES), F32),
                        pltpu.VMEM((n_slabs, seg_rows, LANES), F32),
                        pltpu.VMEM((n_slabs, seg_rows, LANES), F32),
                        pltpu.VMEM((1, d), F32),
                        pltpu.VMEM((2, tb, d), F32),
                        pltpu.VMEM((n_slabs, POOL_PAD + tb, LANES), F32),
                        pltpu.VMEM((2, MXU_COLS // LANES, POOL_PAD + tb, LANES), F32),
                        pltpu.VMEM((tb, d), BF16),
                        pltpu.VMEM((2, tb, 2 * LANES), F32),
                        pltpu.VMEM((n_chunks, tb, MXU_COLS), F32),
                        pltpu.VMEM((tb, d), BF16),
                        pltpu.VMEM((tb, d), BF16),
                        pltpu.VMEM((tb, d), BF16),
                        pltpu.VMEM((tb, d), F32),
                        pltpu.VMEM((n_chunks, tb, MXU_COLS), F32),
                        pltpu.VMEM((tb, d), BF16)],
        compiler_params=pltpu.CompilerParams(
            vmem_limit_bytes=VMEM_LIMIT_BYTES,
            dimension_semantics=("arbitrary",)),
        name="prompt_step",
    )(x_prompt, *weights)

    xs = jnp.swapaxes(x_sample, 0, 1)
    sconv = jnp.swapaxes(state_conv[0], 0, 1)
    spool = jnp.swapaxes(state_pool[0], 0, 1)
    y_s, conv_s, lru_s, pool_s = pl.pallas_call(
        _sample_kernel,
        grid=(1,),
        in_specs=[_whole(xs.shape), _whole(sconv.shape), _whole((bs, d)), _whole(spool.shape)]
        + weight_specs,
        out_specs=[_whole_out(xs.shape), _whole_out(sconv.shape), _whole_out((bs, d)),
                   _whole_out(spool.shape)],
        out_shape=[jax.ShapeDtypeStruct(xs.shape, F32),
                   jax.ShapeDtypeStruct(sconv.shape, F32),
                   jax.ShapeDtypeStruct((bs, d), F32),
                   jax.ShapeDtypeStruct(spool.shape, F32)],
        compiler_params=pltpu.CompilerParams(vmem_limit_bytes=VMEM_LIMIT_BYTES),
        name="sample_step",
    )(xs, sconv, state_lru[0], spool, *weights)

    return (y_p, jnp.swapaxes(y_s, 0, 1),
            conv_p[None], lru_p.reshape(1, bp, d), pool_p[None],
            jnp.swapaxes(conv_s, 0, 1)[None], lru_s[None], jnp.swapaxes(pool_s, 0, 1)[None])
```

```python
import functools

import jax
import jax.numpy as jnp
from jax import lax
from jax.experimental import pallas as pl
from jax.experimental.pallas import tpu as pltpu

PAST_LEN = 16384
N_LRU_BLOCKS = 8
CONV_W = 4
LRU_C = 8.0
POOL_WINDOWS = (2, 4, 8, 16)
POOL_PAST = max(POOL_WINDOWS) - 1
EPS = 1e-6

LANES = 128
SUBLANES = 8
MXU_COLS = 256
CONV_PAD = SUBLANES
POOL_LEVELS = max(POOL_WINDOWS).bit_length() - 1
POOL_PAD = POOL_LEVELS * SUBLANES
PROMPT_BLOCK_ROWS = 512
VMEM_LIMIT_BYTES = 56 * 1024 * 1024

F32 = jnp.float32
BF16 = jnp.bfloat16


def _rmsnorm(x, g):
    return x * lax.rsqrt(jnp.mean(x * x, axis=-1, keepdims=True) + EPS) * g


def _silu_from_half(hg):
    return hg + hg * jnp.tanh(hg)


def _half_neg_c_softplus_neg(lam):
    z = -lam
    return (-0.5 * LRU_C) * (jnp.maximum(z, 0.0) + jnp.log1p(jnp.exp(-jnp.abs(z))))


def _lru_terms(hx, zr, zi, hbr, hbi, hn):
    log_a = hn + hn * jnp.tanh(zr + hbr)
    a = jnp.exp(log_a)
    y = jnp.tanh(log_a) * (-1.0 - a * a)
    mult = jnp.where(y > 0.0, y * lax.rsqrt(y), 0.0)
    gated = hx + hx * jnp.tanh(zi + hbi)
    return a, mult, gated


def _lane_slab(ref, c):
    return ref[:, c * LANES:(c + 1) * LANES]


def _cols(q):
    return slice(q * MXU_COLS, (q + 1) * MXU_COLS)


def _prompt_stage_order(n_chunks, slabs_per_chunk):
    assert (n_chunks, slabs_per_chunk) == (4, 2)
    return ("aN aX0 bN aX1 aC0 bX0 aC1 aC2 aC3 aR0 aR1 aK0 aX2 aS0 aK1 bX1 aS1 bP0 aR2 aC4 aR3 bM0 aC5 aX3 "
            "aK2 aS2 bX2 aK3 aS3 aR4 bP1 aR5 bM1 aC6 aG0 aC7 aK4 bX3 aS4 aK5 aR6 aR7 aS5 bP2 bM2 aK6 aG1 aS6 "
            "aK7 bG0 aS7 bP3 bM3 aH0 aG2 aH1 bZ0 bG1 aH2 aG3 bZ1 bG2 aH3 bG3 bZ2 bZ3 bO0 bO1 bO2 bO3 aO0 bF "
            "aO1 aO2 aO3").split()


def _prompt_kernel(x_ref, an_ref, awin_ref, cw_ref, cb_ref, wri_ref, br_ref, bi_ref, lam_ref,
                   awout_ref, bn_ref, bwin_ref, wgrp_ref, bgrp_ref, bscale_ref, bwout_ref, fn_ref,
                   y_ref, conv_ref, lru_ref, pool_ref,
                   ext_s, sa_s, sb_s, sh_s, h_s, h1_s, px_s, pt_s,
                   ua_s, zr_s, ga_s, hsg_s, ub_s, pooled_s, z_s, gq_s, zg_s, *, blocks_per_seq):
    n = pl.program_id(0)
    n_blocks = pl.num_programs(0) - 1
    ta = jnp.minimum(n, n_blocks - 1) % blocks_per_seq
    tbk = jnp.maximum(n - 1, 0) % blocks_per_seq
    tb, d = x_ref.shape[1], x_ref.shape[2]
    n_slabs = d // LANES
    n_chunks = d // MXU_COLS
    slabs_per_chunk = MXU_COLS // LANES
    assert d // len(POOL_WINDOWS) == MXU_COLS, "one pooling group per matmul column chunk"
    seg = tb // SUBLANES
    pitch = seg + 4

    @pl.when(n == 0)
    def _():
        h1_s[...] = jnp.zeros_like(h1_s)

    @pl.when(ta == 0)
    def _():
        ext_s[:, 0:CONV_PAD, :] = jnp.zeros((n_slabs, CONV_PAD, LANES), F32)
        h_s[...] = jnp.zeros_like(h_s)

    @pl.when(tbk == 0)
    def _():
        px_s[:, 0:POOL_PAD, :] = jnp.zeros((n_slabs, POOL_PAD, LANES), F32)

    h1_in = h1_s.at[(n + 1) % 2]
    h1_out = h1_s.at[n % 2]
    sub = lax.broadcasted_iota(jnp.int32, (SUBLANES, 1), 0)
    first_token = jnp.logical_and(ta == 0, sub == 0)
    row16 = lax.broadcasted_iota(jnp.int32, (2 * SUBLANES, 1), 0)
    pos16 = tbk * tb + row16
    hn_row = _half_neg_c_softplus_neg(lam_ref[...])
    hcw, hcb = 0.5 * cw_ref[...], 0.5 * cb_ref[...]
    hbr, hbi = 0.5 * br_ref[...], 0.5 * bi_ref[...]
    held = {}
    carries = [None] * n_slabs
    stages = {}

    def a_norm():
        ua_s[...] = _rmsnorm(x_ref[0], an_ref[...]).astype(BF16)

    def a_xb_chunk(q):
        pq = jnp.dot(ua_s[...], awin_ref[:, _cols(q)], preferred_element_type=F32)
        for i in range(slabs_per_chunk):
            ext_s[q * slabs_per_chunk + i, CONV_PAD:CONV_PAD + tb, :] = pq[:, i * LANES:(i + 1) * LANES]

    def a_gate_chunk(q):
        ga_s[q] = jnp.dot(ua_s[...], awin_ref[:, _cols(n_chunks + q)], preferred_element_type=F32)

    def a_conv(c):
        lanes = slice(c * LANES, (c + 1) * LANES)
        hx = ext_s[c, CONV_PAD:CONV_PAD + tb, :] * hcw[CONV_W - 1:CONV_W, lanes] + hcb[:, lanes]
        for k in range(CONV_W - 1):
            hx = hx + ext_s[c, pl.ds(CONV_PAD - (CONV_W - 1) + k, tb), :] * hcw[k:k + 1, lanes]
        held["hx", c] = hx
        held["hxb", c] = hx.astype(BF16)

    def a_gate_proj(c):
        zr_s[c % 2] = jnp.dot(held.pop(("hxb", c)), wri_ref[c], preferred_element_type=F32)

    def a_coeffs(c):
        a, mult, gated = _lru_terms(held.pop(("hx", c)), zr_s[c % 2, :, :LANES], zr_s[c % 2, :, LANES:],
                                    hbr[:, c * LANES:(c + 1) * LANES], hbi[:, c * LANES:(c + 1) * LANES],
                                    hn_row[:, c * LANES:(c + 1) * LANES])
        mult = jnp.concatenate(
            [jnp.where(first_token, 1.0, mult[0:SUBLANES]), mult[SUBLANES:]], axis=0)
        bterm = mult * gated
        for j in range(SUBLANES):
            sa_s[c, pl.ds(j * pitch, seg), :] = a[j * seg:(j + 1) * seg]
            sb_s[c, pl.ds(j * pitch, seg), :] = bterm[j * seg:(j + 1) * seg]

    def a_scan(c):
        hl = sb_s[c, pl.ds(0, SUBLANES, stride=pitch), :]
        pp = sa_s[c, pl.ds(0, SUBLANES, stride=pitch), :]
        for i in range(1, seg):
            av = sa_s[c, pl.ds(i, SUBLANES, stride=pitch), :]
            hl = av * hl + sb_s[c, pl.ds(i, SUBLANES, stride=pitch), :]
            pp = av * pp
        h_in = jnp.broadcast_to(_lane_slab(h_s, c), (SUBLANES, LANES))
        cin = h_in
        for _ in range(SUBLANES - 1):
            cin = jnp.where(sub == 0, h_in, pltpu.roll(hl + pp * cin, 1, 0))
        carries[c] = (hl + pp * cin)[SUBLANES - 1:SUBLANES]
        h = cin
        for i in range(seg):
            h = sa_s[c, pl.ds(i, SUBLANES, stride=pitch), :] * h + sb_s[c, pl.ds(i, SUBLANES, stride=pitch), :]
            sh_s[c, pl.ds(i, SUBLANES, stride=pitch), :] = h

    def a_gated_out(q):
        hs = jnp.concatenate(
            [jnp.concatenate([sh_s[c, pl.ds(j * pitch, seg), :] for j in range(SUBLANES)], axis=0)
             for c in range(q * slabs_per_chunk, (q + 1) * slabs_per_chunk)], axis=1)
        hsg_s[:, _cols(q)] = (hs * _silu_from_half(ga_s[q])).astype(BF16)

    def a_out_proj(q):
        y0 = jnp.dot(hsg_s[...], awout_ref[:, _cols(q)], preferred_element_type=F32)
        h1_out[:, _cols(q)] = x_ref[0, :, _cols(q)] + y0

    def b_norm():
        ub_s[...] = _rmsnorm(h1_in[...], bn_ref[...]).astype(BF16)

    def b_xb_chunk(q):
        pq = jnp.dot(ub_s[...], bwin_ref[:, _cols(q)], preferred_element_type=F32)
        for i in range(slabs_per_chunk):
            px_s[q * slabs_per_chunk + i, POOL_PAD:POOL_PAD + tb, :] = pq[:, i * LANES:(i + 1) * LANES]

    def pool_slab(c, w):
        levels = w.bit_length() - 1
        tmp = c % slabs_per_chunk
        src = px_s.at[c]
        s = None
        for k in range(levels):
            back = (levels - 1 - k) * SUBLANES
            rows = tb + back
            s = (src[pl.ds(POOL_PAD - back, rows), :]
                 + src[pl.ds(POOL_PAD - back - (1 << k), rows), :])
            if k < levels - 1:
                pt_s[k % 2, tmp, POOL_PAD - back:POOL_PAD + tb, :] = s
                src = pt_s.at[k % 2, tmp]
        inv_cnt = 1.0 / jnp.minimum(pos16 + 1, w).astype(F32)
        mean = jnp.concatenate([s[0:2 * SUBLANES] * inv_cnt, s[2 * SUBLANES:] * (1.0 / w)], axis=0)
        return mean - px_s[c, POOL_PAD:POOL_PAD + tb, :]

    def b_pool(q):
        pooled_s[:, _cols(q)] = jnp.concatenate(
            [pool_slab(q * slabs_per_chunk + i, POOL_WINDOWS[q]) for i in range(slabs_per_chunk)],
            axis=1).astype(BF16)

    def b_group_proj(q):
        z_s[:, _cols(q)] = jnp.dot(pooled_s[:, _cols(q)], wgrp_ref[q], preferred_element_type=F32)

    def b_gate_chunk(q):
        gq_s[q] = jnp.dot(ub_s[...], bwin_ref[:, _cols(n_chunks + q)], preferred_element_type=F32)

    def b_gated_proj(q):
        z1 = (z_s[:, _cols(q)] + bgrp_ref[:, _cols(q)]) * bscale_ref[:, _cols(q)]
        zg_s[:, _cols(q)] = (z1 * _silu_from_half(gq_s[q])).astype(BF16)

    def b_out_proj(q):
        y1 = jnp.dot(zg_s[...], bwout_ref[:, _cols(q)], preferred_element_type=F32)
        y_ref[0, :, _cols(q)] = h1_in[:, _cols(q)] + y1

    def b_final_norm():
        y_ref[0] = _rmsnorm(y_ref[0], fn_ref[...])

    stages["aN"] = a_norm
    stages["bN"] = b_norm
    stages["bF"] = b_final_norm
    for q in range(n_chunks):
        for name, fn in (("aX", a_xb_chunk), ("aG", a_gate_chunk), ("aH", a_gated_out),
                         ("aO", a_out_proj), ("bX", b_xb_chunk), ("bP", b_pool),
                         ("bM", b_group_proj), ("bG", b_gate_chunk), ("bZ", b_gated_proj),
                         ("bO", b_out_proj)):
            stages[f"{name}{q}"] = functools.partial(fn, q)
    for c in range(n_slabs):
        for name, fn in (("aC", a_conv), ("aR", a_gate_proj), ("aK", a_coeffs), ("aS", a_scan)):
            stages[f"{name}{c}"] = functools.partial(fn, c)

    order = _prompt_stage_order(n_chunks, slabs_per_chunk)
    assert sorted(order) == sorted(stages), "every stage is issued exactly once"
    for name in order:
        stages[name]()
    h_s[...] = jnp.concatenate(carries, axis=1)

    @pl.when(jnp.logical_and(ta == blocks_per_seq - 1, n < n_blocks))
    def _():
        for c in range(n_slabs):
            conv_ref[0, :, c * LANES:(c + 1) * LANES] = ext_s[c, tb + CONV_PAD - (CONV_W - 1):tb + CONV_PAD, :]
        lru_ref[0] = h_s[...]

    @pl.when(jnp.logical_and(tbk == blocks_per_seq - 1, n > 0))
    def _():
        for c in range(n_slabs):
            pool_ref[0, :, c * LANES:(c + 1) * LANES] = px_s[c, tb + POOL_PAD - POOL_PAST:tb + POOL_PAD, :]

    ext_s[:, 0:CONV_PAD, :] = ext_s[:, tb:tb + CONV_PAD, :]
    px_s[:, 0:POOL_PAD, :] = px_s[:, tb:tb + POOL_PAD, :]


def _sample_kernel(x_ref, sconv_ref, slru_ref, spool_ref,
                   an_ref, awin_ref, cw_ref, cb_ref, wri_ref, br_ref, bi_ref, lam_ref,
                   awout_ref, bn_ref, bwin_ref, wgrp_ref, bgrp_ref, bscale_ref, bwout_ref, fn_ref,
                   y_ref, conv_ref, lru_ref, pool_ref):
    ts, nb, d = x_ref.shape
    lb = d // N_LRU_BLOCKS
    gw = d // len(POOL_WINDOWS)

    def step(ref, k):
        return ref[k]

    def slab(v, i):
        return v[i * nb:(i + 1) * nb]

    x = x_ref[...].reshape(ts * nb, d)

    u = _rmsnorm(x, an_ref[...]).astype(BF16)
    proj = jnp.dot(u, awin_ref[...], preferred_element_type=F32)
    xb = proj[:, :d]
    ext = [step(sconv_ref, k) for k in range(CONV_W - 1)] + [slab(xb, i) for i in range(ts)]
    hcw, hcb = 0.5 * cw_ref[...], 0.5 * cb_ref[...]
    hx_slabs = []
    for i in range(ts):
        acc = hcb
        for k in range(CONV_W):
            acc = acc + ext[i + k] * hcw[k:k + 1, :]
        hx_slabs.append(acc)
    hx = jnp.concatenate(hx_slabs, axis=0)
    hx_bf16 = hx.astype(BF16)
    zr, zi = [], []
    for n in range(N_LRU_BLOCKS):
        z = jnp.dot(hx_bf16[:, n * lb:(n + 1) * lb], wri_ref[n], preferred_element_type=F32)
        zr.append(z[:, :lb])
        zi.append(z[:, lb:])
    a, mult, gated = _lru_terms(hx, jnp.concatenate(zr, axis=1), jnp.concatenate(zi, axis=1),
                                0.5 * br_ref[...], 0.5 * bi_ref[...],
                                _half_neg_c_softplus_neg(lam_ref[...]))
    h = slru_ref[...]
    hs_slabs = []
    for i in range(ts):
        m = slab(mult, i)
        if PAST_LEN + i == 0:
            m = jnp.ones_like(m)
        h = slab(a, i) * h + m * slab(gated, i)
        hs_slabs.append(h)
    lru_ref[...] = h
    hs = jnp.concatenate(hs_slabs, axis=0)
    y0 = jnp.dot((hs * _silu_from_half(proj[:, d:])).astype(BF16), awout_ref[...],
                 preferred_element_type=F32)
    h1 = x + y0
    for k in range(CONV_W - 1):
        conv_ref[k] = ext[ts + k]

    u1 = _rmsnorm(h1, bn_ref[...]).astype(BF16)
    proj1 = jnp.dot(u1, bwin_ref[...], preferred_element_type=F32)
    xb1 = proj1[:, :d]
    ext2 = [step(spool_ref, k) for k in range(POOL_PAST)] + [slab(xb1, i) for i in range(ts)]
    pooled_slabs = []
    for i in range(ts):
        means = []
        for gi, w in enumerate(POOL_WINDOWS):
            lanes = slice(gi * gw, (gi + 1) * gw)
            s = ext2[POOL_PAST + i][:, lanes]
            for j in range(1, w):
                s = s + ext2[POOL_PAST + i - j][:, lanes]
            means.append(s * (1.0 / min(PAST_LEN + i + 1, w)))
        pooled_slabs.append(jnp.concatenate(means, axis=1) - ext2[POOL_PAST + i])
    pooled = jnp.concatenate(pooled_slabs, axis=0).astype(BF16)
    z1 = jnp.concatenate(
        [jnp.dot(pooled[:, g * gw:(g + 1) * gw], wgrp_ref[g], preferred_element_type=F32)
         for g in range(len(POOL_WINDOWS))], axis=1)
    z1 = (z1 + bgrp_ref[...]) * bscale_ref[...]
    y1 = jnp.dot((z1 * _silu_from_half(proj1[:, d:])).astype(BF16), bwout_ref[...],
                 preferred_element_type=F32)
    h2 = h1 + y1
    y_ref[...] = _rmsnorm(h2, fn_ref[...]).reshape(ts, nb, d)
    for k in range(POOL_PAST):
        pool_ref[k] = ext2[ts + k]


def _whole(shape):
    return pl.BlockSpec(shape, lambda *_: (0,) * len(shape), pipeline_mode=pl.Buffered(1))


def _whole_out(shape):
    return pl.BlockSpec(shape, lambda *_: (0,) * len(shape))


def _half_gate_columns(w_in, d):
    return jnp.concatenate([w_in[:, :d], 0.5 * w_in[:, d:]], axis=1).astype(BF16)


def kernel(x_prompt, x_sample, state_conv, state_lru, state_pool, a_norm, a_w_in, a_conv_w, a_conv_b, a_w_r, a_b_r, a_w_i, a_b_i, a_lam, a_w_out, b_norm, b_w_in, b_w_grp, b_b_grp, b_scale, b_w_out, final_norm):
    bp, tp, d = x_prompt.shape
    bs, ts, _ = x_sample.shape
    tb = PROMPT_BLOCK_ROWS
    assert a_norm.shape[0] == 1 and b_norm.shape[0] == 1, "one layer of each mixer type"
    assert d % MXU_COLS == 0 and d // N_LRU_BLOCKS == LANES
    assert tp % tb == 0 and tb % (SUBLANES * SUBLANES) == 0 and tb >= 2 * SUBLANES
    assert CONV_W - 1 <= ts <= POOL_PAST

    row = lambda v: v.reshape(1, d)
    wri = jnp.concatenate([a_w_r[0], a_w_i[0]], axis=-1).astype(BF16)
    weights = (row(a_norm[0]), _half_gate_columns(a_w_in[0], d), a_conv_w[0],
               row(a_conv_b[0]), wri, row(a_b_r[0]), row(a_b_i[0]), row(a_lam[0]),
               a_w_out[0].astype(BF16), row(b_norm[0]), _half_gate_columns(b_w_in[0], d),
               b_w_grp[0].astype(BF16), row(b_b_grp[0]), row(b_scale[0]), b_w_out[0].astype(BF16),
               row(final_norm))
    weight_specs = [_whole(w.shape) for w in weights]

    n_slabs = d // LANES
    n_chunks = d // MXU_COLS
    seg_rows = SUBLANES * (tb // SUBLANES + 4)
    nt = tp // tb
    n_blocks = bp * nt
    blk0 = lambda n: jnp.minimum(n, n_blocks - 1)
    blk1 = lambda n: jnp.maximum(n - 1, 0)
    y_p, conv_p, lru_p, pool_p = pl.pallas_call(
        functools.partial(_prompt_kernel, blocks_per_seq=nt),
        grid=(n_blocks + 1,),
        in_specs=[pl.BlockSpec((1, tb, d), lambda n: (blk0(n) // nt, blk0(n) % nt, 0))] + weight_specs,
        out_specs=[pl.BlockSpec((1, tb, d), lambda n: (blk1(n) // nt, blk1(n) % nt, 0)),
                   pl.BlockSpec((1, CONV_W - 1, d), lambda n: (blk0(n) // nt, 0, 0)),
                   pl.BlockSpec((1, 1, d), lambda n: (blk0(n) // nt, 0, 0)),
                   pl.BlockSpec((1, POOL_PAST, d), lambda n: (blk1(n) // nt, 0, 0))],
        out_shape=[jax.ShapeDtypeStruct((bp, tp, d), F32),
                   jax.ShapeDtypeStruct((bp, CONV_W - 1, d), F32),
                   jax.ShapeDtypeStruct((bp, 1, d), F32),
                   jax.ShapeDtypeStruct((bp, POOL_PAST, d), F32)],
        scratch_shapes=[pltpu.VMEM((n_slabs, CONV_PAD + tb, LANES), F32),
                        pltpu.VMEM((n_slabs, seg_rows, LANES), F32),
                        pltpu.VMEM((n_slabs, seg_rows, LANES), F32),
                        pltpu.VMEM((n_slabs, seg_rows, LANES), F32),
                        pltpu.VMEM((1, d), F32),
                        pltpu.VMEM((2, tb, d), F32),
                        pltpu.VMEM((n_slabs, POOL_PAD + tb, LANES), F32),
                        pltpu.VMEM((2, MXU_COLS // LANES, POOL_PAD + tb, LANES), F32),
                        pltpu.VMEM((tb, d), BF16),
                        pltpu.VMEM((2, tb, 2 * LANES), F32),
                        pltpu.VMEM((n_chunks, tb, MXU_COLS), F32),
                        pltpu.VMEM((tb, d), BF16),
                        pltpu.VMEM((tb, d), BF16),
                        pltpu.VMEM((tb, d), BF16),
                        pltpu.VMEM((tb, d), F32),
                        pltpu.VMEM((n_chunks, tb, MXU_COLS), F32),
                        pltpu.VMEM((tb, d), BF16)],
        compiler_params=pltpu.CompilerParams(
            vmem_limit_bytes=VMEM_LIMIT_BYTES,
            dimension_semantics=("arbitrary",)),
        name="prompt_step",
    )(x_prompt, *weights)

    xs = jnp.swapaxes(x_sample, 0, 1)
    sconv = jnp.swapaxes(state_conv[0], 0, 1)
    spool = jnp.swapaxes(state_pool[0], 0, 1)
    y_s, conv_s, lru_s, pool_s = pl.pallas_call(
        _sample_kernel,
        grid=(1,),
        in_specs=[_whole(xs.shape), _whole(sconv.shape), _whole((bs, d)), _whole(spool.shape)]
        + weight_specs,
        out_specs=[_whole_out(xs.shape), _whole_out(sconv.shape), _whole_out((bs, d)),
                   _whole_out(spool.shape)],
        out_shape=[jax.ShapeDtypeStruct(xs.shape, F32),
                   jax.ShapeDtypeStruct(sconv.shape, F32),
                   jax.ShapeDtypeStruct((bs, d), F32),
                   jax.ShapeDtypeStruct(spool.shape, F32)],
        compiler_params=pltpu.CompilerParams(vmem_limit_bytes=VMEM_LIMIT_BYTES),
        name="sample_step",
    )(xs, sconv, state_lru[0], spool, *weights)

    return (y_p, jnp.swapaxes(y_s, 0, 1),
            conv_p[None], lru_p.reshape(1, bp, d), pool_p[None],
            jnp.swapaxes(conv_s, 0, 1)[None], lru_s[None], jnp.swapaxes(pool_s, 0, 1)[None])
```

```python
import functools

import jax
import jax.numpy as jnp
from jax import lax
from jax.experimental import pallas as pl
from jax.experimental.pallas import tpu as pltpu

PAST_LEN = 16384
N_LRU_BLOCKS = 8
CONV_W = 4
LRU_C = 8.0
POOL_WINDOWS = (2, 4, 8, 16)
POOL_PAST = max(POOL_WINDOWS) - 1
EPS = 1e-6

LANES = 128
SUBLANES = 8
MXU_COLS = 256
CONV_PAD = SUBLANES
POOL_LEVELS = max(POOL_WINDOWS).bit_length() - 1
POOL_PAD = POOL_LEVELS * SUBLANES
PROMPT_BLOCK_ROWS = 512
VMEM_LIMIT_BYTES = 56 * 1024 * 1024

F32 = jnp.float32
BF16 = jnp.bfloat16


def _rmsnorm(x, g):
    return x * lax.rsqrt(jnp.mean(x * x, axis=-1, keepdims=True) + EPS) * g


def _silu_from_half(hg):
    return hg + hg * jnp.tanh(hg)


def _half_neg_c_softplus_neg(lam):
    z = -lam
    return (-0.5 * LRU_C) * (jnp.maximum(z, 0.0) + jnp.log1p(jnp.exp(-jnp.abs(z))))


def _lru_terms(hx, zr, zi, hbr, hbi, hn):
    log_a = hn + hn * jnp.tanh(zr + hbr)
    a = jnp.exp(log_a)
    y = jnp.tanh(log_a) * (-1.0 - a * a)
    mult = jnp.where(y > 0.0, y * lax.rsqrt(y), 0.0)
    gated = hx + hx * jnp.tanh(zi + hbi)
    return a, mult, gated


def _lane_slab(ref, c):
    return ref[:, c * LANES:(c + 1) * LANES]


def _cols(q):
    return slice(q * MXU_COLS, (q + 1) * MXU_COLS)


def _prompt_stage_order(n_chunks, slabs_per_chunk):
    assert (n_chunks, slabs_per_chunk) == (4, 2)
    return ("aN aX0 bN aX1 aC0 bX0 aC1 aC2 aC3 aR0 aR1 aK0 aX2 aS0 aK1 bX1 aS1 bP0 aR2 aC4 aR3 bM0 aC5 aX3 "
            "aK2 aS2 bX2 aK3 aS3 aR4 bP1 aR5 bM1 aC6 aG0 aC7 aK4 bX3 aS4 aK5 aR6 aR7 aS5 bP2 bM2 aK6 aG1 aS6 "
            "aK7 bG0 aS7 bP3 bM3 aH0 aG2 aH1 bZ0 bG1 aH2 aG3 bZ1 bG2 aH3 bG3 bZ2 bZ3 bO0 bO1 bO2 bO3 aO0 bF "
            "aO1 aO2 aO3").split()


def _prompt_kernel(x_ref, an_ref, awin_ref, cw_ref, cb_ref, wri_ref, br_ref, bi_ref, lam_ref,
                   awout_ref, bn_ref, bwin_ref, wgrp_ref, bgrp_ref, bscale_ref, bwout_ref, fn_ref,
                   y_ref, conv_ref, lru_ref, pool_ref,
                   ext_s, sa_s, sb_s, sh_s, h_s, h1_s, px_s, pt_s,
                   ua_s, zr_s, ga_s, hsg_s, ub_s, pooled_s, z_s, gq_s, zg_s, *, blocks_per_seq):
    n = pl.program_id(0)
    n_blocks = pl.num_programs(0) - 1
    ta = jnp.minimum(n, n_blocks - 1) % blocks_per_seq
    tbk = jnp.maximum(n - 1, 0) % blocks_per_seq
    tb, d = x_ref.shape[1], x_ref.shape[2]
    n_slabs = d // LANES
    n_chunks = d // MXU_COLS
    slabs_per_chunk = MXU_COLS // LANES
    assert d // len(POOL_WINDOWS) == MXU_COLS, "one pooling group per matmul column chunk"
    seg = tb // SUBLANES
    pitch = seg + 4

    @pl.when(n == 0)
    def _():
        h1_s[...] = jnp.zeros_like(h1_s)

    @pl.when(ta == 0)
    def _():
        ext_s[:, 0:CONV_PAD, :] = jnp.zeros((n_slabs, CONV_PAD, LANES), F32)
        h_s[...] = jnp.zeros_like(h_s)

    @pl.when(tbk == 0)
    def _():
        px_s[:, 0:POOL_PAD, :] = jnp.zeros((n_slabs, POOL_PAD, LANES), F32)

    h1_in = h1_s.at[(n + 1) % 2]
    h1_out = h1_s.at[n % 2]
    sub = lax.broadcasted_iota(jnp.int32, (SUBLANES, 1), 0)
    first_token = jnp.logical_and(ta == 0, sub == 0)
    row16 = lax.broadcasted_iota(jnp.int32, (2 * SUBLANES, 1), 0)
    pos16 = tbk * tb + row16
    hn_row = _half_neg_c_softplus_neg(lam_ref[...])
    hcw, hcb = 0.5 * cw_ref[...], 0.5 * cb_ref[...]
    hbr, hbi = 0.5 * br_ref[...], 0.5 * bi_ref[...]
    held = {}
    carries = [None] * n_slabs
    stages = {}

    def a_norm():
        ua_s[...] = _rmsnorm(x_ref[0], an_ref[...]).astype(BF16)

    def a_xb_chunk(q):
        pq = jnp.dot(ua_s[...], awin_ref[:, _cols(q)], preferred_element_type=F32)
        for i in range(slabs_per_chunk):
            ext_s[q * slabs_per_chunk + i, CONV_PAD:CONV_PAD + tb, :] = pq[:, i * LANES:(i + 1) * LANES]

    def a_gate_chunk(q):
        ga_s[q] = jnp.dot(ua_s[...], awin_ref[:, _cols(n_chunks + q)], preferred_element_type=F32)

    def a_conv(c):
        lanes = slice(c * LANES, (c + 1) * LANES)
        hx = ext_s[c, CONV_PAD:CONV_PAD + tb, :] * hcw[CONV_W - 1:CONV_W, lanes] + hcb[:, lanes]
        for k in range(CONV_W - 1):
            hx = hx + ext_s[c, pl.ds(CONV_PAD - (CONV_W - 1) + k, tb), :] * hcw[k:k + 1, lanes]
        held["hx", c] = hx
        held["hxb", c] = hx.astype(BF16)

    def a_gate_proj(c):
        zr_s[c % 2] = jnp.dot(held.pop(("hxb", c)), wri_ref[c], preferred_element_type=F32)

    def a_coeffs(c):
        a, mult, gated = _lru_terms(held.pop(("hx", c)), zr_s[c % 2, :, :LANES], zr_s[c % 2, :, LANES:],
                                    hbr[:, c * LANES:(c + 1) * LANES], hbi[:, c * LANES:(c + 1) * LANES],
                                    hn_row[:, c * LANES:(c + 1) * LANES])
        mult = jnp.concatenate(
            [jnp.where(first_token, 1.0, mult[0:SUBLANES]), mult[SUBLANES:]], axis=0)
        bterm = mult * gated
        for j in range(SUBLANES):
            sa_s[c, pl.ds(j * pitch, seg), :] = a[j * seg:(j + 1) * seg]
            sb_s[c, pl.ds(j * pitch, seg), :] = bterm[j * seg:(j + 1) * seg]

    def a_scan(c):
        hl = sb_s[c, pl.ds(0, SUBLANES, stride=pitch), :]
        pp = sa_s[c, pl.ds(0, SUBLANES, stride=pitch), :]
        for i in range(1, seg):
            av = sa_s[c, pl.ds(i, SUBLANES, stride=pitch), :]
            hl = av * hl + sb_s[c, pl.ds(i, SUBLANES, stride=pitch), :]
            pp = av * pp
        h_in = jnp.broadcast_to(_lane_slab(h_s, c), (SUBLANES, LANES))
        cin = h_in
        for _ in range(SUBLANES - 1):
            cin = jnp.where(sub == 0, h_in, pltpu.roll(hl + pp * cin, 1, 0))
        carries[c] = (hl + pp * cin)[SUBLANES - 1:SUBLANES]
        h = cin
        for i in range(seg):
            h = sa_s[c, pl.ds(i, SUBLANES, stride=pitch), :] * h + sb_s[c, pl.ds(i, SUBLANES, stride=pitch), :]
            sh_s[c, pl.ds(i, SUBLANES, stride=pitch), :] = h

    def a_gated_out(q):
        hs = jnp.concatenate(
            [jnp.concatenate([sh_s[c, pl.ds(j * pitch, seg), :] for j in range(SUBLANES)], axis=0)
             for c in range(q * slabs_per_chunk, (q + 1) * slabs_per_chunk)], axis=1)
        hsg_s[:, _cols(q)] = (hs * _silu_from_half(ga_s[q])).astype(BF16)

    def a_out_proj(q):
        y0 = jnp.dot(hsg_s[...], awout_ref[:, _cols(q)], preferred_element_type=F32)
        h1_out[:, _cols(q)] = x_ref[0, :, _cols(q)] + y0

    def b_norm():
        ub_s[...] = _rmsnorm(h1_in[...], bn_ref[...]).astype(BF16)

    def b_xb_chunk(q):
        pq = jnp.dot(ub_s[...], bwin_ref[:, _cols(q)], preferred_element_type=F32)
        for i in range(slabs_per_chunk):
            px_s[q * slabs_per_chunk + i, POOL_PAD:POOL_PAD + tb, :] = pq[:, i * LANES:(i + 1) * LANES]

    def pool_slab(c, w):
        levels = w.bit_length() - 1
        tmp = c % slabs_per_chunk
        src = px_s.at[c]
        s = None
        for k in range(levels):
            back = (levels - 1 - k) * SUBLANES
            rows = tb + back
            s = (src[pl.ds(POOL_PAD - back, rows), :]
                 + src[pl.ds(POOL_PAD - back - (1 << k), rows), :])
            if k < levels - 1:
                pt_s[k % 2, tmp, POOL_PAD - back:POOL_PAD + tb, :] = s
                src = pt_s.at[k % 2, tmp]
        inv_cnt = 1.0 / jnp.minimum(pos16 + 1, w).astype(F32)
        mean = jnp.concatenate([s[0:2 * SUBLANES] * inv_cnt, s[2 * SUBLANES:] * (1.0 / w)], axis=0)
        return mean - px_s[c, POOL_PAD:POOL_PAD + tb, :]

    def b_pool(q):
        pooled_s[:, _cols(q)] = jnp.concatenate(
            [pool_slab(q * slabs_per_chunk + i, POOL_WINDOWS[q]) for i in range(slabs_per_chunk)],
            axis=1).astype(BF16)

    def b_group_proj(q):
        z_s[:, _cols(q)] = jnp.dot(pooled_s[:, _cols(q)], wgrp_ref[q], preferred_element_type=F32)

    def b_gate_chunk(q):
        gq_s[q] = jnp.dot(ub_s[...], bwin_ref[:, _cols(n_chunks + q)], preferred_element_type=F32)

    def b_gated_proj(q):
        z1 = (z_s[:, _cols(q)] + bgrp_ref[:, _cols(q)]) * bscale_ref[:, _cols(q)]
        zg_s[:, _cols(q)] = (z1 * _silu_from_half(gq_s[q])).astype(BF16)

    def b_out_proj(q):
        y1 = jnp.dot(zg_s[...], bwout_ref[:, _cols(q)], preferred_element_type=F32)
        y_ref[0, :, _cols(q)] = h1_in[:, _cols(q)] + y1

    def b_final_norm():
        y_ref[0] = _rmsnorm(y_ref[0], fn_ref[...])

    stages["aN"] = a_norm
    stages["bN"] = b_norm
    stages["bF"] = b_final_norm
    for q in range(n_chunks):
        for name, fn in (("aX", a_xb_chunk), ("aG", a_gate_chunk), ("aH", a_gated_out),
                         ("aO", a_out_proj), ("bX", b_xb_chunk), ("bP", b_pool),
                         ("bM", b_group_proj), ("bG", b_gate_chunk), ("bZ", b_gated_proj),
                         ("bO", b_out_proj)):
            stages[f"{name}{q}"] = functools.partial(fn, q)
    for c in range(n_slabs):
        for name, fn in (("aC", a_conv), ("aR", a_gate_proj), ("aK", a_coeffs), ("aS", a_scan)):
            stages[f"{name}{c}"] = functools.partial(fn, c)

    order = _prompt_stage_order(n_chunks, slabs_per_chunk)
    assert sorted(order) == sorted(stages), "every stage is issued exactly once"
    for name in order:
        stages[name]()
    h_s[...] = jnp.concatenate(carries, axis=1)

    @pl.when(jnp.logical_and(ta == blocks_per_seq - 1, n < n_blocks))
    def _():
        for c in range(n_slabs):
            conv_ref[0, :, c * LANES:(c + 1) * LANES] = ext_s[c, tb + CONV_PAD - (CONV_W - 1):tb + CONV_PAD, :]
        lru_ref[0] = h_s[...]

    @pl.when(jnp.logical_and(tbk == blocks_per_seq - 1, n > 0))
    def _():
        for c in range(n_slabs):
            pool_ref[0, :, c * LANES:(c + 1) * LANES] = px_s[c, tb + POOL_PAD - POOL_PAST:tb + POOL_PAD, :]

    ext_s[:, 0:CONV_PAD, :] = ext_s[:, tb:tb + CONV_PAD, :]
    px_s[:, 0:POOL_PAD, :] = px_s[:, tb:tb + POOL_PAD, :]


def _sample_kernel(x_ref, sconv_ref, slru_ref, spool_ref,
                   an_ref, awin_ref, cw_ref, cb_ref, wri_ref, br_ref, bi_ref, lam_ref,
                   awout_ref, bn_ref, bwin_ref, wgrp_ref, bgrp_ref, bscale_ref, bwout_ref, fn_ref,
                   y_ref, conv_ref, lru_ref, pool_ref):
    nb, ts, d = x_ref.shape
    lb = d // N_LRU_BLOCKS
    gw = d // len(POOL_WINDOWS)

    def step(ref, k):
        return ref[:, k, :]

    def slab(v, i):
        return v[i * nb:(i + 1) * nb]

    x = jnp.concatenate([step(x_ref, i) for i in range(ts)], axis=0)

    u = _rmsnorm(x, an_ref[...]).astype(BF16)
    proj = jnp.dot(u, awin_ref[...], preferred_element_type=F32)
    xb = proj[:, :d]
    ext = [step(sconv_ref, k) for k in range(CONV_W - 1)] + [slab(xb, i) for i in range(ts)]
    hcw, hcb = 0.5 * cw_ref[...], 0.5 * cb_ref[...]
    hx_slabs = []
    for i in range(ts):
        acc = hcb
        for k in range(CONV_W):
            acc = acc + ext[i + k] * hcw[k:k + 1, :]
        hx_slabs.append(acc)
    hx = jnp.concatenate(hx_slabs, axis=0)
    hx_bf16 = hx.astype(BF16)
    zr, zi = [], []
    for n in range(N_LRU_BLOCKS):
        z = jnp.dot(hx_bf16[:, n * lb:(n + 1) * lb], wri_ref[n], preferred_element_type=F32)
        zr.append(z[:, :lb])
        zi.append(z[:, lb:])
    a, mult, gated = _lru_terms(hx, jnp.concatenate(zr, axis=1), jnp.concatenate(zi, axis=1),
                                0.5 * br_ref[...], 0.5 * bi_ref[...],
                                _half_neg_c_softplus_neg(lam_ref[...]))
    h = slru_ref[...]
    hs_slabs = []
    for i in range(ts):
        m = slab(mult, i)
        if PAST_LEN + i == 0:
            m = jnp.ones_like(m)
        h = slab(a, i) * h + m * slab(gated, i)
        hs_slabs.append(h)
    lru_ref[...] = h
    hs = jnp.concatenate(hs_slabs, axis=0)
    y0 = jnp.dot((hs * _silu_from_half(proj[:, d:])).astype(BF16), awout_ref[...],
                 preferred_element_type=F32)
    h1 = x + y0
    for k in range(CONV_W - 1):
        conv_ref[:, k, :] = ext[ts + k]

    u1 = _rmsnorm(h1, bn_ref[...]).astype(BF16)
    proj1 = jnp.dot(u1, bwin_ref[...], preferred_element_type=F32)
    xb1 = proj1[:, :d]
    ext2 = [step(spool_ref, k) for k in range(POOL_PAST)] + [slab(xb1, i) for i in range(ts)]
    pooled_slabs = []
    for i in range(ts):
        means = []
        for gi, w in enumerate(POOL_WINDOWS):
            lanes = slice(gi * gw, (gi + 1) * gw)
            s = ext2[POOL_PAST + i][:, lanes]
            for j in range(1, w):
                s = s + ext2[POOL_PAST + i - j][:, lanes]
            means.append(s * (1.0 / min(PAST_LEN + i + 1, w)))
        pooled_slabs.append(jnp.concatenate(means, axis=1) - ext2[POOL_PAST + i])
    pooled = jnp.concatenate(pooled_slabs, axis=0).astype(BF16)
    z1 = jnp.concatenate(
        [jnp.dot(pooled[:, g * gw:(g + 1) * gw], wgrp_ref[g], preferred_element_type=F32)
         for g in range(len(POOL_WINDOWS))], axis=1)
    z1 = (z1 + bgrp_ref[...]) * bscale_ref[...]
    y1 = jnp.dot((z1 * _silu_from_half(proj1[:, d:])).astype(BF16), bwout_ref[...],
                 preferred_element_type=F32)
    h2 = h1 + y1
    y = _rmsnorm(h2, fn_ref[...])
    for i in range(ts):
        y_ref[:, i, :] = slab(y, i)
    for k in range(POOL_PAST):
        pool_ref[:, k, :] = ext2[ts + k]


def _whole(shape):
    return pl.BlockSpec(shape, lambda *_: (0,) * len(shape), pipeline_mode=pl.Buffered(1))


def _whole_out(shape):
    return pl.BlockSpec(shape, lambda *_: (0,) * len(shape))


def _half_gate_columns(w_in, d):
    return jnp.concatenate([w_in[:, :d], 0.5 * w_in[:, d:]], axis=1).astype(BF16)


def kernel(x_prompt, x_sample, state_conv, state_lru, state_pool, a_norm, a_w_in, a_conv_w, a_conv_b, a_w_r, a_b_r, a_w_i, a_b_i, a_lam, a_w_out, b_norm, b_w_in, b_w_grp, b_b_grp, b_scale, b_w_out, final_norm):
    bp, tp, d = x_prompt.shape
    bs, ts, _ = x_sample.shape
    tb = PROMPT_BLOCK_ROWS
    assert a_norm.shape[0] == 1 and b_norm.shape[0] == 1, "one layer of each mixer type"
    assert d % MXU_COLS == 0 and d // N_LRU_BLOCKS == LANES
    assert tp % tb == 0 and tb % (SUBLANES * SUBLANES) == 0 and tb >= 2 * SUBLANES
    assert CONV_W - 1 <= ts <= POOL_PAST

    row = lambda v: v.reshape(1, d)
    wri = jnp.concatenate([a_w_r[0], a_w_i[0]], axis=-1).astype(BF16)
    weights = (row(a_norm[0]), _half_gate_columns(a_w_in[0], d), a_conv_w[0],
               row(a_conv_b[0]), wri, row(a_b_r[0]), row(a_b_i[0]), row(a_lam[0]),
               a_w_out[0].astype(BF16), row(b_norm[0]), _half_gate_columns(b_w_in[0], d),
               b_w_grp[0].astype(BF16), row(b_b_grp[0]), row(b_scale[0]), b_w_out[0].astype(BF16),
               row(final_norm))
    weight_specs = [_whole(w.shape) for w in weights]

    n_slabs = d // LANES
    n_chunks = d // MXU_COLS
    seg_rows = SUBLANES * (tb // SUBLANES + 4)
    nt = tp // tb
    n_blocks = bp * nt
    blk0 = lambda n: jnp.minimum(n, n_blocks - 1)
    blk1 = lambda n: jnp.maximum(n - 1, 0)
    y_p, conv_p, lru_p, pool_p = pl.pallas_call(
        functools.partial(_prompt_kernel, blocks_per_seq=nt),
        grid=(n_blocks + 1,),
        in_specs=[pl.BlockSpec((1, tb, d), lambda n: (blk0(n) // nt, blk0(n) % nt, 0))] + weight_specs,
        out_specs=[pl.BlockSpec((1, tb, d), lambda n: (blk1(n) // nt, blk1(n) % nt, 0)),
                   pl.BlockSpec((1, CONV_W - 1, d), lambda n: (blk0(n) // nt, 0, 0)),
                   pl.BlockSpec((1, 1, d), lambda n: (blk0(n) // nt, 0, 0)),
                   pl.BlockSpec((1, POOL_PAST, d), lambda n: (blk1(n) // nt, 0, 0))],
        out_shape=[jax.ShapeDtypeStruct((bp, tp, d), F32),
                   jax.ShapeDtypeStruct((bp, CONV_W - 1, d), F32),
                   jax.ShapeDtypeStruct((bp, 1, d), F32),
                   jax.ShapeDtypeStruct((bp, POOL_PAST, d), F32)],
        scratch_shapes=[pltpu.VMEM((n_slabs, CONV_PAD + tb, LANES), F32),
                        pltpu.VMEM((n_slabs, seg_rows, LANES), F32),
                        pltpu.VMEM((n_slabs, seg_rows, LANES), F32),
                        pltpu.VMEM((n_slabs, seg_rows, LANES), F32),
                        pltpu.VMEM((1, d), F32),
                        pltpu.VMEM((2, tb, d), F32),
                        pltpu.VMEM((n_slabs, POOL_PAD + tb, LANES), F32),
                        pltpu.VMEM((2, MXU_COLS // LANES, POOL_PAD + tb, LANES), F32),
                        pltpu.VMEM((tb, d), BF16),
                        pltpu.VMEM((2, tb, 2 * LANES), F32),
                        pltpu.VMEM((n_chunks, tb, MXU_COLS), F32),
                        pltpu.VMEM((tb, d), BF16),
                        pltpu.VMEM((tb, d), BF16),
                        pltpu.VMEM((tb, d), BF16),
                        pltpu.VMEM((tb, d), F32),
                        pltpu.VMEM((n_chunks, tb, MXU_COLS), F32),
                        pltpu.VMEM((tb, d), BF16)],
        compiler_params=pltpu.CompilerParams(
            vmem_limit_bytes=VMEM_LIMIT_BYTES,
            dimension_semantics=("arbitrary",)),
        name="prompt_step",
    )(x_prompt, *weights)

    xs, sconv, spool = x_sample, state_conv[0], state_pool[0]
    y_s, conv_s, lru_s, pool_s = pl.pallas_call(
        _sample_kernel,
        grid=(1,),
        in_specs=[_whole(xs.shape), _whole(sconv.shape), _whole((bs, d)), _whole(spool.shape)]
        + weight_specs,
        out_specs=[_whole_out(xs.shape), _whole_out(sconv.shape), _whole_out((bs, d)),
                   _whole_out(spool.shape)],
        out_shape=[jax.ShapeDtypeStruct(xs.shape, F32),
                   jax.ShapeDtypeStruct(sconv.shape, F32),
                   jax.ShapeDtypeStruct((bs, d), F32),
                   jax.ShapeDtypeStruct(spool.shape, F32)],
        compiler_params=pltpu.CompilerParams(vmem_limit_bytes=VMEM_LIMIT_BYTES),
        name="sample_step",
    )(xs, sconv, state_lru[0], spool, *weights)

    return (y_p, y_s, conv_p[None], lru_p.reshape(1, bp, d), pool_p[None],
            conv_s[None], lru_s[None], pool_s[None])
```

```python
import functools

import jax
import jax.numpy as jnp
from jax import lax
from jax.experimental import pallas as pl
from jax.experimental.pallas import tpu as pltpu

PAST_LEN = 16384
N_LRU_BLOCKS = 8
CONV_W = 4
LRU_C = 8.0
POOL_WINDOWS = (2, 4, 8, 16)
POOL_PAST = max(POOL_WINDOWS) - 1
EPS = 1e-6

LANES = 128
SUBLANES = 8
MXU_COLS = 256
CONV_PAD = SUBLANES
POOL_LEVELS = max(POOL_WINDOWS).bit_length() - 1
POOL_PAD = POOL_LEVELS * SUBLANES
PROMPT_BLOCK_ROWS = 512
SAMPLE_GRID = 2
VMEM_LIMIT_BYTES = 56 * 1024 * 1024

F32 = jnp.float32
BF16 = jnp.bfloat16


def _rmsnorm(x, g):
    return x * lax.rsqrt(jnp.mean(x * x, axis=-1, keepdims=True) + EPS) * g


def _silu_from_half(hg):
    return hg + hg * jnp.tanh(hg)


def _half_neg_c_softplus_neg(lam):
    z = -lam
    return (-0.5 * LRU_C) * (jnp.maximum(z, 0.0) + jnp.log1p(jnp.exp(-jnp.abs(z))))


def _lru_terms(hx, zr, zi, hbr, hbi, hn):
    log_a = hn + hn * jnp.tanh(zr + hbr)
    a = jnp.exp(log_a)
    y = jnp.tanh(log_a) * (-1.0 - a * a)
    mult = jnp.where(y > 0.0, y * lax.rsqrt(y), 0.0)
    gated = hx + hx * jnp.tanh(zi + hbi)
    return a, mult, gated


def _lane_slab(ref, c):
    return ref[:, c * LANES:(c + 1) * LANES]


def _cols(q):
    return slice(q * MXU_COLS, (q + 1) * MXU_COLS)


def _prompt_stage_order(n_chunks, slabs_per_chunk):
    assert (n_chunks, slabs_per_chunk) == (4, 2)
    return ("aN aX0 bN aX1 aC0 bX0 aC1 aC2 aC3 aR0 aR1 aK0 aX2 aS0 aK1 bX1 aS1 bP0 aR2 aC4 aR3 bM0 aC5 aX3 "
            "aK2 aS2 bX2 aK3 aS3 aR4 bP1 aR5 bM1 aC6 aG0 aC7 aK4 bX3 aS4 aK5 aR6 aR7 aS5 bP2 bM2 aK6 aG1 aS6 "
            "aK7 bG0 aS7 bP3 bM3 aH0 aG2 aH1 bZ0 bG1 aH2 aG3 bZ1 bG2 aH3 bG3 bZ2 bZ3 bO0 bO1 bO2 bO3 aO0 bF "
            "aO1 aO2 aO3").split()


def _prompt_kernel(x_ref, an_ref, awin_ref, cw_ref, cb_ref, wri_ref, br_ref, bi_ref, lam_ref,
                   awout_ref, bn_ref, bwin_ref, wgrp_ref, bgrp_ref, bscale_ref, bwout_ref, fn_ref,
                   y_ref, conv_ref, lru_ref, pool_ref,
                   ext_s, sa_s, sb_s, sh_s, h_s, h1_s, px_s, pt_s,
                   ua_s, zr_s, ga_s, hsg_s, ub_s, pooled_s, z_s, gq_s, zg_s, *, blocks_per_seq):
    n = pl.program_id(0)
    n_blocks = pl.num_programs(0) - 1
    ta = jnp.minimum(n, n_blocks - 1) % blocks_per_seq
    tbk = jnp.maximum(n - 1, 0) % blocks_per_seq
    tb, d = x_ref.shape[1], x_ref.shape[2]
    n_slabs = d // LANES
    n_chunks = d // MXU_COLS
    slabs_per_chunk = MXU_COLS // LANES
    assert d // len(POOL_WINDOWS) == MXU_COLS, "one pooling group per matmul column chunk"
    seg = tb // SUBLANES
    pitch = seg + 4

    @pl.when(n == 0)
    def _():
        h1_s[...] = jnp.zeros_like(h1_s)

    @pl.when(ta == 0)
    def _():
        ext_s[:, 0:CONV_PAD, :] = jnp.zeros((n_slabs, CONV_PAD, LANES), F32)
        h_s[...] = jnp.zeros_like(h_s)

    @pl.when(tbk == 0)
    def _():
        px_s[:, 0:POOL_PAD, :] = jnp.zeros((n_slabs, POOL_PAD, LANES), F32)

    h1_in = h1_s.at[(n + 1) % 2]
    h1_out = h1_s.at[n % 2]
    sub = lax.broadcasted_iota(jnp.int32, (SUBLANES, 1), 0)
    first_token = jnp.logical_and(ta == 0, sub == 0)
    row16 = lax.broadcasted_iota(jnp.int32, (2 * SUBLANES, 1), 0)
    pos16 = tbk * tb + row16
    hn_row = _half_neg_c_softplus_neg(lam_ref[...])
    hcw, hcb = 0.5 * cw_ref[...], 0.5 * cb_ref[...]
    hbr, hbi = 0.5 * br_ref[...], 0.5 * bi_ref[...]
    held = {}
    carries = [None] * n_slabs
    stages = {}

    def a_norm():
        ua_s[...] = _rmsnorm(x_ref[0], an_ref[...]).astype(BF16)

    def a_xb_chunk(q):
        pq = jnp.dot(ua_s[...], awin_ref[:, _cols(q)], preferred_element_type=F32)
        for i in range(slabs_per_chunk):
            ext_s[q * slabs_per_chunk + i, CONV_PAD:CONV_PAD + tb, :] = pq[:, i * LANES:(i + 1) * LANES]

    def a_gate_chunk(q):
        ga_s[q] = jnp.dot(ua_s[...], awin_ref[:, _cols(n_chunks + q)], preferred_element_type=F32)

    def a_conv(c):
        lanes = slice(c * LANES, (c + 1) * LANES)
        hx = ext_s[c, CONV_PAD:CONV_PAD + tb, :] * hcw[CONV_W - 1:CONV_W, lanes] + hcb[:, lanes]
        for k in range(CONV_W - 1):
            hx = hx + ext_s[c, pl.ds(CONV_PAD - (CONV_W - 1) + k, tb), :] * hcw[k:k + 1, lanes]
        held["hx", c] = hx
        held["hxb", c] = hx.astype(BF16)

    def a_gate_proj(c):
        zr_s[c % 2] = jnp.dot(held.pop(("hxb", c)), wri_ref[c], preferred_element_type=F32)

    def a_coeffs(c):
        a, mult, gated = _lru_terms(held.pop(("hx", c)), zr_s[c % 2, :, :LANES], zr_s[c % 2, :, LANES:],
                                    hbr[:, c * LANES:(c + 1) * LANES], hbi[:, c * LANES:(c + 1) * LANES],
                                    hn_row[:, c * LANES:(c + 1) * LANES])
        mult = jnp.concatenate(
            [jnp.where(first_token, 1.0, mult[0:SUBLANES]), mult[SUBLANES:]], axis=0)
        bterm = mult * gated
        for j in range(SUBLANES):
            sa_s[c, pl.ds(j * pitch, seg), :] = a[j * seg:(j + 1) * seg]
            sb_s[c, pl.ds(j * pitch, seg), :] = bterm[j * seg:(j + 1) * seg]

    def a_scan(c):
        hl = sb_s[c, pl.ds(0, SUBLANES, stride=pitch), :]
        pp = sa_s[c, pl.ds(0, SUBLANES, stride=pitch), :]
        for i in range(1, seg):
            av = sa_s[c, pl.ds(i, SUBLANES, stride=pitch), :]
            hl = av * hl + sb_s[c, pl.ds(i, SUBLANES, stride=pitch), :]
            pp = av * pp
        h_in = jnp.broadcast_to(_lane_slab(h_s, c), (SUBLANES, LANES))
        cin = h_in
        for _ in range(SUBLANES - 1):
            cin = jnp.where(sub == 0, h_in, pltpu.roll(hl + pp * cin, 1, 0))
        carries[c] = (hl + pp * cin)[SUBLANES - 1:SUBLANES]
        h = cin
        for i in range(seg):
            h = sa_s[c, pl.ds(i, SUBLANES, stride=pitch), :] * h + sb_s[c, pl.ds(i, SUBLANES, stride=pitch), :]
            sh_s[c, pl.ds(i, SUBLANES, stride=pitch), :] = h

    def a_gated_out(q):
        hs = jnp.concatenate(
            [jnp.concatenate([sh_s[c, pl.ds(j * pitch, seg), :] for j in range(SUBLANES)], axis=0)
             for c in range(q * slabs_per_chunk, (q + 1) * slabs_per_chunk)], axis=1)
        hsg_s[:, _cols(q)] = (hs * _silu_from_half(ga_s[q])).astype(BF16)

    def a_out_proj(q):
        y0 = jnp.dot(hsg_s[...], awout_ref[:, _cols(q)], preferred_element_type=F32)
        h1_out[:, _cols(q)] = x_ref[0, :, _cols(q)] + y0

    def b_norm():
        ub_s[...] = _rmsnorm(h1_in[...], bn_ref[...]).astype(BF16)

    def b_xb_chunk(q):
        pq = jnp.dot(ub_s[...], bwin_ref[:, _cols(q)], preferred_element_type=F32)
        for i in range(slabs_per_chunk):
            px_s[q * slabs_per_chunk + i, POOL_PAD:POOL_PAD + tb, :] = pq[:, i * LANES:(i + 1) * LANES]

    def pool_slab(c, w):
        levels = w.bit_length() - 1
        tmp = c % slabs_per_chunk
        src = px_s.at[c]
        s = None
        for k in range(levels):
            back = (levels - 1 - k) * SUBLANES
            rows = tb + back
            s = (src[pl.ds(POOL_PAD - back, rows), :]
                 + src[pl.ds(POOL_PAD - back - (1 << k), rows), :])
            if k < levels - 1:
                pt_s[k % 2, tmp, POOL_PAD - back:POOL_PAD + tb, :] = s
                src = pt_s.at[k % 2, tmp]
        inv_cnt = 1.0 / jnp.minimum(pos16 + 1, w).astype(F32)
        mean = jnp.concatenate([s[0:2 * SUBLANES] * inv_cnt, s[2 * SUBLANES:] * (1.0 / w)], axis=0)
        return mean - px_s[c, POOL_PAD:POOL_PAD + tb, :]

    def b_pool(q):
        pooled_s[:, _cols(q)] = jnp.concatenate(
            [pool_slab(q * slabs_per_chunk + i, POOL_WINDOWS[q]) for i in range(slabs_per_chunk)],
            axis=1).astype(BF16)

    def b_group_proj(q):
        z_s[:, _cols(q)] = jnp.dot(pooled_s[:, _cols(q)], wgrp_ref[q], preferred_element_type=F32)

    def b_gate_chunk(q):
        gq_s[q] = jnp.dot(ub_s[...], bwin_ref[:, _cols(n_chunks + q)], preferred_element_type=F32)

    def b_gated_proj(q):
        z1 = (z_s[:, _cols(q)] + bgrp_ref[:, _cols(q)]) * bscale_ref[:, _cols(q)]
        zg_s[:, _cols(q)] = (z1 * _silu_from_half(gq_s[q])).astype(BF16)

    def b_out_proj(q):
        y1 = jnp.dot(zg_s[...], bwout_ref[:, _cols(q)], preferred_element_type=F32)
        y_ref[0, :, _cols(q)] = h1_in[:, _cols(q)] + y1

    def b_final_norm():
        y_ref[0] = _rmsnorm(y_ref[0], fn_ref[...])

    stages["aN"] = a_norm
    stages["bN"] = b_norm
    stages["bF"] = b_final_norm
    for q in range(n_chunks):
        for name, fn in (("aX", a_xb_chunk), ("aG", a_gate_chunk), ("aH", a_gated_out),
                         ("aO", a_out_proj), ("bX", b_xb_chunk), ("bP", b_pool),
                         ("bM", b_group_proj), ("bG", b_gate_chunk), ("bZ", b_gated_proj),
                         ("bO", b_out_proj)):
            stages[f"{name}{q}"] = functools.partial(fn, q)
    for c in range(n_slabs):
        for name, fn in (("aC", a_conv), ("aR", a_gate_proj), ("aK", a_coeffs), ("aS", a_scan)):
            stages[f"{name}{c}"] = functools.partial(fn, c)

    order = _prompt_stage_order(n_chunks, slabs_per_chunk)
    assert sorted(order) == sorted(stages), "every stage is issued exactly once"
    for name in order:
        stages[name]()
    h_s[...] = jnp.concatenate(carries, axis=1)

    @pl.when(jnp.logical_and(ta == blocks_per_seq - 1, n < n_blocks))
    def _():
        for c in range(n_slabs):
            conv_ref[0, :, c * LANES:(c + 1) * LANES] = ext_s[c, tb + CONV_PAD - (CONV_W - 1):tb + CONV_PAD, :]
        lru_ref[0] = h_s[...]

    @pl.when(jnp.logical_and(tbk == blocks_per_seq - 1, n > 0))
    def _():
        for c in range(n_slabs):
            pool_ref[0, :, c * LANES:(c + 1) * LANES] = px_s[c, tb + POOL_PAD - POOL_PAST:tb + POOL_PAD, :]

    ext_s[:, 0:CONV_PAD, :] = ext_s[:, tb:tb + CONV_PAD, :]
    px_s[:, 0:POOL_PAD, :] = px_s[:, tb:tb + POOL_PAD, :]


def _sample_kernel(x_ref, sconv_ref, slru_ref, spool_ref,
                   an_ref, awin_ref, cw_ref, cb_ref, wri_ref, br_ref, bi_ref, lam_ref,
                   awout_ref, bn_ref, bwin_ref, wgrp_ref, bgrp_ref, bscale_ref, bwout_ref, fn_ref,
                   y_ref, conv_ref, lru_ref, pool_ref):
    ts, nb, d = x_ref.shape
    lb = d // N_LRU_BLOCKS
    gw = d // len(POOL_WINDOWS)

    def step(ref, k):
        return ref[k]

    def slab(v, i):
        return v[i * nb:(i + 1) * nb]

    x = x_ref[...].reshape(ts * nb, d)

    u = _rmsnorm(x, an_ref[...]).astype(BF16)
    proj = jnp.dot(u, awin_ref[...], preferred_element_type=F32)
    xb = proj[:, :d]
    ext = [step(sconv_ref, k) for k in range(CONV_W - 1)] + [slab(xb, i) for i in range(ts)]
    hcw, hcb = 0.5 * cw_ref[...], 0.5 * cb_ref[...]
    hx_slabs = []
    for i in range(ts):
        acc = hcb
        for k in range(CONV_W):
            acc = acc + ext[i + k] * hcw[k:k + 1, :]
        hx_slabs.append(acc)
    hx = jnp.concatenate(hx_slabs, axis=0)
    hx_bf16 = hx.astype(BF16)
    zr, zi = [], []
    for n in range(N_LRU_BLOCKS):
        z = jnp.dot(hx_bf16[:, n * lb:(n + 1) * lb], wri_ref[n], preferred_element_type=F32)
        zr.append(z[:, :lb])
        zi.append(z[:, lb:])
    a, mult, gated = _lru_terms(hx, jnp.concatenate(zr, axis=1), jnp.concatenate(zi, axis=1),
                                0.5 * br_ref[...], 0.5 * bi_ref[...],
                                _half_neg_c_softplus_neg(lam_ref[...]))
    h = slru_ref[...]
    hs_slabs = []
    for i in range(ts):
        m = slab(mult, i)
        if PAST_LEN + i == 0:
            m = jnp.ones_like(m)
        h = slab(a, i) * h + m * slab(gated, i)
        hs_slabs.append(h)
    lru_ref[...] = h
    hs = jnp.concatenate(hs_slabs, axis=0)
    y0 = jnp.dot((hs * _silu_from_half(proj[:, d:])).astype(BF16), awout_ref[...],
                 preferred_element_type=F32)
    h1 = x + y0
    for k in range(CONV_W - 1):
        conv_ref[k] = ext[ts + k]

    u1 = _rmsnorm(h1, bn_ref[...]).astype(BF16)
    proj1 = jnp.dot(u1, bwin_ref[...], preferred_element_type=F32)
    xb1 = proj1[:, :d]
    ext2 = [step(spool_ref, k) for k in range(POOL_PAST)] + [slab(xb1, i) for i in range(ts)]
    pooled_slabs = []
    for i in range(ts):
        means = []
        for gi, w in enumerate(POOL_WINDOWS):
            lanes = slice(gi * gw, (gi + 1) * gw)
            s = ext2[POOL_PAST + i][:, lanes]
            for j in range(1, w):
                s = s + ext2[POOL_PAST + i - j][:, lanes]
            means.append(s * (1.0 / min(PAST_LEN + i + 1, w)))
        pooled_slabs.append(jnp.concatenate(means, axis=1) - ext2[POOL_PAST + i])
    pooled = jnp.concatenate(pooled_slabs, axis=0).astype(BF16)
    z1 = jnp.concatenate(
        [jnp.dot(pooled[:, g * gw:(g + 1) * gw], wgrp_ref[g], preferred_element_type=F32)
         for g in range(len(POOL_WINDOWS))], axis=1)
    z1 = (z1 + bgrp_ref[...]) * bscale_ref[...]
    y1 = jnp.dot((z1 * _silu_from_half(proj1[:, d:])).astype(BF16), bwout_ref[...],
                 preferred_element_type=F32)
    h2 = h1 + y1
    y_ref[...] = _rmsnorm(h2, fn_ref[...]).reshape(ts, nb, d)
    for k in range(POOL_PAST):
        pool_ref[k] = ext2[ts + k]


def _whole(shape):
    return pl.BlockSpec(shape, lambda *_: (0,) * len(shape), pipeline_mode=pl.Buffered(1))


def _half_gate_columns(w_in, d):
    return jnp.concatenate([w_in[:, :d], 0.5 * w_in[:, d:]], axis=1).astype(BF16)


def kernel(x_prompt, x_sample, state_conv, state_lru, state_pool, a_norm, a_w_in, a_conv_w, a_conv_b, a_w_r, a_b_r, a_w_i, a_b_i, a_lam, a_w_out, b_norm, b_w_in, b_w_grp, b_b_grp, b_scale, b_w_out, final_norm):
    bp, tp, d = x_prompt.shape
    bs, ts, _ = x_sample.shape
    tb = PROMPT_BLOCK_ROWS
    assert a_norm.shape[0] == 1 and b_norm.shape[0] == 1, "one layer of each mixer type"
    assert d % MXU_COLS == 0 and d // N_LRU_BLOCKS == LANES
    assert tp % tb == 0 and tb % (SUBLANES * SUBLANES) == 0 and tb >= 2 * SUBLANES
    assert CONV_W - 1 <= ts <= POOL_PAST and bs % (SAMPLE_GRID * SUBLANES) == 0

    row = lambda v: v.reshape(1, d)
    wri = jnp.concatenate([a_w_r[0], a_w_i[0]], axis=-1).astype(BF16)
    weights = (row(a_norm[0]), _half_gate_columns(a_w_in[0], d), a_conv_w[0],
               row(a_conv_b[0]), wri, row(a_b_r[0]), row(a_b_i[0]), row(a_lam[0]),
               a_w_out[0].astype(BF16), row(b_norm[0]), _half_gate_columns(b_w_in[0], d),
               b_w_grp[0].astype(BF16), row(b_b_grp[0]), row(b_scale[0]), b_w_out[0].astype(BF16),
               row(final_norm))
    weight_specs = [_whole(w.shape) for w in weights]

    n_slabs = d // LANES
    n_chunks = d // MXU_COLS
    seg_rows = SUBLANES * (tb // SUBLANES + 4)
    nt = tp // tb
    n_blocks = bp * nt
    blk0 = lambda n: jnp.minimum(n, n_blocks - 1)
    blk1 = lambda n: jnp.maximum(n - 1, 0)
    y_p, conv_p, lru_p, pool_p = pl.pallas_call(
        functools.partial(_prompt_kernel, blocks_per_seq=nt),
        grid=(n_blocks + 1,),
        in_specs=[pl.BlockSpec((1, tb, d), lambda n: (blk0(n) // nt, blk0(n) % nt, 0))] + weight_specs,
        out_specs=[pl.BlockSpec((1, tb, d), lambda n: (blk1(n) // nt, blk1(n) % nt, 0)),
                   pl.BlockSpec((1, CONV_W - 1, d), lambda n: (blk0(n) // nt, 0, 0)),
                   pl.BlockSpec((1, 1, d), lambda n: (blk0(n) // nt, 0, 0)),
                   pl.BlockSpec((1, POOL_PAST, d), lambda n: (blk1(n) // nt, 0, 0))],
        out_shape=[jax.ShapeDtypeStruct((bp, tp, d), F32),
                   jax.ShapeDtypeStruct((bp, CONV_W - 1, d), F32),
                   jax.ShapeDtypeStruct((bp, 1, d), F32),
                   jax.ShapeDtypeStruct((bp, POOL_PAST, d), F32)],
        scratch_shapes=[pltpu.VMEM((n_slabs, CONV_PAD + tb, LANES), F32),
                        pltpu.VMEM((n_slabs, seg_rows, LANES), F32),
                        pltpu.VMEM((n_slabs, seg_rows, LANES), F32),
                        pltpu.VMEM((n_slabs, seg_rows, LANES), F32),
                        pltpu.VMEM((1, d), F32),
                        pltpu.VMEM((2, tb, d), F32),
                        pltpu.VMEM((n_slabs, POOL_PAD + tb, LANES), F32),
                        pltpu.VMEM((2, MXU_COLS // LANES, POOL_PAD + tb, LANES), F32),
                        pltpu.VMEM((tb, d), BF16),
                        pltpu.VMEM((2, tb, 2 * LANES), F32),
                        pltpu.VMEM((n_chunks, tb, MXU_COLS), F32),
                        pltpu.VMEM((tb, d), BF16),
                        pltpu.VMEM((tb, d), BF16),
                        pltpu.VMEM((tb, d), BF16),
                        pltpu.VMEM((tb, d), F32),
                        pltpu.VMEM((n_chunks, tb, MXU_COLS), F32),
                        pltpu.VMEM((tb, d), BF16)],
        compiler_params=pltpu.CompilerParams(
            vmem_limit_bytes=VMEM_LIMIT_BYTES,
            dimension_semantics=("arbitrary",)),
        name="prompt_step",
    )(x_prompt, *weights)

    xs = jnp.swapaxes(x_sample, 0, 1)
    sconv = jnp.swapaxes(state_conv[0], 0, 1)
    spool = jnp.swapaxes(state_pool[0], 0, 1)
    nbk = bs // SAMPLE_GRID
    seq_block = lambda t: pl.BlockSpec((t, nbk, d), lambda i: (0, i, 0))
    y_s, conv_s, lru_s, pool_s = pl.pallas_call(
        _sample_kernel,
        grid=(SAMPLE_GRID,),
        in_specs=[seq_block(ts), seq_block(CONV_W - 1), pl.BlockSpec((nbk, d), lambda i: (i, 0)),
                  seq_block(POOL_PAST)] + weight_specs,
        out_specs=[seq_block(ts), seq_block(CONV_W - 1), pl.BlockSpec((nbk, d), lambda i: (i, 0)),
                   seq_block(POOL_PAST)],
        out_shape=[jax.ShapeDtypeStruct(xs.shape, F32),
                   jax.ShapeDtypeStruct(sconv.shape, F32),
                   jax.ShapeDtypeStruct((bs, d), F32),
                   jax.ShapeDtypeStruct(spool.shape, F32)],
        compiler_params=pltpu.CompilerParams(vmem_limit_bytes=VMEM_LIMIT_BYTES),
        name="sample_step",
    )(xs, sconv, state_lru[0], spool, *weights)

    return (y_p, jnp.swapaxes(y_s, 0, 1),
            conv_p[None], lru_p.reshape(1, bp, d), pool_p[None],
            jnp.swapaxes(conv_s, 0, 1)[None], lru_s[None], jnp.swapaxes(pool_s, 0, 1)[None])
```

```python
import functools

import jax
import jax.numpy as jnp
from jax import lax
from jax.experimental import pallas as pl
from jax.experimental.pallas import tpu as pltpu

PAST_LEN = 16384
N_LRU_BLOCKS = 8
CONV_W = 4
LRU_C = 8.0
POOL_WINDOWS = (2, 4, 8, 16)
POOL_PAST = max(POOL_WINDOWS) - 1
EPS = 1e-6

LANES = 128
SUBLANES = 8
MXU_COLS = 256
CONV_PAD = SUBLANES
POOL_LEVELS = max(POOL_WINDOWS).bit_length() - 1
POOL_PAD = POOL_LEVELS * SUBLANES
PROMPT_BLOCK_ROWS = 512
VMEM_LIMIT_BYTES = 56 * 1024 * 1024

F32 = jnp.float32
BF16 = jnp.bfloat16


def _rmsnorm(x, g):
    return x * lax.rsqrt(jnp.mean(x * x, axis=-1, keepdims=True) + EPS) * g


def _silu_from_half(hg):
    return hg + hg * jnp.tanh(hg)


def _half_neg_c_softplus_neg(lam):
    z = -lam
    return (-0.5 * LRU_C) * (jnp.maximum(z, 0.0) + jnp.log1p(jnp.exp(-jnp.abs(z))))


def _lru_terms(hx, zr, zi, hbr, hbi, hn):
    log_a = hn + hn * jnp.tanh(zr + hbr)
    a = jnp.exp(log_a)
    y = jnp.tanh(log_a) * (-1.0 - a * a)
    mult = jnp.where(y > 0.0, y * lax.rsqrt(y), 0.0)
    gated = hx + hx * jnp.tanh(zi + hbi)
    return a, mult, gated


def _lane_slab(ref, c):
    return ref[:, c * LANES:(c + 1) * LANES]


def _cols(q):
    return slice(q * MXU_COLS, (q + 1) * MXU_COLS)


def _prompt_stage_order(n_chunks, slabs_per_chunk):
    assert (n_chunks, slabs_per_chunk) == (4, 2)
    return ("aN aX0 bN aX1 aC0 bX0 aC1 aC2 aC3 aR0 aR1 aK0 aX2 aS0 aK1 bX1 aS1 bP0 aR2 aC4 aR3 bM0 aC5 aX3 "
            "aK2 aS2 bX2 aK3 aS3 aR4 bP1 aR5 bM1 aC6 aG0 aC7 aK4 bX3 aS4 aK5 aR6 aR7 aS5 bP2 bM2 aK6 aG1 aS6 "
            "aK7 bG0 aS7 bP3 bM3 aH0 aG2 aH1 bZ0 bG1 aH2 aG3 bZ1 bG2 aH3 bG3 bZ2 bZ3 bO0 bO1 bO2 bO3 aO0 bF "
            "aO1 aO2 aO3").split()


def _prompt_kernel(x_ref, an_ref, awin_ref, cw_ref, cb_ref, wri_ref, br_ref, bi_ref, lam_ref,
                   awout_ref, bn_ref, bwin_ref, wgrp_ref, bgrp_ref, bscale_ref, bwout_ref, fn_ref,
                   y_ref, conv_ref, lru_ref, pool_ref,
                   ext_s, sa_s, sb_s, sh_s, h_s, h1_s, px_s, pt_s,
                   ua_s, zr_s, ga_s, hsg_s, ub_s, pooled_s, z_s, gq_s, zg_s, *, blocks_per_seq):
    n = pl.program_id(0)
    n_blocks = pl.num_programs(0) - 1
    ta = jnp.minimum(n, n_blocks - 1) % blocks_per_seq
    tbk = jnp.maximum(n - 1, 0) % blocks_per_seq
    tb, d = x_ref.shape[1], x_ref.shape[2]
    n_slabs = d // LANES
    n_chunks = d // MXU_COLS
    slabs_per_chunk = MXU_COLS // LANES
    assert d // len(POOL_WINDOWS) == MXU_COLS, "one pooling group per matmul column chunk"
    seg = tb // SUBLANES
    pitch = seg + 4

    @pl.when(n == 0)
    def _():
        h1_s[...] = jnp.zeros_like(h1_s)

    @pl.when(ta == 0)
    def _():
        ext_s[:, 0:CONV_PAD, :] = jnp.zeros((n_slabs, CONV_PAD, LANES), F32)
        h_s[...] = jnp.zeros_like(h_s)

    @pl.when(tbk == 0)
    def _():
        px_s[:, 0:POOL_PAD, :] = jnp.zeros((n_slabs, POOL_PAD, LANES), F32)

    h1_in = h1_s.at[(n + 1) % 2]
    h1_out = h1_s.at[n % 2]
    sub = lax.broadcasted_iota(jnp.int32, (SUBLANES, 1), 0)
    first_token = jnp.logical_and(ta == 0, sub == 0)
    row16 = lax.broadcasted_iota(jnp.int32, (2 * SUBLANES, 1), 0)
    pos16 = tbk * tb + row16
    hn_row = _half_neg_c_softplus_neg(lam_ref[...])
    hcw, hcb = 0.5 * cw_ref[...], 0.5 * cb_ref[...]
    hbr, hbi = 0.5 * br_ref[...], 0.5 * bi_ref[...]
    held = {}
    carries = [None] * n_slabs
    stages = {}

    def a_norm():
        ua_s[...] = _rmsnorm(x_ref[0], an_ref[...]).astype(BF16)

    def a_xb_chunk(q):
        pq = jnp.dot(ua_s[...], awin_ref[:, _cols(q)], preferred_element_type=F32)
        for i in range(slabs_per_chunk):
            ext_s[q * slabs_per_chunk + i, CONV_PAD:CONV_PAD + tb, :] = pq[:, i * LANES:(i + 1) * LANES]

    def a_gate_chunk(q):
        ga_s[q] = jnp.dot(ua_s[...], awin_ref[:, _cols(n_chunks + q)], preferred_element_type=F32)

    def a_conv(c):
        lanes = slice(c * LANES, (c + 1) * LANES)
        hx = ext_s[c, CONV_PAD:CONV_PAD + tb, :] * hcw[CONV_W - 1:CONV_W, lanes] + hcb[:, lanes]
        for k in range(CONV_W - 1):
            hx = hx + ext_s[c, pl.ds(CONV_PAD - (CONV_W - 1) + k, tb), :] * hcw[k:k + 1, lanes]
        held["hx", c] = hx
        held["hxb", c] = hx.astype(BF16)

    def a_gate_proj(c):
        zr_s[c % 2] = jnp.dot(held.pop(("hxb", c)), wri_ref[c], preferred_element_type=F32)

    def a_coeffs(c):
        a, mult, gated = _lru_terms(held.pop(("hx", c)), zr_s[c % 2, :, :LANES], zr_s[c % 2, :, LANES:],
                                    hbr[:, c * LANES:(c + 1) * LANES], hbi[:, c * LANES:(c + 1) * LANES],
                                    hn_row[:, c * LANES:(c + 1) * LANES])
        mult = jnp.concatenate(
            [jnp.where(first_token, 1.0, mult[0:SUBLANES]), mult[SUBLANES:]], axis=0)
        bterm = mult * gated
        for j in range(SUBLANES):
            sa_s[c, pl.ds(j * pitch, seg), :] = a[j * seg:(j + 1) * seg]
            sb_s[c, pl.ds(j * pitch, seg), :] = bterm[j * seg:(j + 1) * seg]

    def a_scan(c):
        hl = sb_s[c, pl.ds(0, SUBLANES, stride=pitch), :]
        pp = sa_s[c, pl.ds(0, SUBLANES, stride=pitch), :]
        for i in range(1, seg):
            av = sa_s[c, pl.ds(i, SUBLANES, stride=pitch), :]
            hl = av * hl + sb_s[c, pl.ds(i, SUBLANES, stride=pitch), :]
            pp = av * pp
        h_in = jnp.broadcast_to(_lane_slab(h_s, c), (SUBLANES, LANES))
        cin = h_in
        for _ in range(SUBLANES - 1):
            cin = jnp.where(sub == 0, h_in, pltpu.roll(hl + pp * cin, 1, 0))
        carries[c] = (hl + pp * cin)[SUBLANES - 1:SUBLANES]
        h = cin
        for i in range(seg):
            h = sa_s[c, pl.ds(i, SUBLANES, stride=pitch), :] * h + sb_s[c, pl.ds(i, SUBLANES, stride=pitch), :]
            sh_s[c, pl.ds(i, SUBLANES, stride=pitch), :] = h

    def a_gated_out(q):
        hs = jnp.concatenate(
            [jnp.concatenate([sh_s[c, pl.ds(j * pitch, seg), :] for j in range(SUBLANES)], axis=0)
             for c in range(q * slabs_per_chunk, (q + 1) * slabs_per_chunk)], axis=1)
        hsg_s[:, _cols(q)] = (hs * _silu_from_half(ga_s[q])).astype(BF16)

    def a_out_proj(q):
        y0 = jnp.dot(hsg_s[...], awout_ref[:, _cols(q)], preferred_element_type=F32)
        h1_out[:, _cols(q)] = x_ref[0, :, _cols(q)] + y0

    def b_norm():
        ub_s[...] = _rmsnorm(h1_in[...], bn_ref[...]).astype(BF16)

    def b_xb_chunk(q):
        pq = jnp.dot(ub_s[...], bwin_ref[:, _cols(q)], preferred_element_type=F32)
        for i in range(slabs_per_chunk):
            px_s[q * slabs_per_chunk + i, POOL_PAD:POOL_PAD + tb, :] = pq[:, i * LANES:(i + 1) * LANES]

    def pool_slab(c, w):
        levels = w.bit_length() - 1
        tmp = c % slabs_per_chunk
        src = px_s.at[c]
        s = None
        for k in range(levels):
            back = (levels - 1 - k) * SUBLANES
            rows = tb + back
            s = (src[pl.ds(POOL_PAD - back, rows), :]
                 + src[pl.ds(POOL_PAD - back - (1 << k), rows), :])
            if k < levels - 1:
                pt_s[k % 2, tmp, POOL_PAD - back:POOL_PAD + tb, :] = s
                src = pt_s.at[k % 2, tmp]
        inv_cnt = 1.0 / jnp.minimum(pos16 + 1, w).astype(F32)
        mean = jnp.concatenate([s[0:2 * SUBLANES] * inv_cnt, s[2 * SUBLANES:] * (1.0 / w)], axis=0)
        return mean - px_s[c, POOL_PAD:POOL_PAD + tb, :]

    def b_pool(q):
        pooled_s[:, _cols(q)] = jnp.concatenate(
            [pool_slab(q * slabs_per_chunk + i, POOL_WINDOWS[q]) for i in range(slabs_per_chunk)],
            axis=1).astype(BF16)

    def b_group_proj(q):
        z_s[:, _cols(q)] = jnp.dot(pooled_s[:, _cols(q)], wgrp_ref[q], preferred_element_type=F32)

    def b_gate_chunk(q):
        gq_s[q] = jnp.dot(ub_s[...], bwin_ref[:, _cols(n_chunks + q)], preferred_element_type=F32)

    def b_gated_proj(q):
        z1 = (z_s[:, _cols(q)] + bgrp_ref[:, _cols(q)]) * bscale_ref[:, _cols(q)]
        zg_s[:, _cols(q)] = (z1 * _silu_from_half(gq_s[q])).astype(BF16)

    def b_out_proj(q):
        y1 = jnp.dot(zg_s[...], bwout_ref[:, _cols(q)], preferred_element_type=F32)
        y_ref[0, :, _cols(q)] = h1_in[:, _cols(q)] + y1

    def b_final_norm():
        y_ref[0] = _rmsnorm(y_ref[0], fn_ref[...])

    stages["aN"] = a_norm
    stages["bN"] = b_norm
    stages["bF"] = b_final_norm
    for q in range(n_chunks):
        for name, fn in (("aX", a_xb_chunk), ("aG", a_gate_chunk), ("aH", a_gated_out),
                         ("aO", a_out_proj), ("bX", b_xb_chunk), ("bP", b_pool),
                         ("bM", b_group_proj), ("bG", b_gate_chunk), ("bZ", b_gated_proj),
                         ("bO", b_out_proj)):
            stages[f"{name}{q}"] = functools.partial(fn, q)
    for c in range(n_slabs):
        for name, fn in (("aC", a_conv), ("aR", a_gate_proj), ("aK", a_coeffs), ("aS", a_scan)):
            stages[f"{name}{c}"] = functools.partial(fn, c)

    order = _prompt_stage_order(n_chunks, slabs_per_chunk)
    assert sorted(order) == sorted(stages), "every stage is issued exactly once"
    for name in order:
        stages[name]()
    h_s[...] = jnp.concatenate(carries, axis=1)

    @pl.when(jnp.logical_and(ta == blocks_per_seq - 1, n < n_blocks))
    def _():
        for c in range(n_slabs):
            conv_ref[0, :, c * LANES:(c + 1) * LANES] = ext_s[c, tb + CONV_PAD - (CONV_W - 1):tb + CONV_PAD, :]
        lru_ref[0] = h_s[...]

    @pl.when(jnp.logical_and(tbk == blocks_per_seq - 1, n > 0))
    def _():
        for c in range(n_slabs):
            pool_ref[0, :, c * LANES:(c + 1) * LANES] = px_s[c, tb + POOL_PAD - POOL_PAST:tb + POOL_PAD, :]

    ext_s[:, 0:CONV_PAD, :] = ext_s[:, tb:tb + CONV_PAD, :]
    px_s[:, 0:POOL_PAD, :] = px_s[:, tb:tb + POOL_PAD, :]


def _sample_kernel(x_hbm, sconv_hbm, slru_hbm, spool_hbm,
                   an_ref, awin_hbm, cw_ref, cb_ref, wri_hbm, br_ref, bi_ref, lam_ref,
                   awout_hbm, bn_ref, bwin_hbm, wgrp_hbm, bgrp_ref, bscale_ref, bwout_hbm, fn_ref,
                   y_hbm, conv_hbm, lru_hbm, pool_hbm,
                   x_v, sconv_v, slru_v, spool_v, awin_v, wri_v, awout_v, bwin_v, wgrp_v, bwout_v,
                   y_v, conv_v, lru_v, xb1_v, load_sem, store_sem):
    ts, nb, d = x_v.shape
    lb = d // N_LRU_BLOCKS
    gw = d // len(POOL_WINDOWS)
    kept = POOL_PAST - ts

    bulk = (("x", x_hbm, x_v), ("awin", awin_hbm, awin_v), ("sconv", sconv_hbm, sconv_v),
            ("wri", wri_hbm, wri_v), ("slru", slru_hbm, slru_v), ("awout", awout_hbm, awout_v),
            ("bwin", bwin_hbm, bwin_v), ("spool", spool_hbm, spool_v), ("wgrp", wgrp_hbm, wgrp_v),
            ("bwout", bwout_hbm, bwout_v))
    loads = {name: pltpu.make_async_copy(src, dst, load_sem.at[i])
             for i, (name, src, dst) in enumerate(bulk)}
    for load in loads.values():
        load.start()
    stores = {
        "conv": pltpu.make_async_copy(conv_v, conv_hbm, store_sem.at[0]),
        "lru": pltpu.make_async_copy(lru_v, lru_hbm, store_sem.at[1]),
        "pool_kept": pltpu.make_async_copy(spool_v.at[pl.ds(ts, kept)], pool_hbm.at[pl.ds(0, kept)],
                                           store_sem.at[2]),
        "pool_new": pltpu.make_async_copy(xb1_v, pool_hbm.at[pl.ds(kept, ts)], store_sem.at[3]),
        "y": pltpu.make_async_copy(y_v, y_hbm, store_sem.at[4]),
    }

    def slab(v, i):
        return v[i * nb:(i + 1) * nb]

    loads["x"].wait()
    x = x_v[...].reshape(ts * nb, d)
    u = _rmsnorm(x, an_ref[...]).astype(BF16)
    loads["awin"].wait()
    proj = jnp.dot(u, awin_v[...], preferred_element_type=F32)
    xb = proj[:, :d]
    loads["sconv"].wait()
    ext = [sconv_v[k] for k in range(CONV_W - 1)] + [slab(xb, i) for i in range(ts)]
    for k in range(CONV_W - 1):
        conv_v[k] = ext[ts + k]
    stores["conv"].start()
    hcw, hcb = 0.5 * cw_ref[...], 0.5 * cb_ref[...]
    hx_slabs = []
    for i in range(ts):
        acc = hcb
        for k in range(CONV_W):
            acc = acc + ext[i + k] * hcw[k:k + 1, :]
        hx_slabs.append(acc)
    hx = jnp.concatenate(hx_slabs, axis=0)
    hx_bf16 = hx.astype(BF16)
    loads["wri"].wait()
    zr, zi = [], []
    for n in range(N_LRU_BLOCKS):
        z = jnp.dot(hx_bf16[:, n * lb:(n + 1) * lb], wri_v[n], preferred_element_type=F32)
        zr.append(z[:, :lb])
        zi.append(z[:, lb:])
    a, mult, gated = _lru_terms(hx, jnp.concatenate(zr, axis=1), jnp.concatenate(zi, axis=1),
                                0.5 * br_ref[...], 0.5 * bi_ref[...],
                                _half_neg_c_softplus_neg(lam_ref[...]))
    loads["slru"].wait()
    h = slru_v[...]
    hs_slabs = []
    for i in range(ts):
        m = slab(mult, i)
        if PAST_LEN + i == 0:
            m = jnp.ones_like(m)
        h = slab(a, i) * h + m * slab(gated, i)
        hs_slabs.append(h)
    lru_v[...] = h
    stores["lru"].start()
    hs = jnp.concatenate(hs_slabs, axis=0)
    loads["awout"].wait()
    y0 = jnp.dot((hs * _silu_from_half(proj[:, d:])).astype(BF16), awout_v[...],
                 preferred_element_type=F32)
    h1 = x + y0

    u1 = _rmsnorm(h1, bn_ref[...]).astype(BF16)
    loads["bwin"].wait()
    proj1 = jnp.dot(u1, bwin_v[...], preferred_element_type=F32)
    xb1 = proj1[:, :d]
    xb1_v[...] = xb1.reshape(ts, nb, d)
    stores["pool_new"].start()
    loads["spool"].wait()
    stores["pool_kept"].start()
    ext2 = [spool_v[k] for k in range(POOL_PAST)] + [slab(xb1, i) for i in range(ts)]
    pooled_slabs = []
    for i in range(ts):
        means = []
        for gi, w in enumerate(POOL_WINDOWS):
            lanes = slice(gi * gw, (gi + 1) * gw)
            s = ext2[POOL_PAST + i][:, lanes]
            for j in range(1, w):
                s = s + ext2[POOL_PAST + i - j][:, lanes]
            means.append(s * (1.0 / min(PAST_LEN + i + 1, w)))
        pooled_slabs.append(jnp.concatenate(means, axis=1) - ext2[POOL_PAST + i])
    pooled = jnp.concatenate(pooled_slabs, axis=0).astype(BF16)
    loads["wgrp"].wait()
    z1 = jnp.concatenate(
        [jnp.dot(pooled[:, g * gw:(g + 1) * gw], wgrp_v[g], preferred_element_type=F32)
         for g in range(len(POOL_WINDOWS))], axis=1)
    z1 = (z1 + bgrp_ref[...]) * bscale_ref[...]
    loads["bwout"].wait()
    y1 = jnp.dot((z1 * _silu_from_half(proj1[:, d:])).astype(BF16), bwout_v[...],
                 preferred_element_type=F32)
    h2 = h1 + y1
    y_v[...] = _rmsnorm(h2, fn_ref[...]).reshape(ts, nb, d)
    stores["y"].start()
    for store in stores.values():
        store.wait()


def _whole(shape):
    return pl.BlockSpec(shape, lambda *_: (0,) * len(shape), pipeline_mode=pl.Buffered(1))


def _half_gate_columns(w_in, d):
    return jnp.concatenate([w_in[:, :d], 0.5 * w_in[:, d:]], axis=1).astype(BF16)


def kernel(x_prompt, x_sample, state_conv, state_lru, state_pool, a_norm, a_w_in, a_conv_w, a_conv_b, a_w_r, a_b_r, a_w_i, a_b_i, a_lam, a_w_out, b_norm, b_w_in, b_w_grp, b_b_grp, b_scale, b_w_out, final_norm):
    bp, tp, d = x_prompt.shape
    bs, ts, _ = x_sample.shape
    tb = PROMPT_BLOCK_ROWS
    assert a_norm.shape[0] == 1 and b_norm.shape[0] == 1, "one layer of each mixer type"
    assert d % MXU_COLS == 0 and d // N_LRU_BLOCKS == LANES
    assert tp % tb == 0 and tb % (SUBLANES * SUBLANES) == 0 and tb >= 2 * SUBLANES
    assert CONV_W - 1 <= ts <= POOL_PAST

    row = lambda v: v.reshape(1, d)
    wri = jnp.concatenate([a_w_r[0], a_w_i[0]], axis=-1).astype(BF16)
    weights = (row(a_norm[0]), _half_gate_columns(a_w_in[0], d), a_conv_w[0],
               row(a_conv_b[0]), wri, row(a_b_r[0]), row(a_b_i[0]), row(a_lam[0]),
               a_w_out[0].astype(BF16), row(b_norm[0]), _half_gate_columns(b_w_in[0], d),
               b_w_grp[0].astype(BF16), row(b_b_grp[0]), row(b_scale[0]), b_w_out[0].astype(BF16),
               row(final_norm))
    weight_specs = [_whole(w.shape) for w in weights]

    n_slabs = d // LANES
    n_chunks = d // MXU_COLS
    seg_rows = SUBLANES * (tb // SUBLANES + 4)
    nt = tp // tb
    n_blocks = bp * nt
    blk0 = lambda n: jnp.minimum(n, n_blocks - 1)
    blk1 = lambda n: jnp.maximum(n - 1, 0)
    y_p, conv_p, lru_p, pool_p = pl.pallas_call(
        functools.partial(_prompt_kernel, blocks_per_seq=nt),
        grid=(n_blocks + 1,),
        in_specs=[pl.BlockSpec((1, tb, d), lambda n: (blk0(n) // nt, blk0(n) % nt, 0))] + weight_specs,
        out_specs=[pl.BlockSpec((1, tb, d), lambda n: (blk1(n) // nt, blk1(n) % nt, 0)),
                   pl.BlockSpec((1, CONV_W - 1, d), lambda n: (blk0(n) // nt, 0, 0)),
                   pl.BlockSpec((1, 1, d), lambda n: (blk0(n) // nt, 0, 0)),
                   pl.BlockSpec((1, POOL_PAST, d), lambda n: (blk1(n) // nt, 0, 0))],
        out_shape=[jax.ShapeDtypeStruct((bp, tp, d), F32),
                   jax.ShapeDtypeStruct((bp, CONV_W - 1, d), F32),
                   jax.ShapeDtypeStruct((bp, 1, d), F32),
                   jax.ShapeDtypeStruct((bp, POOL_PAST, d), F32)],
        scratch_shapes=[pltpu.VMEM((n_slabs, CONV_PAD + tb, LANES), F32),
                        pltpu.VMEM((n_slabs, seg_rows, LANES), F32),
                        pltpu.VMEM((n_slabs, seg_rows, LANES), F32),
                        pltpu.VMEM((n_slabs, seg_rows, LANES), F32),
                        pltpu.VMEM((1, d), F32),
                        pltpu.VMEM((2, tb, d), F32),
                        pltpu.VMEM((n_slabs, POOL_PAD + tb, LANES), F32),
                        pltpu.VMEM((2, MXU_COLS // LANES, POOL_PAD + tb, LANES), F32),
                        pltpu.VMEM((tb, d), BF16),
                        pltpu.VMEM((2, tb, 2 * LANES), F32),
                        pltpu.VMEM((n_chunks, tb, MXU_COLS), F32),
                        pltpu.VMEM((tb, d), BF16),
                        pltpu.VMEM((tb, d), BF16),
                        pltpu.VMEM((tb, d), BF16),
                        pltpu.VMEM((tb, d), F32),
                        pltpu.VMEM((n_chunks, tb, MXU_COLS), F32),
                        pltpu.VMEM((tb, d), BF16)],
        compiler_params=pltpu.CompilerParams(
            vmem_limit_bytes=VMEM_LIMIT_BYTES,
            dimension_semantics=("arbitrary",)),
        name="prompt_step",
    )(x_prompt, *weights)

    xs = jnp.swapaxes(x_sample, 0, 1)
    sconv = jnp.swapaxes(state_conv[0], 0, 1)
    spool = jnp.swapaxes(state_pool[0], 0, 1)
    hbm = pl.BlockSpec(memory_space=pl.ANY)
    in_hbm = {1, 4, 8, 10, 11, 14}
    bulk_weights = [weights[i] for i in sorted(in_hbm)]
    act = lambda shape: pltpu.VMEM(shape, F32)
    y_s, conv_s, lru_s, pool_s = pl.pallas_call(
        _sample_kernel,
        in_specs=[hbm] * 4 + [hbm if i in in_hbm else _whole(w.shape) for i, w in enumerate(weights)],
        out_specs=[hbm] * 4,
        out_shape=[jax.ShapeDtypeStruct(xs.shape, F32),
                   jax.ShapeDtypeStruct(sconv.shape, F32),
                   jax.ShapeDtypeStruct((bs, d), F32),
                   jax.ShapeDtypeStruct(spool.shape, F32)],
        scratch_shapes=[act(xs.shape), act(sconv.shape), act((bs, d)), act(spool.shape)]
        + [pltpu.VMEM(w.shape, w.dtype) for w in bulk_weights]
        + [act(xs.shape), act(sconv.shape), act((bs, d)), act(xs.shape),
           pltpu.SemaphoreType.DMA((4 + len(bulk_weights),)), pltpu.SemaphoreType.DMA((5,))],
        compiler_params=pltpu.CompilerParams(vmem_limit_bytes=VMEM_LIMIT_BYTES),
        name="sample_step",
    )(xs, sconv, state_lru[0], spool, *weights)

    return (y_p, jnp.swapaxes(y_s, 0, 1),
            conv_p[None], lru_p.reshape(1, bp, d), pool_p[None],
            jnp.swapaxes(conv_s, 0, 1)[None], lru_s[None], jnp.swapaxes(pool_s, 0, 1)[None])
```

```python
import functools

import jax
import jax.numpy as jnp
from jax import lax
from jax.experimental import pallas as pl
from jax.experimental.pallas import tpu as pltpu

PAST_LEN = 16384
N_LRU_BLOCKS = 8
CONV_W = 4
LRU_C = 8.0
POOL_WINDOWS = (2, 4, 8, 16)
POOL_PAST = max(POOL_WINDOWS) - 1
EPS = 1e-6

LANES = 128
SUBLANES = 8
MXU_COLS = 256
CONV_PAD = SUBLANES
POOL_LEVELS = max(POOL_WINDOWS).bit_length() - 1
POOL_PAD = POOL_LEVELS * SUBLANES
PROMPT_BLOCK_ROWS = 512
VMEM_LIMIT_BYTES = 56 * 1024 * 1024

F32 = jnp.float32
BF16 = jnp.bfloat16


def _rmsnorm(x, g):
    return x * lax.rsqrt(jnp.mean(x * x, axis=-1, keepdims=True) + EPS) * g


def _silu_from_half(hg):
    return hg + hg * jnp.tanh(hg)


def _half_neg_c_softplus_neg(lam):
    z = -lam
    return (-0.5 * LRU_C) * (jnp.maximum(z, 0.0) + jnp.log1p(jnp.exp(-jnp.abs(z))))


def _lru_terms(hx, zr, zi, hbr, hbi, hn):
    log_a = hn + hn * jnp.tanh(zr + hbr)
    a = jnp.exp(log_a)
    y = jnp.tanh(log_a) * (-1.0 - a * a)
    mult = jnp.where(y > 0.0, y * lax.rsqrt(y), 0.0)
    gated = hx + hx * jnp.tanh(zi + hbi)
    return a, mult, gated


def _lane_slab(ref, c):
    return ref[:, c * LANES:(c + 1) * LANES]


def _cols(q):
    return slice(q * MXU_COLS, (q + 1) * MXU_COLS)


def _prompt_stage_order(n_chunks, slabs_per_chunk):
    assert (n_chunks, slabs_per_chunk) == (4, 2)
    return ("aN aX0 bN aX1 aC0 bX0 aC1 aC2 aC3 aR0 aR1 aK0 aX2 aS0 aK1 bX1 aS1 bP0 aR2 aC4 aR3 bM0 aC5 aX3 "
            "aK2 aS2 bX2 aK3 aS3 aR4 bP1 aR5 bM1 aC6 aG0 aC7 aK4 bX3 aS4 aK5 aR6 aR7 aS5 bP2 bM2 aK6 aG1 aS6 "
            "aK7 bG0 aS7 bP3 bM3 aH0 aG2 aH1 bZ0 bG1 aH2 aG3 bZ1 bG2 aH3 bG3 bZ2 bZ3 bO0 bO1 bO2 bO3 aO0 bF "
            "aO1 aO2 aO3").split()


def _prompt_kernel(x_ref, an_ref, awin_ref, cw_ref, cb_ref, wri_ref, br_ref, bi_ref, lam_ref,
                   awout_ref, bn_ref, bwin_ref, wgrp_ref, bgrp_ref, bscale_ref, bwout_ref, fn_ref,
                   y_ref, conv_ref, lru_ref, pool_ref,
                   ext_s, sa_s, sb_s, sh_s, h_s, h1_s, px_s, pt_s,
                   ua_s, zr_s, ga_s, hsg_s, ub_s, pooled_s, z_s, gq_s, zg_s, *, blocks_per_seq):
    n = pl.program_id(0)
    n_blocks = pl.num_programs(0) - 1
    ta = jnp.minimum(n, n_blocks - 1) % blocks_per_seq
    tbk = jnp.maximum(n - 1, 0) % blocks_per_seq
    tb, d = x_ref.shape[1], x_ref.shape[2]
    n_slabs = d // LANES
    n_chunks = d // MXU_COLS
    slabs_per_chunk = MXU_COLS // LANES
    assert d // len(POOL_WINDOWS) == MXU_COLS, "one pooling group per matmul column chunk"
    seg = tb // SUBLANES
    pitch = seg + 4

    @pl.when(n == 0)
    def _():
        h1_s[...] = jnp.zeros_like(h1_s)

    @pl.when(ta == 0)
    def _():
        ext_s[:, 0:CONV_PAD, :] = jnp.zeros((n_slabs, CONV_PAD, LANES), F32)
        h_s[...] = jnp.zeros_like(h_s)

    @pl.when(tbk == 0)
    def _():
        px_s[:, 0:POOL_PAD, :] = jnp.zeros((n_slabs, POOL_PAD, LANES), F32)

    h1_in = h1_s.at[(n + 1) % 2]
    h1_out = h1_s.at[n % 2]
    sub = lax.broadcasted_iota(jnp.int32, (SUBLANES, 1), 0)
    first_token = jnp.logical_and(ta == 0, sub == 0)
    row16 = lax.broadcasted_iota(jnp.int32, (2 * SUBLANES, 1), 0)
    pos16 = tbk * tb + row16
    hn_row = _half_neg_c_softplus_neg(lam_ref[...])
    hcw, hcb = 0.5 * cw_ref[...], 0.5 * cb_ref[...]
    hbr, hbi = 0.5 * br_ref[...], 0.5 * bi_ref[...]
    held = {}
    carries = [None] * n_slabs
    stages = {}

    def a_norm():
        ua_s[...] = _rmsnorm(x_ref[0], an_ref[...]).astype(BF16)

    def a_xb_chunk(q):
        pq = jnp.dot(ua_s[...], awin_ref[:, _cols(q)], preferred_element_type=F32)
        for i in range(slabs_per_chunk):
            ext_s[q * slabs_per_chunk + i, CONV_PAD:CONV_PAD + tb, :] = pq[:, i * LANES:(i + 1) * LANES]

    def a_gate_chunk(q):
        ga_s[q] = jnp.dot(ua_s[...], awin_ref[:, _cols(n_chunks + q)], preferred_element_type=F32)

    def a_conv(c):
        lanes = slice(c * LANES, (c + 1) * LANES)
        hx = ext_s[c, CONV_PAD:CONV_PAD + tb, :] * hcw[CONV_W - 1:CONV_W, lanes] + hcb[:, lanes]
        for k in range(CONV_W - 1):
            hx = hx + ext_s[c, pl.ds(CONV_PAD - (CONV_W - 1) + k, tb), :] * hcw[k:k + 1, lanes]
        held["hx", c] = hx
        held["hxb", c] = hx.astype(BF16)

    def a_gate_proj(c):
        zr_s[c % 2] = jnp.dot(held.pop(("hxb", c)), wri_ref[c], preferred_element_type=F32)

    def a_coeffs(c):
        a, mult, gated = _lru_terms(held.pop(("hx", c)), zr_s[c % 2, :, :LANES], zr_s[c % 2, :, LANES:],
                                    hbr[:, c * LANES:(c + 1) * LANES], hbi[:, c * LANES:(c + 1) * LANES],
                                    hn_row[:, c * LANES:(c + 1) * LANES])
        mult = jnp.concatenate(
            [jnp.where(first_token, 1.0, mult[0:SUBLANES]), mult[SUBLANES:]], axis=0)
        bterm = mult * gated
        for j in range(SUBLANES):
            sa_s[c, pl.ds(j * pitch, seg), :] = a[j * seg:(j + 1) * seg]
            sb_s[c, pl.ds(j * pitch, seg), :] = bterm[j * seg:(j + 1) * seg]

    def a_scan(c):
        hl = sb_s[c, pl.ds(0, SUBLANES, stride=pitch), :]
        pp = sa_s[c, pl.ds(0, SUBLANES, stride=pitch), :]
        for i in range(1, seg):
            av = sa_s[c, pl.ds(i, SUBLANES, stride=pitch), :]
            hl = av * hl + sb_s[c, pl.ds(i, SUBLANES, stride=pitch), :]
            pp = av * pp
        h_in = jnp.broadcast_to(_lane_slab(h_s, c), (SUBLANES, LANES))
        cin = h_in
        for _ in range(SUBLANES - 1):
            cin = jnp.where(sub == 0, h_in, pltpu.roll(hl + pp * cin, 1, 0))
        carries[c] = (hl + pp * cin)[SUBLANES - 1:SUBLANES]
        h = cin
        for i in range(seg):
            h = sa_s[c, pl.ds(i, SUBLANES, stride=pitch), :] * h + sb_s[c, pl.ds(i, SUBLANES, stride=pitch), :]
            sh_s[c, pl.ds(i, SUBLANES, stride=pitch), :] = h

    def a_gated_out(q):
        hs = jnp.concatenate(
            [jnp.concatenate([sh_s[c, pl.ds(j * pitch, seg), :] for j in range(SUBLANES)], axis=0)
             for c in range(q * slabs_per_chunk, (q + 1) * slabs_per_chunk)], axis=1)
        hsg_s[:, _cols(q)] = (hs * _silu_from_half(ga_s[q])).astype(BF16)

    def a_out_proj(q):
        y0 = jnp.dot(hsg_s[...], awout_ref[:, _cols(q)], preferred_element_type=F32)
        h1_out[:, _cols(q)] = x_ref[0, :, _cols(q)] + y0

    def b_norm():
        ub_s[...] = _rmsnorm(h1_in[...], bn_ref[...]).astype(BF16)

    def b_xb_chunk(q):
        pq = jnp.dot(ub_s[...], bwin_ref[:, _cols(q)], preferred_element_type=F32)
        for i in range(slabs_per_chunk):
            px_s[q * slabs_per_chunk + i, POOL_PAD:POOL_PAD + tb, :] = pq[:, i * LANES:(i + 1) * LANES]

    def pool_slab(c, w):
        levels = w.bit_length() - 1
        tmp = c % slabs_per_chunk
        src = px_s.at[c]
        s = None
        for k in range(levels):
            back = (levels - 1 - k) * SUBLANES
            rows = tb + back
            s = (src[pl.ds(POOL_PAD - back, rows), :]
                 + src[pl.ds(POOL_PAD - back - (1 << k), rows), :])
            if k < levels - 1:
                pt_s[k % 2, tmp, POOL_PAD - back:POOL_PAD + tb, :] = s
                src = pt_s.at[k % 2, tmp]
        inv_cnt = 1.0 / jnp.minimum(pos16 + 1, w).astype(F32)
        mean = jnp.concatenate([s[0:2 * SUBLANES] * inv_cnt, s[2 * SUBLANES:] * (1.0 / w)], axis=0)
        return mean - px_s[c, POOL_PAD:POOL_PAD + tb, :]

    def b_pool(q):
        pooled_s[:, _cols(q)] = jnp.concatenate(
            [pool_slab(q * slabs_per_chunk + i, POOL_WINDOWS[q]) for i in range(slabs_per_chunk)],
            axis=1).astype(BF16)

    def b_group_proj(q):
        z_s[:, _cols(q)] = jnp.dot(pooled_s[:, _cols(q)], wgrp_ref[q], preferred_element_type=F32)

    def b_gate_chunk(q):
        gq_s[q] = jnp.dot(ub_s[...], bwin_ref[:, _cols(n_chunks + q)], preferred_element_type=F32)

    def b_gated_proj(q):
        z1 = (z_s[:, _cols(q)] + bgrp_ref[:, _cols(q)]) * bscale_ref[:, _cols(q)]
        zg_s[:, _cols(q)] = (z1 * _silu_from_half(gq_s[q])).astype(BF16)

    def b_out_proj(q):
        y1 = jnp.dot(zg_s[...], bwout_ref[:, _cols(q)], preferred_element_type=F32)
        y_ref[0, :, _cols(q)] = h1_in[:, _cols(q)] + y1

    def b_final_norm():
        y_ref[0] = _rmsnorm(y_ref[0], fn_ref[...])

    stages["aN"] = a_norm
    stages["bN"] = b_norm
    stages["bF"] = b_final_norm
    for q in range(n_chunks):
        for name, fn in (("aX", a_xb_chunk), ("aG", a_gate_chunk), ("aH", a_gated_out),
                         ("aO", a_out_proj), ("bX", b_xb_chunk), ("bP", b_pool),
                         ("bM", b_group_proj), ("bG", b_gate_chunk), ("bZ", b_gated_proj),
                         ("bO", b_out_proj)):
            stages[f"{name}{q}"] = functools.partial(fn, q)
    for c in range(n_slabs):
        for name, fn in (("aC", a_conv), ("aR", a_gate_proj), ("aK", a_coeffs), ("aS", a_scan)):
            stages[f"{name}{c}"] = functools.partial(fn, c)

    order = _prompt_stage_order(n_chunks, slabs_per_chunk)
    assert sorted(order) == sorted(stages), "every stage is issued exactly once"
    for name in order:
        stages[name]()
    h_s[...] = jnp.concatenate(carries, axis=1)

    @pl.when(jnp.logical_and(ta == blocks_per_seq - 1, n < n_blocks))
    def _():
        for c in range(n_slabs):
            conv_ref[0, :, c * LANES:(c + 1) * LANES] = ext_s[c, tb + CONV_PAD - (CONV_W - 1):tb + CONV_PAD, :]
        lru_ref[0] = h_s[...]

    @pl.when(jnp.logical_and(tbk == blocks_per_seq - 1, n > 0))
    def _():
        for c in range(n_slabs):
            pool_ref[0, :, c * LANES:(c + 1) * LANES] = px_s[c, tb + POOL_PAD - POOL_PAST:tb + POOL_PAD, :]

    ext_s[:, 0:CONV_PAD, :] = ext_s[:, tb:tb + CONV_PAD, :]
    px_s[:, 0:POOL_PAD, :] = px_s[:, tb:tb + POOL_PAD, :]


def _sample_kernel(x_hbm, sconv_hbm, slru_hbm, spool_hbm,
                   an_ref, awin_hbm, cw_ref, cb_ref, wri_hbm, br_ref, bi_ref, lam_ref,
                   awout_hbm, bn_ref, bwin_hbm, wgrp_hbm, bgrp_ref, bscale_ref, bwout_hbm, fn_ref,
                   y_hbm, conv_hbm, lru_hbm, pool_hbm,
                   x_v, sconv_v, slru_v, spool_v, awin_v, wri_v, awout_v, bwin_v, wgrp_v, bwout_v,
                   y_v, conv_v, lru_v, xb1_v, load_sem, store_sem):
    ts, nb, d = x_v.shape
    lb = d // N_LRU_BLOCKS
    gw = d // len(POOL_WINDOWS)
    kept = POOL_PAST - ts

    bulk = (("x", x_hbm, x_v), ("awin", awin_hbm, awin_v), ("sconv", sconv_hbm, sconv_v),
            ("wri", wri_hbm, wri_v), ("slru", slru_hbm, slru_v), ("awout", awout_hbm, awout_v),
            ("bwin", bwin_hbm, bwin_v), ("spool", spool_hbm, spool_v), ("wgrp", wgrp_hbm, wgrp_v),
            ("bwout", bwout_hbm, bwout_v))
    loads = {name: pltpu.make_async_copy(src, dst, load_sem.at[i])
             for i, (name, src, dst) in enumerate(bulk)}
    for load in loads.values():
        load.start()
    stores = {
        "conv": pltpu.make_async_copy(conv_v, conv_hbm, store_sem.at[0]),
        "lru": pltpu.make_async_copy(lru_v, lru_hbm, store_sem.at[1]),
        "pool_kept": pltpu.make_async_copy(spool_v.at[pl.ds(ts, kept)], pool_hbm.at[pl.ds(0, kept)],
                                           store_sem.at[2]),
        "pool_new": pltpu.make_async_copy(xb1_v, pool_hbm.at[pl.ds(kept, ts)], store_sem.at[3]),
        "y": pltpu.make_async_copy(y_v, y_hbm, store_sem.at[4]),
    }

    def slab(v, i):
        return v[i * nb:(i + 1) * nb]

    for name in ("x", "awin"):
        loads[name].wait()
    x = x_v[...].reshape(ts * nb, d)
    u = _rmsnorm(x, an_ref[...]).astype(BF16)
    proj = jnp.dot(u, awin_v[...], preferred_element_type=F32)
    xb = proj[:, :d]
    for name in ("sconv", "wri", "slru", "awout"):
        loads[name].wait()
    ext = [sconv_v[k] for k in range(CONV_W - 1)] + [slab(xb, i) for i in range(ts)]
    hcw, hcb = 0.5 * cw_ref[...], 0.5 * cb_ref[...]
    hx_slabs = []
    for i in range(ts):
        acc = hcb
        for k in range(CONV_W):
            acc = acc + ext[i + k] * hcw[k:k + 1, :]
        hx_slabs.append(acc)
    hx = jnp.concatenate(hx_slabs, axis=0)
    hx_bf16 = hx.astype(BF16)
    zr, zi = [], []
    for n in range(N_LRU_BLOCKS):
        z = jnp.dot(hx_bf16[:, n * lb:(n + 1) * lb], wri_v[n], preferred_element_type=F32)
        zr.append(z[:, :lb])
        zi.append(z[:, lb:])
    a, mult, gated = _lru_terms(hx, jnp.concatenate(zr, axis=1), jnp.concatenate(zi, axis=1),
                                0.5 * br_ref[...], 0.5 * bi_ref[...],
                                _half_neg_c_softplus_neg(lam_ref[...]))
    h = slru_v[...]
    hs_slabs = []
    for i in range(ts):
        m = slab(mult, i)
        if PAST_LEN + i == 0:
            m = jnp.ones_like(m)
        h = slab(a, i) * h + m * slab(gated, i)
        hs_slabs.append(h)
    hs = jnp.concatenate(hs_slabs, axis=0)
    y0 = jnp.dot((hs * _silu_from_half(proj[:, d:])).astype(BF16), awout_v[...],
                 preferred_element_type=F32)
    h1 = x + y0
    lru_v[...] = h
    for k in range(CONV_W - 1):
        conv_v[k] = ext[ts + k]
    for name in ("bwin", "spool", "wgrp", "bwout"):
        loads[name].wait()
    for name in ("conv", "lru", "pool_kept"):
        stores[name].start()

    u1 = _rmsnorm(h1, bn_ref[...]).astype(BF16)
    proj1 = jnp.dot(u1, bwin_v[...], preferred_element_type=F32)
    xb1 = proj1[:, :d]
    ext2 = [spool_v[k] for k in range(POOL_PAST)] + [slab(xb1, i) for i in range(ts)]
    pooled_slabs = []
    for i in range(ts):
        means = []
        for gi, w in enumerate(POOL_WINDOWS):
            lanes = slice(gi * gw, (gi + 1) * gw)
            s = ext2[POOL_PAST + i][:, lanes]
            for j in range(1, w):
                s = s + ext2[POOL_PAST + i - j][:, lanes]
            means.append(s * (1.0 / min(PAST_LEN + i + 1, w)))
        pooled_slabs.append(jnp.concatenate(means, axis=1) - ext2[POOL_PAST + i])
    pooled = jnp.concatenate(pooled_slabs, axis=0).astype(BF16)
    z1 = jnp.concatenate(
        [jnp.dot(pooled[:, g * gw:(g + 1) * gw], wgrp_v[g], preferred_element_type=F32)
         for g in range(len(POOL_WINDOWS))], axis=1)
    z1 = (z1 + bgrp_ref[...]) * bscale_ref[...]
    y1 = jnp.dot((z1 * _silu_from_half(proj1[:, d:])).astype(BF16), bwout_v[...],
                 preferred_element_type=F32)
    h2 = h1 + y1
    xb1_v[...] = xb1.reshape(ts, nb, d)
    y_v[...] = _rmsnorm(h2, fn_ref[...]).reshape(ts, nb, d)
    for name in ("pool_new", "y"):
        stores[name].start()
    for store in stores.values():
        store.wait()


def _whole(shape):
    return pl.BlockSpec(shape, lambda *_: (0,) * len(shape), pipeline_mode=pl.Buffered(1))


def _half_gate_columns(w_in, d):
    return jnp.concatenate([w_in[:, :d], 0.5 * w_in[:, d:]], axis=1).astype(BF16)


def kernel(x_prompt, x_sample, state_conv, state_lru, state_pool, a_norm, a_w_in, a_conv_w, a_conv_b, a_w_r, a_b_r, a_w_i, a_b_i, a_lam, a_w_out, b_norm, b_w_in, b_w_grp, b_b_grp, b_scale, b_w_out, final_norm):
    bp, tp, d = x_prompt.shape
    bs, ts, _ = x_sample.shape
    tb = PROMPT_BLOCK_ROWS
    assert a_norm.shape[0] == 1 and b_norm.shape[0] == 1, "one layer of each mixer type"
    assert d % MXU_COLS == 0 and d // N_LRU_BLOCKS == LANES
    assert tp % tb == 0 and tb % (SUBLANES * SUBLANES) == 0 and tb >= 2 * SUBLANES
    assert CONV_W - 1 <= ts <= POOL_PAST

    row = lambda v: v.reshape(1, d)
    wri = jnp.concatenate([a_w_r[0], a_w_i[0]], axis=-1).astype(BF16)
    weights = (row(a_norm[0]), _half_gate_columns(a_w_in[0], d), a_conv_w[0],
               row(a_conv_b[0]), wri, row(a_b_r[0]), row(a_b_i[0]), row(a_lam[0]),
               a_w_out[0].astype(BF16), row(b_norm[0]), _half_gate_columns(b_w_in[0], d),
               b_w_grp[0].astype(BF16), row(b_b_grp[0]), row(b_scale[0]), b_w_out[0].astype(BF16),
               row(final_norm))
    weight_specs = [_whole(w.shape) for w in weights]

    n_slabs = d // LANES
    n_chunks = d // MXU_COLS
    seg_rows = SUBLANES * (tb // SUBLANES + 4)
    nt = tp // tb
    n_blocks = bp * nt
    blk0 = lambda n: jnp.minimum(n, n_blocks - 1)
    blk1 = lambda n: jnp.maximum(n - 1, 0)
    y_p, conv_p, lru_p, pool_p = pl.pallas_call(
        functools.partial(_prompt_kernel, blocks_per_seq=nt),
        grid=(n_blocks + 1,),
        in_specs=[pl.BlockSpec((1, tb, d), lambda n: (blk0(n) // nt, blk0(n) % nt, 0))] + weight_specs,
        out_specs=[pl.BlockSpec((1, tb, d), lambda n: (blk1(n) // nt, blk1(n) % nt, 0)),
                   pl.BlockSpec((1, CONV_W - 1, d), lambda n: (blk0(n) // nt, 0, 0)),
                   pl.BlockSpec((1, 1, d), lambda n: (blk0(n) // nt, 0, 0)),
                   pl.BlockSpec((1, POOL_PAST, d), lambda n: (blk1(n) // nt, 0, 0))],
        out_shape=[jax.ShapeDtypeStruct((bp, tp, d), F32),
                   jax.ShapeDtypeStruct((bp, CONV_W - 1, d), F32),
                   jax.ShapeDtypeStruct((bp, 1, d), F32),
                   jax.ShapeDtypeStruct((bp, POOL_PAST, d), F32)],
        scratch_shapes=[pltpu.VMEM((n_slabs, CONV_PAD + tb, LANES), F32),
                        pltpu.VMEM((n_slabs, seg_rows, LANES), F32),
                        pltpu.VMEM((n_slabs, seg_rows, LANES), F32),
                        pltpu.VMEM((n_slabs, seg_rows, LANES), F32),
                        pltpu.VMEM((1, d), F32),
                        pltpu.VMEM((2, tb, d), F32),
                        pltpu.VMEM((n_slabs, POOL_PAD + tb, LANES), F32),
                        pltpu.VMEM((2, MXU_COLS // LANES, POOL_PAD + tb, LANES), F32),
                        pltpu.VMEM((tb, d), BF16),
                        pltpu.VMEM((2, tb, 2 * LANES), F32),
                        pltpu.VMEM((n_chunks, tb, MXU_COLS), F32),
                        pltpu.VMEM((tb, d), BF16),
                        pltpu.VMEM((tb, d), BF16),
                        pltpu.VMEM((tb, d), BF16),
                        pltpu.VMEM((tb, d), F32),
                        pltpu.VMEM((n_chunks, tb, MXU_COLS), F32),
                        pltpu.VMEM((tb, d), BF16)],
        compiler_params=pltpu.CompilerParams(
            vmem_limit_bytes=VMEM_LIMIT_BYTES,
            dimension_semantics=("arbitrary",)),
        name="prompt_step",
    )(x_prompt, *weights)

    xs = jnp.swapaxes(x_sample, 0, 1)
    sconv = jnp.swapaxes(state_conv[0], 0, 1)
    spool = jnp.swapaxes(state_pool[0], 0, 1)
    hbm = pl.BlockSpec(memory_space=pl.ANY)
    in_hbm = {1, 4, 8, 10, 11, 14}
    bulk_weights = [weights[i] for i in sorted(in_hbm)]
    act = lambda shape: pltpu.VMEM(shape, F32)
    y_s, conv_s, lru_s, pool_s = pl.pallas_call(
        _sample_kernel,
        in_specs=[hbm] * 4 + [hbm if i in in_hbm else _whole(w.shape) for i, w in enumerate(weights)],
        out_specs=[hbm] * 4,
        out_shape=[jax.ShapeDtypeStruct(xs.shape, F32),
                   jax.ShapeDtypeStruct(sconv.shape, F32),
                   jax.ShapeDtypeStruct((bs, d), F32),
                   jax.ShapeDtypeStruct(spool.shape, F32)],
        scratch_shapes=[act(xs.shape), act(sconv.shape), act((bs, d)), act(spool.shape)]
        + [pltpu.VMEM(w.shape, w.dtype) for w in bulk_weights]
        + [act(xs.shape), act(sconv.shape), act((bs, d)), act(xs.shape),
           pltpu.SemaphoreType.DMA((4 + len(bulk_weights),)), pltpu.SemaphoreType.DMA((5,))],
        compiler_params=pltpu.CompilerParams(vmem_limit_bytes=VMEM_LIMIT_BYTES),
        name="sample_step",
    )(xs, sconv, state_lru[0], spool, *weights)

    return (y_p, jnp.swapaxes(y_s, 0, 1),
            conv_p[None], lru_p.reshape(1, bp, d), pool_p[None],
            jnp.swapaxes(conv_s, 0, 1)[None], lru_s[None], jnp.swapaxes(pool_s, 0, 1)[None])
```

```python
import functools

import jax
import jax.numpy as jnp
from jax import lax
from jax.experimental import pallas as pl
from jax.experimental.pallas import tpu as pltpu

PAST_LEN = 16384
N_LRU_BLOCKS = 8
CONV_W = 4
LRU_C = 8.0
POOL_WINDOWS = (2, 4, 8, 16)
POOL_PAST = max(POOL_WINDOWS) - 1
EPS = 1e-6

LANES = 128
SUBLANES = 8
MXU_COLS = 256
CONV_PAD = SUBLANES
POOL_LEVELS = max(POOL_WINDOWS).bit_length() - 1
POOL_PAD = POOL_LEVELS * SUBLANES
PROMPT_BLOCK_ROWS = 512
CONV_AHEAD = 4
VMEM_LIMIT_BYTES = 56 * 1024 * 1024

F32 = jnp.float32
BF16 = jnp.bfloat16


def _rmsnorm(x, g):
    return x * lax.rsqrt(jnp.mean(x * x, axis=-1, keepdims=True) + EPS) * g


def _silu_from_half(hg):
    return hg + hg * jnp.tanh(hg)


def _half_neg_c_softplus_neg(lam):
    z = -lam
    return (-0.5 * LRU_C) * (jnp.maximum(z, 0.0) + jnp.log1p(jnp.exp(-jnp.abs(z))))


def _lru_terms(hx, zr, zi, hbr, hbi, hn):
    log_a = hn + hn * jnp.tanh(zr + hbr)
    a = jnp.exp(log_a)
    y = jnp.tanh(log_a) * (-1.0 - a * a)
    mult = jnp.where(y > 0.0, y * lax.rsqrt(y), 0.0)
    gated = hx + hx * jnp.tanh(zi + hbi)
    return a, mult, gated


def _lane_slab(ref, c):
    return ref[:, c * LANES:(c + 1) * LANES]


def _cols(q):
    return slice(q * MXU_COLS, (q + 1) * MXU_COLS)


def _prompt_stage_order(n_chunks, slabs_per_chunk):
    assert (n_chunks, slabs_per_chunk) == (4, 2)
    return ("aN aX0 bN aX1 aC0 bX0 aC1 aC2 aC3 aR0 aR1 aK0 aX2 aS0 aK1 bX1 aS1 bP0 aR2 aC4 aR3 bM0 aC5 aX3 "
            "aK2 aS2 bX2 aK3 aS3 aR4 bP1 aR5 bM1 aC6 aG0 aC7 aK4 bX3 aS4 aK5 aR6 aR7 aS5 bP2 bM2 aK6 aG1 aS6 "
            "aK7 bG0 aS7 bP3 bM3 aH0 aG2 aH1 bZ0 bG1 aH2 aG3 bZ1 bG2 aH3 bG3 bZ2 bZ3 bO0 bO1 bO2 bO3 aO0 bF "
            "aO1 aO2 aO3").split()


def _prompt_kernel(x_ref, an_ref, awin_ref, cw_ref, cb_ref, wri_ref, br_ref, bi_ref, lam_ref,
                   awout_ref, bn_ref, bwin_ref, wgrp_ref, bgrp_ref, bscale_ref, bwout_ref, fn_ref,
                   y_ref, conv_ref, lru_ref, pool_ref,
                   ext_s, sa_s, sb_s, sh_s, h_s, h1_s, px_s, pt_s,
                   ua_s, hx_s, hxb_s, zr_s, ga_s, hsg_s, ub_s, pooled_s, z_s, gq_s, zg_s, *,
                   blocks_per_seq):
    n = pl.program_id(0)
    n_blocks = pl.num_programs(0) - 1
    ta = jnp.minimum(n, n_blocks - 1) % blocks_per_seq
    tbk = jnp.maximum(n - 1, 0) % blocks_per_seq
    tb, d = x_ref.shape[1], x_ref.shape[2]
    n_slabs = d // LANES
    n_chunks = d // MXU_COLS
    slabs_per_chunk = MXU_COLS // LANES
    assert d // len(POOL_WINDOWS) == MXU_COLS, "one pooling group per matmul column chunk"
    seg = tb // SUBLANES
    pitch = seg + 4
    conv_ahead = hx_s.shape[0]

    @pl.when(n == 0)
    def _():
        h1_s[...] = jnp.zeros_like(h1_s)

    @pl.when(ta == 0)
    def _():
        ext_s[:, 0:CONV_PAD, :] = jnp.zeros((n_slabs, CONV_PAD, LANES), F32)
        h_s[...] = jnp.zeros_like(h_s)

    @pl.when(tbk == 0)
    def _():
        px_s[:, 0:POOL_PAD, :] = jnp.zeros((n_slabs, POOL_PAD, LANES), F32)

    h1_in = h1_s.at[(n + 1) % 2]
    h1_out = h1_s.at[n % 2]
    sub = lax.broadcasted_iota(jnp.int32, (SUBLANES, 1), 0)
    first_token = jnp.logical_and(ta == 0, sub == 0)
    row16 = lax.broadcasted_iota(jnp.int32, (2 * SUBLANES, 1), 0)
    pos16 = tbk * tb + row16
    hn_row = _half_neg_c_softplus_neg(lam_ref[...])
    hcw, hcb = 0.5 * cw_ref[...], 0.5 * cb_ref[...]
    hbr, hbi = 0.5 * br_ref[...], 0.5 * bi_ref[...]
    carries = [None] * n_slabs
    stages = {}

    def a_norm():
        ua_s[...] = _rmsnorm(x_ref[0], an_ref[...]).astype(BF16)

    def a_xb_chunk(q):
        pq = jnp.dot(ua_s[...], awin_ref[:, _cols(q)], preferred_element_type=F32)
        for i in range(slabs_per_chunk):
            ext_s[q * slabs_per_chunk + i, CONV_PAD:CONV_PAD + tb, :] = pq[:, i * LANES:(i + 1) * LANES]

    def a_gate_chunk(q):
        ga_s[q] = jnp.dot(ua_s[...], awin_ref[:, _cols(n_chunks + q)], preferred_element_type=F32)

    def a_conv(c):
        lanes = slice(c * LANES, (c + 1) * LANES)
        for j in range(SUBLANES):
            r0 = CONV_PAD + j * seg
            hx = ext_s[c, r0:r0 + seg, :] * hcw[CONV_W - 1:CONV_W, lanes] + hcb[:, lanes]
            for k in range(CONV_W - 1):
                hx = hx + ext_s[c, pl.ds(r0 - (CONV_W - 1) + k, seg), :] * hcw[k:k + 1, lanes]
            hx_s[c % conv_ahead, j * seg:(j + 1) * seg, :] = hx
            hxb_s[c % conv_ahead, j * seg:(j + 1) * seg, :] = hx.astype(BF16)

    def a_gate_proj(c):
        zr_s[c % 2] = jnp.dot(hxb_s[c % conv_ahead], wri_ref[c], preferred_element_type=F32)

    def a_coeffs(c):
        lanes = slice(c * LANES, (c + 1) * LANES)
        for j in range(SUBLANES):
            rows = slice(j * seg, (j + 1) * seg)
            a, mult, gated = _lru_terms(hx_s[c % conv_ahead, rows, :], zr_s[c % 2, rows, :LANES],
                                        zr_s[c % 2, rows, LANES:], hbr[:, lanes], hbi[:, lanes],
                                        hn_row[:, lanes])
            if j == 0:
                mult = jnp.concatenate(
                    [jnp.where(first_token, 1.0, mult[0:SUBLANES]), mult[SUBLANES:]], axis=0)
            sa_s[c, pl.ds(j * pitch, seg), :] = a
            sb_s[c, pl.ds(j * pitch, seg), :] = mult * gated

    def a_scan(c):
        hl = sb_s[c, pl.ds(0, SUBLANES, stride=pitch), :]
        pp = sa_s[c, pl.ds(0, SUBLANES, stride=pitch), :]
        for i in range(1, seg):
            av = sa_s[c, pl.ds(i, SUBLANES, stride=pitch), :]
            hl = av * hl + sb_s[c, pl.ds(i, SUBLANES, stride=pitch), :]
            pp = av * pp
        h_in = jnp.broadcast_to(_lane_slab(h_s, c), (SUBLANES, LANES))
        cin = h_in
        for _ in range(SUBLANES - 1):
            cin = jnp.where(sub == 0, h_in, pltpu.roll(hl + pp * cin, 1, 0))
        carries[c] = (hl + pp * cin)[SUBLANES - 1:SUBLANES]
        h = cin
        for i in range(seg):
            h = sa_s[c, pl.ds(i, SUBLANES, stride=pitch), :] * h + sb_s[c, pl.ds(i, SUBLANES, stride=pitch), :]
            sh_s[c, pl.ds(i, SUBLANES, stride=pitch), :] = h

    def a_gated_out(q):
        hs = jnp.concatenate(
            [jnp.concatenate([sh_s[c, pl.ds(j * pitch, seg), :] for j in range(SUBLANES)], axis=0)
             for c in range(q * slabs_per_chunk, (q + 1) * slabs_per_chunk)], axis=1)
        hsg_s[:, _cols(q)] = (hs * _silu_from_half(ga_s[q])).astype(BF16)

    def a_out_proj(q):
        y0 = jnp.dot(hsg_s[...], awout_ref[:, _cols(q)], preferred_element_type=F32)
        h1_out[:, _cols(q)] = x_ref[0, :, _cols(q)] + y0

    def b_norm():
        ub_s[...] = _rmsnorm(h1_in[...], bn_ref[...]).astype(BF16)

    def b_xb_chunk(q):
        pq = jnp.dot(ub_s[...], bwin_ref[:, _cols(q)], preferred_element_type=F32)
        for i in range(slabs_per_chunk):
            px_s[q * slabs_per_chunk + i, POOL_PAD:POOL_PAD + tb, :] = pq[:, i * LANES:(i + 1) * LANES]

    def pool_slab(c, w):
        levels = w.bit_length() - 1
        tmp = c % slabs_per_chunk
        src = px_s.at[c]
        s = None
        for k in range(levels):
            back = (levels - 1 - k) * SUBLANES
            rows = tb + back
            s = (src[pl.ds(POOL_PAD - back, rows), :]
                 + src[pl.ds(POOL_PAD - back - (1 << k), rows), :])
            if k < levels - 1:
                pt_s[k % 2, tmp, POOL_PAD - back:POOL_PAD + tb, :] = s
                src = pt_s.at[k % 2, tmp]
        inv_cnt = 1.0 / jnp.minimum(pos16 + 1, w).astype(F32)
        mean = jnp.concatenate([s[0:2 * SUBLANES] * inv_cnt, s[2 * SUBLANES:] * (1.0 / w)], axis=0)
        return mean - px_s[c, POOL_PAD:POOL_PAD + tb, :]

    def b_pool(q):
        pooled_s[:, _cols(q)] = jnp.concatenate(
            [pool_slab(q * slabs_per_chunk + i, POOL_WINDOWS[q]) for i in range(slabs_per_chunk)],
            axis=1).astype(BF16)

    def b_group_proj(q):
        z_s[:, _cols(q)] = jnp.dot(pooled_s[:, _cols(q)], wgrp_ref[q], preferred_element_type=F32)

    def b_gate_chunk(q):
        gq_s[q] = jnp.dot(ub_s[...], bwin_ref[:, _cols(n_chunks + q)], preferred_element_type=F32)

    def b_gated_proj(q):
        z1 = (z_s[:, _cols(q)] + bgrp_ref[:, _cols(q)]) * bscale_ref[:, _cols(q)]
        zg_s[:, _cols(q)] = (z1 * _silu_from_half(gq_s[q])).astype(BF16)

    def b_out_proj(q):
        y1 = jnp.dot(zg_s[...], bwout_ref[:, _cols(q)], preferred_element_type=F32)
        y_ref[0, :, _cols(q)] = h1_in[:, _cols(q)] + y1

    def b_final_norm():
        y_ref[0] = _rmsnorm(y_ref[0], fn_ref[...])

    stages["aN"] = a_norm
    stages["bN"] = b_norm
    stages["bF"] = b_final_norm
    for q in range(n_chunks):
        for name, fn in (("aX", a_xb_chunk), ("aG", a_gate_chunk), ("aH", a_gated_out),
                         ("aO", a_out_proj), ("bX", b_xb_chunk), ("bP", b_pool),
                         ("bM", b_group_proj), ("bG", b_gate_chunk), ("bZ", b_gated_proj),
                         ("bO", b_out_proj)):
            stages[f"{name}{q}"] = functools.partial(fn, q)
    for c in range(n_slabs):
        for name, fn in (("aC", a_conv), ("aR", a_gate_proj), ("aK", a_coeffs), ("aS", a_scan)):
            stages[f"{name}{c}"] = functools.partial(fn, c)

    order = _prompt_stage_order(n_chunks, slabs_per_chunk)
    assert sorted(order) == sorted(stages), "every stage is issued exactly once"
    at = order.index
    for c in range(n_slabs):
        assert c < conv_ahead or at(f"aK{c - conv_ahead}") < at(f"aC{c}")
        assert c < zr_s.shape[0] or at(f"aK{c - zr_s.shape[0]}") < at(f"aR{c}")
    for name in order:
        stages[name]()
    h_s[...] = jnp.concatenate(carries, axis=1)

    @pl.when(jnp.logical_and(ta == blocks_per_seq - 1, n < n_blocks))
    def _():
        for c in range(n_slabs):
            conv_ref[0, :, c * LANES:(c + 1) * LANES] = ext_s[c, tb + CONV_PAD - (CONV_W - 1):tb + CONV_PAD, :]
        lru_ref[0] = h_s[...]

    @pl.when(jnp.logical_and(tbk == blocks_per_seq - 1, n > 0))
    def _():
        for c in range(n_slabs):
            pool_ref[0, :, c * LANES:(c + 1) * LANES] = px_s[c, tb + POOL_PAD - POOL_PAST:tb + POOL_PAD, :]

    ext_s[:, 0:CONV_PAD, :] = ext_s[:, tb:tb + CONV_PAD, :]
    px_s[:, 0:POOL_PAD, :] = px_s[:, tb:tb + POOL_PAD, :]


def _sample_kernel(x_hbm, sconv_hbm, slru_hbm, spool_hbm,
                   an_ref, awin_hbm, cw_ref, cb_ref, wri_hbm, br_ref, bi_ref, lam_ref,
                   awout_hbm, bn_ref, bwin_hbm, wgrp_hbm, bgrp_ref, bscale_ref, bwout_hbm, fn_ref,
                   y_hbm, conv_hbm, lru_hbm, pool_hbm,
                   x_v, sconv_v, slru_v, spool_v, awin_v, wri_v, awout_v, bwin_v, wgrp_v, bwout_v,
                   y_v, conv_v, lru_v, xb1_v, load_sem, store_sem):
    ts, nb, d = x_v.shape
    lb = d // N_LRU_BLOCKS
    gw = d // len(POOL_WINDOWS)
    kept = POOL_PAST - ts

    bulk = (("x", x_hbm, x_v), ("awin", awin_hbm, awin_v), ("sconv", sconv_hbm, sconv_v),
            ("wri", wri_hbm, wri_v), ("slru", slru_hbm, slru_v), ("awout", awout_hbm, awout_v),
            ("bwin", bwin_hbm, bwin_v), ("spool", spool_hbm, spool_v), ("wgrp", wgrp_hbm, wgrp_v),
            ("bwout", bwout_hbm, bwout_v))
    loads = {name: pltpu.make_async_copy(src, dst, load_sem.at[i])
             for i, (name, src, dst) in enumerate(bulk)}
    for load in loads.values():
        load.start()
    stores = {
        "conv": pltpu.make_async_copy(conv_v, conv_hbm, store_sem.at[0]),
        "lru": pltpu.make_async_copy(lru_v, lru_hbm, store_sem.at[1]),
        "pool_kept": pltpu.make_async_copy(spool_v.at[pl.ds(ts, kept)], pool_hbm.at[pl.ds(0, kept)],
                                           store_sem.at[2]),
        "pool_new": pltpu.make_async_copy(xb1_v, pool_hbm.at[pl.ds(kept, ts)], store_sem.at[3]),
        "y": pltpu.make_async_copy(y_v, y_hbm, store_sem.at[4]),
    }

    def slab(v, i):
        return v[i * nb:(i + 1) * nb]

    loads["x"].wait()
    x = x_v[...].reshape(ts * nb, d)
    u = _rmsnorm(x, an_ref[...]).astype(BF16)
    loads["awin"].wait()
    proj = jnp.dot(u, awin_v[...], preferred_element_type=F32)
    xb = proj[:, :d]
    for name in ("sconv", "wri", "slru", "awout"):
        loads[name].wait()
    ext = [sconv_v[k] for k in range(CONV_W - 1)] + [slab(xb, i) for i in range(ts)]
    hcw, hcb = 0.5 * cw_ref[...], 0.5 * cb_ref[...]
    hx_slabs = []
    for i in range(ts):
        acc = hcb
        for k in range(CONV_W):
            acc = acc + ext[i + k] * hcw[k:k + 1, :]
        hx_slabs.append(acc)
    hx = jnp.concatenate(hx_slabs, axis=0)
    hx_bf16 = hx.astype(BF16)
    zr, zi = [], []
    for n in range(N_LRU_BLOCKS):
        z = jnp.dot(hx_bf16[:, n * lb:(n + 1) * lb], wri_v[n], preferred_element_type=F32)
        zr.append(z[:, :lb])
        zi.append(z[:, lb:])
    a, mult, gated = _lru_terms(hx, jnp.concatenate(zr, axis=1), jnp.concatenate(zi, axis=1),
                                0.5 * br_ref[...], 0.5 * bi_ref[...],
                                _half_neg_c_softplus_neg(lam_ref[...]))
    h = slru_v[...]
    hs_slabs = []
    for i in range(ts):
        m = slab(mult, i)
        if PAST_LEN + i == 0:
            m = jnp.ones_like(m)
        h = slab(a, i) * h + m * slab(gated, i)
        hs_slabs.append(h)
    hs = jnp.concatenate(hs_slabs, axis=0)
    y0 = jnp.dot((hs * _silu_from_half(proj[:, d:])).astype(BF16), awout_v[...],
                 preferred_element_type=F32)
    h1 = x + y0
    lru_v[...] = h
    for k in range(CONV_W - 1):
        conv_v[k] = ext[ts + k]
    for name in ("bwin", "spool", "wgrp", "bwout"):
        loads[name].wait()
    for name in ("conv", "lru", "pool_kept"):
        stores[name].start()

    u1 = _rmsnorm(h1, bn_ref[...]).astype(BF16)
    proj1 = jnp.dot(u1, bwin_v[...], preferred_element_type=F32)
    xb1 = proj1[:, :d]
    ext2 = [spool_v[k] for k in range(POOL_PAST)] + [slab(xb1, i) for i in range(ts)]
    pooled_slabs = []
    for i in range(ts):
        means = []
        for gi, w in enumerate(POOL_WINDOWS):
            lanes = slice(gi * gw, (gi + 1) * gw)
            s = ext2[POOL_PAST + i][:, lanes]
            for j in range(1, w):
                s = s + ext2[POOL_PAST + i - j][:, lanes]
            means.append(s * (1.0 / min(PAST_LEN + i + 1, w)))
        pooled_slabs.append(jnp.concatenate(means, axis=1) - ext2[POOL_PAST + i])
    pooled = jnp.concatenate(pooled_slabs, axis=0).astype(BF16)
    z1 = jnp.concatenate(
        [jnp.dot(pooled[:, g * gw:(g + 1) * gw], wgrp_v[g], preferred_element_type=F32)
         for g in range(len(POOL_WINDOWS))], axis=1)
    z1 = (z1 + bgrp_ref[...]) * bscale_ref[...]
    y1 = jnp.dot((z1 * _silu_from_half(proj1[:, d:])).astype(BF16), bwout_v[...],
                 preferred_element_type=F32)
    h2 = h1 + y1
    xb1_v[...] = xb1.reshape(ts, nb, d)
    y_v[...] = _rmsnorm(h2, fn_ref[...]).reshape(ts, nb, d)
    for name in ("pool_new", "y"):
        stores[name].start()
    for store in stores.values():
        store.wait()


def _whole(shape):
    return pl.BlockSpec(shape, lambda *_: (0,) * len(shape), pipeline_mode=pl.Buffered(1))


def _half_gate_columns(w_in, d):
    return jnp.concatenate([w_in[:, :d], 0.5 * w_in[:, d:]], axis=1).astype(BF16)


def kernel(x_prompt, x_sample, state_conv, state_lru, state_pool, a_norm, a_w_in, a_conv_w, a_conv_b, a_w_r, a_b_r, a_w_i, a_b_i, a_lam, a_w_out, b_norm, b_w_in, b_w_grp, b_b_grp, b_scale, b_w_out, final_norm):
    bp, tp, d = x_prompt.shape
    bs, ts, _ = x_sample.shape
    tb = PROMPT_BLOCK_ROWS
    assert a_norm.shape[0] == 1 and b_norm.shape[0] == 1, "one layer of each mixer type"
    assert d % MXU_COLS == 0 and d // N_LRU_BLOCKS == LANES
    assert tp % tb == 0 and tb % (SUBLANES * SUBLANES) == 0 and tb >= 2 * SUBLANES
    assert CONV_W - 1 <= ts <= POOL_PAST

    row = lambda v: v.reshape(1, d)
    wri = jnp.concatenate([a_w_r[0], a_w_i[0]], axis=-1).astype(BF16)
    weights = (row(a_norm[0]), _half_gate_columns(a_w_in[0], d), a_conv_w[0],
               row(a_conv_b[0]), wri, row(a_b_r[0]), row(a_b_i[0]), row(a_lam[0]),
               a_w_out[0].astype(BF16), row(b_norm[0]), _half_gate_columns(b_w_in[0], d),
               b_w_grp[0].astype(BF16), row(b_b_grp[0]), row(b_scale[0]), b_w_out[0].astype(BF16),
               row(final_norm))
    weight_specs = [_whole(w.shape) for w in weights]

    n_slabs = d // LANES
    n_chunks = d // MXU_COLS
    seg_rows = SUBLANES * (tb // SUBLANES + 4)
    nt = tp // tb
    n_blocks = bp * nt
    blk0 = lambda n: jnp.minimum(n, n_blocks - 1)
    blk1 = lambda n: jnp.maximum(n - 1, 0)
    y_p, conv_p, lru_p, pool_p = pl.pallas_call(
        functools.partial(_prompt_kernel, blocks_per_seq=nt),
        grid=(n_blocks + 1,),
        in_specs=[pl.BlockSpec((1, tb, d), lambda n: (blk0(n) // nt, blk0(n) % nt, 0))] + weight_specs,
        out_specs=[pl.BlockSpec((1, tb, d), lambda n: (blk1(n) // nt, blk1(n) % nt, 0)),
                   pl.BlockSpec((1, CONV_W - 1, d), lambda n: (blk0(n) // nt, 0, 0)),
                   pl.BlockSpec((1, 1, d), lambda n: (blk0(n) // nt, 0, 0)),
                   pl.BlockSpec((1, POOL_PAST, d), lambda n: (blk1(n) // nt, 0, 0))],
        out_shape=[jax.ShapeDtypeStruct((bp, tp, d), F32),
                   jax.ShapeDtypeStruct((bp, CONV_W - 1, d), F32),
                   jax.ShapeDtypeStruct((bp, 1, d), F32),
                   jax.ShapeDtypeStruct((bp, POOL_PAST, d), F32)],
        scratch_shapes=[pltpu.VMEM((n_slabs, CONV_PAD + tb, LANES), F32),
                        pltpu.VMEM((n_slabs, seg_rows, LANES), F32),
                        pltpu.VMEM((n_slabs, seg_rows, LANES), F32),
                        pltpu.VMEM((n_slabs, seg_rows, LANES), F32),
                        pltpu.VMEM((1, d), F32),
                        pltpu.VMEM((2, tb, d), F32),
                        pltpu.VMEM((n_slabs, POOL_PAD + tb, LANES), F32),
                        pltpu.VMEM((2, MXU_COLS // LANES, POOL_PAD + tb, LANES), F32),
                        pltpu.VMEM((tb, d), BF16),
                        pltpu.VMEM((CONV_AHEAD, tb, LANES), F32),
                        pltpu.VMEM((CONV_AHEAD, tb, LANES), BF16),
                        pltpu.VMEM((2, tb, 2 * LANES), F32),
                        pltpu.VMEM((n_chunks, tb, MXU_COLS), F32),
                        pltpu.VMEM((tb, d), BF16),
                        pltpu.VMEM((tb, d), BF16),
                        pltpu.VMEM((tb, d), BF16),
                        pltpu.VMEM((tb, d), F32),
                        pltpu.VMEM((n_chunks, tb, MXU_COLS), F32),
                        pltpu.VMEM((tb, d), BF16)],
        compiler_params=pltpu.CompilerParams(
            vmem_limit_bytes=VMEM_LIMIT_BYTES,
            dimension_semantics=("arbitrary",)),
        name="prompt_step",
    )(x_prompt, *weights)

    xs = jnp.swapaxes(x_sample, 0, 1)
    sconv = jnp.swapaxes(state_conv[0], 0, 1)
    spool = jnp.swapaxes(state_pool[0], 0, 1)
    hbm = pl.BlockSpec(memory_space=pl.ANY)
    in_hbm = {1, 4, 8, 10, 11, 14}
    bulk_weights = [weights[i] for i in sorted(in_hbm)]
    act = lambda shape: pltpu.VMEM(shape, F32)
    y_s, conv_s, lru_s, pool_s = pl.pallas_call(
        _sample_kernel,
        in_specs=[hbm] * 4 + [hbm if i in in_hbm else _whole(w.shape) for i, w in enumerate(weights)],
        out_specs=[hbm] * 4,
        out_shape=[jax.ShapeDtypeStruct(xs.shape, F32),
                   jax.ShapeDtypeStruct(sconv.shape, F32),
                   jax.ShapeDtypeStruct((bs, d), F32),
                   jax.ShapeDtypeStruct(spool.shape, F32)],
        scratch_shapes=[act(xs.shape), act(sconv.shape), act((bs, d)), act(spool.shape)]
        + [pltpu.VMEM(w.shape, w.dtype) for w in bulk_weights]
        + [act(xs.shape), act(sconv.shape), act((bs, d)), act(xs.shape),
           pltpu.SemaphoreType.DMA((4 + len(bulk_weights),)), pltpu.SemaphoreType.DMA((5,))],
        compiler_params=pltpu.CompilerParams(vmem_limit_bytes=VMEM_LIMIT_BYTES),
        name="sample_step",
    )(xs, sconv, state_lru[0], spool, *weights)

    return (y_p, jnp.swapaxes(y_s, 0, 1),
            conv_p[None], lru_p.reshape(1, bp, d), pool_p[None],
            jnp.swapaxes(conv_s, 0, 1)[None], lru_s[None], jnp.swapaxes(pool_s, 0, 1)[None])
```

```python
import functools

import jax
import jax.numpy as jnp
from jax import lax
from jax.experimental import pallas as pl
from jax.experimental.pallas import tpu as pltpu

PAST_LEN = 16384
N_LRU_BLOCKS = 8
CONV_W = 4
LRU_C = 8.0
POOL_WINDOWS = (2, 4, 8, 16)
POOL_PAST = max(POOL_WINDOWS) - 1
EPS = 1e-6

LANES = 128
SUBLANES = 8
MXU_COLS = 256
CONV_PAD = SUBLANES
POOL_LEVELS = max(POOL_WINDOWS).bit_length() - 1
POOL_PAD = POOL_LEVELS * SUBLANES
PROMPT_BLOCK_ROWS = 512
CONV_AHEAD = 4
VMEM_LIMIT_BYTES = 56 * 1024 * 1024

F32 = jnp.float32
BF16 = jnp.bfloat16


def _rmsnorm(x, g):
    return x * lax.rsqrt(jnp.mean(x * x, axis=-1, keepdims=True) + EPS) * g


def _silu_from_half(hg):
    return hg + hg * jnp.tanh(hg)


def _half_neg_c_softplus_neg(lam):
    z = -lam
    return (-0.5 * LRU_C) * (jnp.maximum(z, 0.0) + jnp.log1p(jnp.exp(-jnp.abs(z))))


def _lru_terms(hx, zr, zi, hbr, hbi, hn):
    log_a = hn + hn * jnp.tanh(zr + hbr)
    a = jnp.exp(log_a)
    y = jnp.tanh(log_a) * (-1.0 - a * a)
    mult = jnp.where(y > 0.0, y * lax.rsqrt(y), 0.0)
    gated = hx + hx * jnp.tanh(zi + hbi)
    return a, mult, gated


def _lane_slab(ref, c):
    return ref[:, c * LANES:(c + 1) * LANES]


def _cols(q):
    return slice(q * MXU_COLS, (q + 1) * MXU_COLS)


def _prompt_stage_order(n_chunks, slabs_per_chunk):
    assert (n_chunks, slabs_per_chunk) == (4, 2)
    return ("aN aX0 bN aX1 aC0 bX0 aC1 aC2 aC3 aR0 aR1 aK0 aX2 aS0 aK1 bX1 aS1 bP0 aR2 aC4 aR3 bM0 aC5 aX3 "
            "aK2 aS2 bX2 aK3 aS3 aR4 bP1 aR5 bM1 aC6 aG0 aC7 aK4 bX3 aS4 aK5 aR6 aR7 aS5 bP2 bM2 aK6 aG1 aS6 "
            "aK7 bG0 aS7 bP3 bM3 aH0 aG2 aH1 bZ0 bG1 aH2 aG3 bZ1 bG2 aH3 bG3 bZ2 bZ3 bO0 bO1 bO2 bO3 aO0 bF "
            "aO1 aO2 aO3").split()


def _prompt_kernel(x_ref, an_ref, awin_ref, cw_ref, cb_ref, wri_ref, br_ref, bi_ref, lam_ref,
                   awout_ref, bn_ref, bwin_ref, wgrp_ref, bgrp_ref, bscale_ref, bwout_ref, fn_ref,
                   y_ref, conv_ref, lru_ref, pool_ref,
                   ext_s, sa_s, sb_s, sh_s, h_s, h1_s, px_s, pt_s,
                   ua_s, hx_s, hxb_s, zr_s, ga_s, hsg_s, ub_s, pooled_s, z_s, gq_s, zg_s, *,
                   blocks_per_seq):
    n = pl.program_id(0)
    n_blocks = pl.num_programs(0) - 1
    ta = jnp.minimum(n, n_blocks - 1) % blocks_per_seq
    tbk = jnp.maximum(n - 1, 0) % blocks_per_seq
    tb, d = x_ref.shape[1], x_ref.shape[2]
    n_slabs = d // LANES
    n_chunks = d // MXU_COLS
    slabs_per_chunk = MXU_COLS // LANES
    assert d // len(POOL_WINDOWS) == MXU_COLS, "one pooling group per matmul column chunk"
    seg = tb // SUBLANES
    pitch = seg + 4
    conv_ahead = hx_s.shape[0]

    @pl.when(n == 0)
    def _():
        h1_s[...] = jnp.zeros_like(h1_s)

    @pl.when(ta == 0)
    def _():
        ext_s[:, 0:CONV_PAD, :] = jnp.zeros((n_slabs, CONV_PAD, LANES), F32)
        h_s[...] = jnp.zeros_like(h_s)

    @pl.when(tbk == 0)
    def _():
        px_s[:, 0:POOL_PAD, :] = jnp.zeros((n_slabs, POOL_PAD, LANES), F32)

    h1_in = h1_s.at[(n + 1) % 2]
    h1_out = h1_s.at[n % 2]
    sub = lax.broadcasted_iota(jnp.int32, (SUBLANES, 1), 0)
    first_token = jnp.logical_and(ta == 0, sub == 0)
    row16 = lax.broadcasted_iota(jnp.int32, (2 * SUBLANES, 1), 0)
    pos16 = tbk * tb + row16
    hn_row = _half_neg_c_softplus_neg(lam_ref[...])
    hcw, hcb = 0.5 * cw_ref[...], 0.5 * cb_ref[...]
    hbr, hbi = 0.5 * br_ref[...], 0.5 * bi_ref[...]
    carries = [None] * n_slabs
    stages = {}

    def a_norm():
        ua_s[...] = _rmsnorm(x_ref[0], an_ref[...]).astype(BF16)

    def a_xb_chunk(q):
        pq = jnp.dot(ua_s[...], awin_ref[:, _cols(q)], preferred_element_type=F32)
        for i in range(slabs_per_chunk):
            ext_s[q * slabs_per_chunk + i, CONV_PAD:CONV_PAD + tb, :] = pq[:, i * LANES:(i + 1) * LANES]

    def a_gate_chunk(q):
        ga_s[q] = jnp.dot(ua_s[...], awin_ref[:, _cols(n_chunks + q)], preferred_element_type=F32)

    def a_conv(c):
        lanes = slice(c * LANES, (c + 1) * LANES)
        for j in range(SUBLANES):
            r0 = CONV_PAD + j * seg
            hx = ext_s[c, r0:r0 + seg, :] * hcw[CONV_W - 1:CONV_W, lanes] + hcb[:, lanes]
            for k in range(CONV_W - 1):
                hx = hx + ext_s[c, pl.ds(r0 - (CONV_W - 1) + k, seg), :] * hcw[k:k + 1, lanes]
            hx_s[c % conv_ahead, j * seg:(j + 1) * seg, :] = hx
            hxb_s[c % conv_ahead, j * seg:(j + 1) * seg, :] = hx.astype(BF16)

    def a_gate_proj(c):
        zr_s[c % 2] = jnp.dot(hxb_s[c % conv_ahead], wri_ref[c], preferred_element_type=F32)

    def a_coeffs(c):
        lanes = slice(c * LANES, (c + 1) * LANES)
        for j in range(SUBLANES):
            rows = slice(j * seg, (j + 1) * seg)
            a, mult, gated = _lru_terms(hx_s[c % conv_ahead, rows, :], zr_s[c % 2, rows, :LANES],
                                        zr_s[c % 2, rows, LANES:], hbr[:, lanes], hbi[:, lanes],
                                        hn_row[:, lanes])
            if j == 0:
                mult = jnp.concatenate(
                    [jnp.where(first_token, 1.0, mult[0:SUBLANES]), mult[SUBLANES:]], axis=0)
            sa_s[c, pl.ds(j * pitch, seg), :] = a
            sb_s[c, pl.ds(j * pitch, seg), :] = mult * gated

    def a_scan(c):
        hl = sb_s[c, pl.ds(0, SUBLANES, stride=pitch), :]
        pp = sa_s[c, pl.ds(0, SUBLANES, stride=pitch), :]
        for i in range(1, seg):
            av = sa_s[c, pl.ds(i, SUBLANES, stride=pitch), :]
            hl = av * hl + sb_s[c, pl.ds(i, SUBLANES, stride=pitch), :]
            pp = av * pp
        h_in = jnp.broadcast_to(_lane_slab(h_s, c), (SUBLANES, LANES))
        cin = h_in
        for _ in range(SUBLANES - 1):
            cin = jnp.where(sub == 0, h_in, pltpu.roll(hl + pp * cin, 1, 0))
        carries[c] = (hl + pp * cin)[SUBLANES - 1:SUBLANES]
        h = cin
        for i in range(seg):
            h = sa_s[c, pl.ds(i, SUBLANES, stride=pitch), :] * h + sb_s[c, pl.ds(i, SUBLANES, stride=pitch), :]
            sh_s[c, pl.ds(i, SUBLANES, stride=pitch), :] = h

    def a_gated_out(q):
        hs = jnp.concatenate(
            [jnp.concatenate([sh_s[c, pl.ds(j * pitch, seg), :] for j in range(SUBLANES)], axis=0)
             for c in range(q * slabs_per_chunk, (q + 1) * slabs_per_chunk)], axis=1)
        hsg_s[:, _cols(q)] = (hs * _silu_from_half(ga_s[q])).astype(BF16)

    def a_out_proj(q):
        y0 = jnp.dot(hsg_s[...], awout_ref[:, _cols(q)], preferred_element_type=F32)
        h1_out[:, _cols(q)] = x_ref[0, :, _cols(q)] + y0

    def b_norm():
        ub_s[...] = _rmsnorm(h1_in[...], bn_ref[...]).astype(BF16)

    def b_xb_chunk(q):
        pq = jnp.dot(ub_s[...], bwin_ref[:, _cols(q)], preferred_element_type=F32)
        for i in range(slabs_per_chunk):
            px_s[q * slabs_per_chunk + i, POOL_PAD:POOL_PAD + tb, :] = pq[:, i * LANES:(i + 1) * LANES]

    def pool_slab(c, w):
        levels = w.bit_length() - 1
        tmp = c % slabs_per_chunk
        src = px_s.at[c]
        s = None
        for k in range(levels):
            back = (levels - 1 - k) * SUBLANES
            rows = tb + back
            s = (src[pl.ds(POOL_PAD - back, rows), :]
                 + src[pl.ds(POOL_PAD - back - (1 << k), rows), :])
            if k < levels - 1:
                pt_s[k % 2, tmp, POOL_PAD - back:POOL_PAD + tb, :] = s
                src = pt_s.at[k % 2, tmp]
        inv_cnt = 1.0 / jnp.minimum(pos16 + 1, w).astype(F32)
        mean = jnp.concatenate([s[0:2 * SUBLANES] * inv_cnt, s[2 * SUBLANES:] * (1.0 / w)], axis=0)
        return mean - px_s[c, POOL_PAD:POOL_PAD + tb, :]

    def b_pool(q):
        pooled_s[:, _cols(q)] = jnp.concatenate(
            [pool_slab(q * slabs_per_chunk + i, POOL_WINDOWS[q]) for i in range(slabs_per_chunk)],
            axis=1).astype(BF16)

    def b_group_proj(q):
        z_s[:, _cols(q)] = jnp.dot(pooled_s[:, _cols(q)], wgrp_ref[q], preferred_element_type=F32)

    def b_gate_chunk(q):
        gq_s[q] = jnp.dot(ub_s[...], bwin_ref[:, _cols(n_chunks + q)], preferred_element_type=F32)

    def b_gated_proj(q):
        z1 = (z_s[:, _cols(q)] + bgrp_ref[:, _cols(q)]) * bscale_ref[:, _cols(q)]
        zg_s[:, _cols(q)] = (z1 * _silu_from_half(gq_s[q])).astype(BF16)

    def b_out_proj(q):
        y1 = jnp.dot(zg_s[...], bwout_ref[:, _cols(q)], preferred_element_type=F32)
        y_ref[0, :, _cols(q)] = h1_in[:, _cols(q)] + y1

    def b_final_norm():
        y_ref[0] = _rmsnorm(y_ref[0], fn_ref[...])

    stages["aN"] = a_norm
    stages["bN"] = b_norm
    stages["bF"] = b_final_norm
    for q in range(n_chunks):
        for name, fn in (("aX", a_xb_chunk), ("aG", a_gate_chunk), ("aH", a_gated_out),
                         ("aO", a_out_proj), ("bX", b_xb_chunk), ("bP", b_pool),
                         ("bM", b_group_proj), ("bG", b_gate_chunk), ("bZ", b_gated_proj),
                         ("bO", b_out_proj)):
            stages[f"{name}{q}"] = functools.partial(fn, q)
    for c in range(n_slabs):
        for name, fn in (("aC", a_conv), ("aR", a_gate_proj), ("aK", a_coeffs), ("aS", a_scan)):
            stages[f"{name}{c}"] = functools.partial(fn, c)

    order = _prompt_stage_order(n_chunks, slabs_per_chunk)
    assert sorted(order) == sorted(stages), "every stage is issued exactly once"
    at = order.index
    for c in range(n_slabs):
        assert c < conv_ahead or at(f"aK{c - conv_ahead}") < at(f"aC{c}")
        assert c < zr_s.shape[0] or at(f"aK{c - zr_s.shape[0]}") < at(f"aR{c}")
    for name in order:
        stages[name]()
    h_s[...] = jnp.concatenate(carries, axis=1)

    @pl.when(jnp.logical_and(ta == blocks_per_seq - 1, n < n_blocks))
    def _():
        for c in range(n_slabs):
            conv_ref[0, :, c * LANES:(c + 1) * LANES] = ext_s[c, tb + CONV_PAD - (CONV_W - 1):tb + CONV_PAD, :]
        lru_ref[0] = h_s[...]

    @pl.when(jnp.logical_and(tbk == blocks_per_seq - 1, n > 0))
    def _():
        for c in range(n_slabs):
            pool_ref[0, :, c * LANES:(c + 1) * LANES] = px_s[c, tb + POOL_PAD - POOL_PAST:tb + POOL_PAD, :]

    ext_s[:, 0:CONV_PAD, :] = ext_s[:, tb:tb + CONV_PAD, :]
    px_s[:, 0:POOL_PAD, :] = px_s[:, tb:tb + POOL_PAD, :]


def _cast_rows(dst, src, scale=None, rows_per_step=128):
    for r in range(0, src.shape[0], rows_per_step):
        v = src[r:r + rows_per_step, :]
        dst[r:r + rows_per_step, :] = (v if scale is None else v * scale).astype(dst.dtype)


def _sample_kernel(x_hbm, sconv_hbm, slru_hbm, spool_hbm,
                   awin_f, wr_f, wi_f, awout_f, bwin_f, wgrp_f, bwout_f,
                   an_ref, cw_ref, cb_ref, br_ref, bi_ref, lam_ref, bn_ref, bgrp_ref, bscale_ref, fn_ref,
                   y_hbm, conv_hbm, lru_hbm, pool_hbm,
                   awin_o, wri_o, awout_o, bwin_o, wgrp_o, bwout_o,
                   x_v, sconv_v, slru_v, spool_v, y_v, conv_v, lru_v, xb1_v,
                   awin_v, wri_v, awout_v, bwin_v, wgrp_v, bwout_v,
                   stage_in, stage_out, stage_r, stage_i, stage_g, load_sem, store_sem):
    ts, nb, d = x_v.shape
    lb = d // N_LRU_BLOCKS
    gw = d // len(POOL_WINDOWS)
    kept = POOL_PAST - ts

    sources = (("x", x_hbm, x_v), ("awin", awin_f.at[0], stage_in), ("sconv", sconv_hbm, sconv_v),
               ("wr", wr_f.at[0], stage_r), ("wi", wi_f.at[0], stage_i), ("slru", slru_hbm, slru_v),
               ("awout", awout_f.at[0], stage_out), ("spool", spool_hbm, spool_v),
               ("wgrp", wgrp_f.at[0], stage_g),
               ("bwin", bwin_f.at[0], stage_in), ("bwout", bwout_f.at[0], stage_out))
    loads = {name: pltpu.make_async_copy(src, dst, load_sem.at[i])
             for i, (name, src, dst) in enumerate(sources)}
    for name in ("x", "awin", "sconv", "wr", "wi", "slru", "awout", "spool", "wgrp"):
        loads[name].start()
    sinks = (("conv", conv_v, conv_hbm), ("lru", lru_v, lru_hbm),
             ("pool_kept", spool_v.at[pl.ds(ts, kept)], pool_hbm.at[pl.ds(0, kept)]),
             ("pool_new", xb1_v, pool_hbm.at[pl.ds(kept, ts)]), ("y", y_v, y_hbm),
             ("awin", awin_v, awin_o), ("wri", wri_v, wri_o), ("awout", awout_v, awout_o),
             ("bwin", bwin_v, bwin_o), ("wgrp", wgrp_v, wgrp_o), ("bwout", bwout_v, bwout_o))
    stores = {name: pltpu.make_async_copy(src, dst, store_sem.at[i])
              for i, (name, src, dst) in enumerate(sinks)}
    gate_half = jnp.where(lax.broadcasted_iota(jnp.int32, (1, 2 * d), 1) < d, 1.0, 0.5)

    def slab(v, i):
        return v[i * nb:(i + 1) * nb]

    loads["x"].wait()
    x = x_v[...].reshape(ts * nb, d)
    u = _rmsnorm(x, an_ref[...]).astype(BF16)
    loads["awin"].wait()
    _cast_rows(awin_v, stage_in, gate_half)
    loads["bwin"].start()
    stores["awin"].start()
    proj = jnp.dot(u, awin_v[...], preferred_element_type=F32)
    xb = proj[:, :d]
    for name in ("sconv", "wr", "wi", "slru", "awout"):
        loads[name].wait()
    for n in range(N_LRU_BLOCKS):
        wri_v[n, :, :lb] = stage_r[n].astype(BF16)
        wri_v[n, :, lb:] = stage_i[n].astype(BF16)
    _cast_rows(awout_v, stage_out)
    loads["bwout"].start()
    for name in ("wri", "awout"):
        stores[name].start()
    ext = [sconv_v[k] for k in range(CONV_W - 1)] + [slab(xb, i) for i in range(ts)]
    hcw, hcb = 0.5 * cw_ref[...], 0.5 * cb_ref[...]
    hx_slabs = []
    for i in range(ts):
        acc = hcb
        for k in range(CONV_W):
            acc = acc + ext[i + k] * hcw[k:k + 1, :]
        hx_slabs.append(acc)
    hx = jnp.concatenate(hx_slabs, axis=0)
    hx_bf16 = hx.astype(BF16)
    zr, zi = [], []
    for n in range(N_LRU_BLOCKS):
        z = jnp.dot(hx_bf16[:, n * lb:(n + 1) * lb], wri_v[n], preferred_element_type=F32)
        zr.append(z[:, :lb])
        zi.append(z[:, lb:])
    a, mult, gated = _lru_terms(hx, jnp.concatenate(zr, axis=1), jnp.concatenate(zi, axis=1),
                                0.5 * br_ref[...], 0.5 * bi_ref[...],
                                _half_neg_c_softplus_neg(lam_ref[...]))
    h = slru_v[...]
    hs_slabs = []
    for i in range(ts):
        m = slab(mult, i)
        if PAST_LEN + i == 0:
            m = jnp.ones_like(m)
        h = slab(a, i) * h + m * slab(gated, i)
        hs_slabs.append(h)
    hs = jnp.concatenate(hs_slabs, axis=0)
    y0 = jnp.dot((hs * _silu_from_half(proj[:, d:])).astype(BF16), awout_v[...],
                 preferred_element_type=F32)
    h1 = x + y0
    lru_v[...] = h
    for k in range(CONV_W - 1):
        conv_v[k] = ext[ts + k]
    for name in ("bwin", "spool", "wgrp", "bwout"):
        loads[name].wait()
    _cast_rows(bwin_v, stage_in, gate_half)
    for g in range(len(POOL_WINDOWS)):
        wgrp_v[g] = stage_g[g].astype(BF16)
    _cast_rows(bwout_v, stage_out)
    for name in ("conv", "lru", "pool_kept", "bwin", "wgrp", "bwout"):
        stores[name].start()

    u1 = _rmsnorm(h1, bn_ref[...]).astype(BF16)
    proj1 = jnp.dot(u1, bwin_v[...], preferred_element_type=F32)
    xb1 = proj1[:, :d]
    ext2 = [spool_v[k] for k in range(POOL_PAST)] + [slab(xb1, i) for i in range(ts)]
    pooled_slabs = []
    for i in range(ts):
        means = []
        for gi, w in enumerate(POOL_WINDOWS):
            lanes = slice(gi * gw, (gi + 1) * gw)
            s = ext2[POOL_PAST + i][:, lanes]
            for j in range(1, w):
                s = s + ext2[POOL_PAST + i - j][:, lanes]
            means.append(s * (1.0 / min(PAST_LEN + i + 1, w)))
        pooled_slabs.append(jnp.concatenate(means, axis=1) - ext2[POOL_PAST + i])
    pooled = jnp.concatenate(pooled_slabs, axis=0).astype(BF16)
    z1 = jnp.concatenate(
        [jnp.dot(pooled[:, g * gw:(g + 1) * gw], wgrp_v[g], preferred_element_type=F32)
         for g in range(len(POOL_WINDOWS))], axis=1)
    z1 = (z1 + bgrp_ref[...]) * bscale_ref[...]
    y1 = jnp.dot((z1 * _silu_from_half(proj1[:, d:])).astype(BF16), bwout_v[...],
                 preferred_element_type=F32)
    h2 = h1 + y1
    xb1_v[...] = xb1.reshape(ts, nb, d)
    y_v[...] = _rmsnorm(h2, fn_ref[...]).reshape(ts, nb, d)
    for name in ("pool_new", "y"):
        stores[name].start()
    for store in stores.values():
        store.wait()


def _whole(shape):
    return pl.BlockSpec(shape, lambda *_: (0,) * len(shape), pipeline_mode=pl.Buffered(1))


def kernel(x_prompt, x_sample, state_conv, state_lru, state_pool, a_norm, a_w_in, a_conv_w, a_conv_b, a_w_r, a_b_r, a_w_i, a_b_i, a_lam, a_w_out, b_norm, b_w_in, b_w_grp, b_b_grp, b_scale, b_w_out, final_norm):
    bp, tp, d = x_prompt.shape
    bs, ts, _ = x_sample.shape
    tb = PROMPT_BLOCK_ROWS
    assert a_norm.shape[0] == 1 and b_norm.shape[0] == 1, "one layer of each mixer type"
    assert d % MXU_COLS == 0 and d // N_LRU_BLOCKS == LANES
    assert tp % tb == 0 and tb % (SUBLANES * SUBLANES) == 0 and tb >= 2 * SUBLANES
    assert CONV_W - 1 <= ts <= POOL_PAST

    row = lambda v: v.reshape(1, d)
    n_slabs = d // LANES
    n_chunks = d // MXU_COLS
    lb = d // N_LRU_BLOCKS
    gw = d // len(POOL_WINDOWS)

    xs = jnp.swapaxes(x_sample, 0, 1)
    sconv = jnp.swapaxes(state_conv[0], 0, 1)
    spool = jnp.swapaxes(state_pool[0], 0, 1)
    hbm = pl.BlockSpec(memory_space=pl.ANY)
    act = lambda shape: pltpu.VMEM(shape, F32)
    matrices = (a_w_in, a_w_r, a_w_i, a_w_out, b_w_in, b_w_grp, b_w_out)
    vectors = (row(a_norm[0]), a_conv_w[0], row(a_conv_b[0]), row(a_b_r[0]), row(a_b_i[0]),
               row(a_lam[0]), row(b_norm[0]), row(b_b_grp[0]), row(b_scale[0]), row(final_norm))
    bf16_shapes = ((d, 2 * d), (N_LRU_BLOCKS, lb, 2 * lb), (d, d), (d, 2 * d),
                   (len(POOL_WINDOWS), gw, gw), (d, d))
    (y_s, conv_s, lru_s, pool_s,
     awin, wri, awout, bwin, wgrp, bwout) = pl.pallas_call(
        _sample_kernel,
        in_specs=[hbm] * (4 + len(matrices)) + [_whole(v.shape) for v in vectors],
        out_specs=[hbm] * (4 + len(bf16_shapes)),
        out_shape=[jax.ShapeDtypeStruct(xs.shape, F32),
                   jax.ShapeDtypeStruct(sconv.shape, F32),
                   jax.ShapeDtypeStruct((bs, d), F32),
                   jax.ShapeDtypeStruct(spool.shape, F32)]
        + [jax.ShapeDtypeStruct(shape, BF16) for shape in bf16_shapes],
        scratch_shapes=[act(xs.shape), act(sconv.shape), act((bs, d)), act(spool.shape),
                        act(xs.shape), act(sconv.shape), act((bs, d)), act(xs.shape)]
        + [pltpu.VMEM(shape, BF16) for shape in bf16_shapes]
        + [act((d, 2 * d)), act((d, d)), act((N_LRU_BLOCKS, lb, lb)), act((N_LRU_BLOCKS, lb, lb)),
           act((len(POOL_WINDOWS), gw, gw)),
           pltpu.SemaphoreType.DMA((11,)), pltpu.SemaphoreType.DMA((11,))],
        compiler_params=pltpu.CompilerParams(vmem_limit_bytes=VMEM_LIMIT_BYTES),
        name="sample_step",
    )(xs, sconv, state_lru[0], spool, *matrices, *vectors)

    weights = (row(a_norm[0]), awin, a_conv_w[0], row(a_conv_b[0]), wri, row(a_b_r[0]),
               row(a_b_i[0]), row(a_lam[0]), awout, row(b_norm[0]), bwin, wgrp, row(b_b_grp[0]),
               row(b_scale[0]), bwout, row(final_norm))
    weight_specs = [_whole(w.shape) for w in weights]
    seg_rows = SUBLANES * (tb // SUBLANES + 4)
    nt = tp // tb
    n_blocks = bp * nt
    blk0 = lambda n: jnp.minimum(n, n_blocks - 1)
    blk1 = lambda n: jnp.maximum(n - 1, 0)
    y_p, conv_p, lru_p, pool_p = pl.pallas_call(
        functools.partial(_prompt_kernel, blocks_per_seq=nt),
        grid=(n_blocks + 1,),
        in_specs=[pl.BlockSpec((1, tb, d), lambda n: (blk0(n) // nt, blk0(n) % nt, 0))] + weight_specs,
        out_specs=[pl.BlockSpec((1, tb, d), lambda n: (blk1(n) // nt, blk1(n) % nt, 0)),
                   pl.BlockSpec((1, CONV_W - 1, d), lambda n: (blk0(n) // nt, 0, 0)),
                   pl.BlockSpec((1, 1, d), lambda n: (blk0(n) // nt, 0, 0)),
                   pl.BlockSpec((1, POOL_PAST, d), lambda n: (blk1(n) // nt, 0, 0))],
        out_shape=[jax.ShapeDtypeStruct((bp, tp, d), F32),
                   jax.ShapeDtypeStruct((bp, CONV_W - 1, d), F32),
                   jax.ShapeDtypeStruct((bp, 1, d), F32),
                   jax.ShapeDtypeStruct((bp, POOL_PAST, d), F32)],
        scratch_shapes=[pltpu.VMEM((n_slabs, CONV_PAD + tb, LANES), F32),
                        pltpu.VMEM((n_slabs, seg_rows, LANES), F32),
                        pltpu.VMEM((n_slabs, seg_rows, LANES), F32),
                        pltpu.VMEM((n_slabs, seg_rows, LANES), F32),
                        pltpu.VMEM((1, d), F32),
                        pltpu.VMEM((2, tb, d), F32),
                        pltpu.VMEM((n_slabs, POOL_PAD + tb, LANES), F32),
                        pltpu.VMEM((2, MXU_COLS // LANES, POOL_PAD + tb, LANES), F32),
                        pltpu.VMEM((tb, d), BF16),
                        pltpu.VMEM((CONV_AHEAD, tb, LANES), F32),
                        pltpu.VMEM((CONV_AHEAD, tb, LANES), BF16),
                        pltpu.VMEM((2, tb, 2 * LANES), F32),
                        pltpu.VMEM((n_chunks, tb, MXU_COLS), F32),
                        pltpu.VMEM((tb, d), BF16),
                        pltpu.VMEM((tb, d), BF16),
                        pltpu.VMEM((tb, d), BF16),
                        pltpu.VMEM((tb, d), F32),
                        pltpu.VMEM((n_chunks, tb, MXU_COLS), F32),
                        pltpu.VMEM((tb, d), BF16)],
        compiler_params=pltpu.CompilerParams(
            vmem_limit_bytes=VMEM_LIMIT_BYTES,
            dimension_semantics=("arbitrary",)),
        name="prompt_step",
    )(x_prompt, *weights)

    return (y_p, jnp.swapaxes(y_s, 0, 1),
            conv_p[None], lru_p.reshape(1, bp, d), pool_p[None],
            jnp.swapaxes(conv_s, 0, 1)[None], lru_s[None], jnp.swapaxes(pool_s, 0, 1)[None])
```

```python
import functools

import jax
import jax.numpy as jnp
from jax import lax
from jax.experimental import pallas as pl
from jax.experimental.pallas import tpu as pltpu

PAST_LEN = 16384
N_LRU_BLOCKS = 8
CONV_W = 4
LRU_C = 8.0
POOL_WINDOWS = (2, 4, 8, 16)
POOL_PAST = max(POOL_WINDOWS) - 1
EPS = 1e-6

LANES = 128
SUBLANES = 8
MXU_COLS = 256
CONV_PAD = SUBLANES
POOL_LEVELS = max(POOL_WINDOWS).bit_length() - 1
POOL_PAD = POOL_LEVELS * SUBLANES
PROMPT_BLOCK_ROWS = 512
CONV_AHEAD = 4
VMEM_LIMIT_BYTES = 56 * 1024 * 1024

F32 = jnp.float32
BF16 = jnp.bfloat16


def _rmsnorm(x, g):
    return x * lax.rsqrt(jnp.mean(x * x, axis=-1, keepdims=True) + EPS) * g


def _silu_from_half(hg):
    return hg + hg * jnp.tanh(hg)


def _half_neg_c_softplus_neg(lam):
    z = -lam
    return (-0.5 * LRU_C) * (jnp.maximum(z, 0.0) + jnp.log1p(jnp.exp(-jnp.abs(z))))


def _lru_terms(hx, zr, zi, hbr, hbi, hn):
    log_a = hn + hn * jnp.tanh(zr + hbr)
    a = jnp.exp(log_a)
    y = jnp.tanh(log_a) * (-1.0 - a * a)
    mult = jnp.where(y > 0.0, y * lax.rsqrt(y), 0.0)
    gated = hx + hx * jnp.tanh(zi + hbi)
    return a, mult, gated


def _lane_slab(ref, c):
    return ref[:, c * LANES:(c + 1) * LANES]


def _cols(q):
    return slice(q * MXU_COLS, (q + 1) * MXU_COLS)


def _prompt_stage_order(n_chunks, slabs_per_chunk):
    assert (n_chunks, slabs_per_chunk) == (4, 2)
    return ("aN aX0 bN aX1 aC0 bX0 aC1 aC2 aC3 aR0 aR1 aK0 aX2 aS0 aK1 bX1 aS1 bP0 aR2 aC4 aR3 bM0 aC5 aX3 "
            "aK2 aS2 bX2 aK3 aS3 aR4 bP1 aR5 bM1 aC6 aG0 aC7 aK4 bX3 aS4 aK5 aR6 aR7 aS5 bP2 bM2 aK6 aG1 aS6 "
            "aK7 bG0 aS7 bP3 bM3 aH0 aG2 aH1 bZ0 bG1 aH2 aG3 bZ1 bG2 aH3 bG3 bZ2 bZ3 bO0 bO1 bO2 bO3 aO0 bF "
            "aO1 aO2 aO3").split()


def _prompt_kernel(x_ref, an_ref, awin_ref, cw_ref, cb_ref, wri_ref, br_ref, bi_ref, lam_ref,
                   awout_ref, bn_ref, bwin_ref, wgrp_ref, bgrp_ref, bscale_ref, bwout_ref, fn_ref,
                   y_ref, conv_ref, lru_ref, pool_ref,
                   ext_s, sa_s, sb_s, sh_s, h_s, h1_s, px_s, pt_s,
                   ua_s, hx_s, hxb_s, zr_s, ga_s, hsg_s, ub_s, pooled_s, z_s, gq_s, zg_s, *,
                   blocks_per_seq):
    n = pl.program_id(0)
    n_blocks = pl.num_programs(0) - 1
    ta = jnp.minimum(n, n_blocks - 1) % blocks_per_seq
    tbk = jnp.maximum(n - 1, 0) % blocks_per_seq
    tb, d = x_ref.shape[1], x_ref.shape[2]
    n_slabs = d // LANES
    n_chunks = d // MXU_COLS
    slabs_per_chunk = MXU_COLS // LANES
    assert d // len(POOL_WINDOWS) == MXU_COLS, "one pooling group per matmul column chunk"
    seg = tb // SUBLANES
    pitch = seg + 4
    conv_ahead = hx_s.shape[0]

    @pl.when(n == 0)
    def _():
        h1_s[...] = jnp.zeros_like(h1_s)

    @pl.when(ta == 0)
    def _():
        ext_s[:, 0:CONV_PAD, :] = jnp.zeros((n_slabs, CONV_PAD, LANES), F32)
        h_s[...] = jnp.zeros_like(h_s)

    @pl.when(tbk == 0)
    def _():
        px_s[:, 0:POOL_PAD, :] = jnp.zeros((n_slabs, POOL_PAD, LANES), F32)

    h1_in = h1_s.at[(n + 1) % 2]
    h1_out = h1_s.at[n % 2]
    sub = lax.broadcasted_iota(jnp.int32, (SUBLANES, 1), 0)
    first_token = jnp.logical_and(ta == 0, sub == 0)
    row16 = lax.broadcasted_iota(jnp.int32, (2 * SUBLANES, 1), 0)
    pos16 = tbk * tb + row16
    hn_row = _half_neg_c_softplus_neg(lam_ref[...])
    hcw, hcb = 0.5 * cw_ref[...], 0.5 * cb_ref[...]
    hbr, hbi = 0.5 * br_ref[...], 0.5 * bi_ref[...]
    carries = [None] * n_slabs
    stages = {}

    def a_norm():
        ua_s[...] = _rmsnorm(x_ref[0], an_ref[...]).astype(BF16)

    def a_xb_chunk(q):
        pq = jnp.dot(ua_s[...], awin_ref[:, _cols(q)], preferred_element_type=F32)
        for i in range(slabs_per_chunk):
            ext_s[q * slabs_per_chunk + i, CONV_PAD:CONV_PAD + tb, :] = pq[:, i * LANES:(i + 1) * LANES]

    def a_gate_chunk(q):
        ga_s[q] = jnp.dot(ua_s[...], awin_ref[:, _cols(n_chunks + q)], preferred_element_type=F32)

    def a_conv(c):
        lanes = slice(c * LANES, (c + 1) * LANES)
        for j in range(SUBLANES):
            r0 = CONV_PAD + j * seg
            hx = ext_s[c, r0:r0 + seg, :] * hcw[CONV_W - 1:CONV_W, lanes] + hcb[:, lanes]
            for k in range(CONV_W - 1):
                hx = hx + ext_s[c, pl.ds(r0 - (CONV_W - 1) + k, seg), :] * hcw[k:k + 1, lanes]
            hx_s[c % conv_ahead, j * seg:(j + 1) * seg, :] = hx
            hxb_s[c % conv_ahead, j * seg:(j + 1) * seg, :] = hx.astype(BF16)

    def a_gate_proj(c):
        zr_s[c % 2] = jnp.dot(hxb_s[c % conv_ahead], wri_ref[c], preferred_element_type=F32)

    def a_coeffs(c):
        lanes = slice(c * LANES, (c + 1) * LANES)
        for j in range(SUBLANES):
            rows = slice(j * seg, (j + 1) * seg)
            a, mult, gated = _lru_terms(hx_s[c % conv_ahead, rows, :], zr_s[c % 2, rows, :LANES],
                                        zr_s[c % 2, rows, LANES:], hbr[:, lanes], hbi[:, lanes],
                                        hn_row[:, lanes])
            if j == 0:
                mult = jnp.concatenate(
                    [jnp.where(first_token, 1.0, mult[0:SUBLANES]), mult[SUBLANES:]], axis=0)
            sa_s[c, pl.ds(j * pitch, seg), :] = a
            sb_s[c, pl.ds(j * pitch, seg), :] = mult * gated

    def a_scan(c):
        hl = sb_s[c, pl.ds(0, SUBLANES, stride=pitch), :]
        pp = sa_s[c, pl.ds(0, SUBLANES, stride=pitch), :]
        for i in range(1, seg):
            av = sa_s[c, pl.ds(i, SUBLANES, stride=pitch), :]
            hl = av * hl + sb_s[c, pl.ds(i, SUBLANES, stride=pitch), :]
            pp = av * pp
        h_in = jnp.broadcast_to(_lane_slab(h_s, c), (SUBLANES, LANES))
        cin = h_in
        for _ in range(SUBLANES - 1):
            cin = jnp.where(sub == 0, h_in, pltpu.roll(hl + pp * cin, 1, 0))
        carries[c] = (hl + pp * cin)[SUBLANES - 1:SUBLANES]
        h = cin
        for i in range(seg):
            h = sa_s[c, pl.ds(i, SUBLANES, stride=pitch), :] * h + sb_s[c, pl.ds(i, SUBLANES, stride=pitch), :]
            sh_s[c, pl.ds(i, SUBLANES, stride=pitch), :] = h

    def a_gated_out(q):
        hs = jnp.concatenate(
            [jnp.concatenate([sh_s[c, pl.ds(j * pitch, seg), :] for j in range(SUBLANES)], axis=0)
             for c in range(q * slabs_per_chunk, (q + 1) * slabs_per_chunk)], axis=1)
        hsg_s[:, _cols(q)] = (hs * _silu_from_half(ga_s[q])).astype(BF16)

    def a_out_proj(q):
        y0 = jnp.dot(hsg_s[...], awout_ref[:, _cols(q)], preferred_element_type=F32)
        h1_out[:, _cols(q)] = x_ref[0, :, _cols(q)] + y0

    def b_norm():
        ub_s[...] = _rmsnorm(h1_in[...], bn_ref[...]).astype(BF16)

    def b_xb_chunk(q):
        pq = jnp.dot(ub_s[...], bwin_ref[:, _cols(q)], preferred_element_type=F32)
        for i in range(slabs_per_chunk):
            px_s[q * slabs_per_chunk + i, POOL_PAD:POOL_PAD + tb, :] = pq[:, i * LANES:(i + 1) * LANES]

    def pool_slab(c, w):
        levels = w.bit_length() - 1
        tmp = c % slabs_per_chunk
        src = px_s.at[c]
        s = None
        for k in range(levels):
            back = (levels - 1 - k) * SUBLANES
            rows = tb + back
            s = (src[pl.ds(POOL_PAD - back, rows), :]
                 + src[pl.ds(POOL_PAD - back - (1 << k), rows), :])
            if k < levels - 1:
                pt_s[k % 2, tmp, POOL_PAD - back:POOL_PAD + tb, :] = s
                src = pt_s.at[k % 2, tmp]
        inv_cnt = 1.0 / jnp.minimum(pos16 + 1, w).astype(F32)
        mean = jnp.concatenate([s[0:2 * SUBLANES] * inv_cnt, s[2 * SUBLANES:] * (1.0 / w)], axis=0)
        return mean - px_s[c, POOL_PAD:POOL_PAD + tb, :]

    def b_pool(q):
        pooled_s[:, _cols(q)] = jnp.concatenate(
            [pool_slab(q * slabs_per_chunk + i, POOL_WINDOWS[q]) for i in range(slabs_per_chunk)],
            axis=1).astype(BF16)

    def b_group_proj(q):
        z_s[:, _cols(q)] = jnp.dot(pooled_s[:, _cols(q)], wgrp_ref[q], preferred_element_type=F32)

    def b_gate_chunk(q):
        gq_s[q] = jnp.dot(ub_s[...], bwin_ref[:, _cols(n_chunks + q)], preferred_element_type=F32)

    def b_gated_proj(q):
        z1 = (z_s[:, _cols(q)] + bgrp_ref[:, _cols(q)]) * bscale_ref[:, _cols(q)]
        zg_s[:, _cols(q)] = (z1 * _silu_from_half(gq_s[q])).astype(BF16)

    def b_out_proj(q):
        y1 = jnp.dot(zg_s[...], bwout_ref[:, _cols(q)], preferred_element_type=F32)
        y_ref[0, :, _cols(q)] = h1_in[:, _cols(q)] + y1

    def b_final_norm():
        y_ref[0] = _rmsnorm(y_ref[0], fn_ref[...])

    stages["aN"] = a_norm
    stages["bN"] = b_norm
    stages["bF"] = b_final_norm
    for q in range(n_chunks):
        for name, fn in (("aX", a_xb_chunk), ("aG", a_gate_chunk), ("aH", a_gated_out),
                         ("aO", a_out_proj), ("bX", b_xb_chunk), ("bP", b_pool),
                         ("bM", b_group_proj), ("bG", b_gate_chunk), ("bZ", b_gated_proj),
                         ("bO", b_out_proj)):
            stages[f"{name}{q}"] = functools.partial(fn, q)
    for c in range(n_slabs):
        for name, fn in (("aC", a_conv), ("aR", a_gate_proj), ("aK", a_coeffs), ("aS", a_scan)):
            stages[f"{name}{c}"] = functools.partial(fn, c)

    order = _prompt_stage_order(n_chunks, slabs_per_chunk)
    assert sorted(order) == sorted(stages), "every stage is issued exactly once"
    at = order.index
    for c in range(n_slabs):
        assert c < conv_ahead or at(f"aK{c - conv_ahead}") < at(f"aC{c}")
        assert c < zr_s.shape[0] or at(f"aK{c - zr_s.shape[0]}") < at(f"aR{c}")
    for name in order:
        stages[name]()
    h_s[...] = jnp.concatenate(carries, axis=1)

    @pl.when(jnp.logical_and(ta == blocks_per_seq - 1, n < n_blocks))
    def _():
        for c in range(n_slabs):
            conv_ref[0, :, c * LANES:(c + 1) * LANES] = ext_s[c, tb + CONV_PAD - (CONV_W - 1):tb + CONV_PAD, :]
        lru_ref[0] = h_s[...]

    @pl.when(jnp.logical_and(tbk == blocks_per_seq - 1, n > 0))
    def _():
        for c in range(n_slabs):
            pool_ref[0, :, c * LANES:(c + 1) * LANES] = px_s[c, tb + POOL_PAD - POOL_PAST:tb + POOL_PAD, :]

    ext_s[:, 0:CONV_PAD, :] = ext_s[:, tb:tb + CONV_PAD, :]
    px_s[:, 0:POOL_PAD, :] = px_s[:, tb:tb + POOL_PAD, :]


def _cast_rows(dst, src, scale=None, rows_per_step=128):
    for r in range(0, src.shape[0], rows_per_step):
        v = src[r:r + rows_per_step, :]
        dst[r:r + rows_per_step, :] = (v if scale is None else v * scale).astype(dst.dtype)


def _sample_kernel(x_hbm, sconv_hbm, slru_hbm, spool_hbm,
                   awin_f, wr_f, wi_f, awout_f, bwin_f, wgrp_f, bwout_f,
                   an_ref, cw_ref, cb_ref, br_ref, bi_ref, lam_ref, bn_ref, bgrp_ref, bscale_ref, fn_ref,
                   y_hbm, conv_hbm, lru_hbm, pool_hbm,
                   awin_o, wri_o, awout_o, bwin_o, wgrp_o, bwout_o,
                   x_v, sconv_v, slru_v, pool_n, y_v, conv_v, lru_v,
                   awin_v, wri_v, awout_v, bwin_v, wgrp_v, bwout_v,
                   stage_in, stage_r, stage_i, stage_g, load_sem, store_sem):
    ts, nb, d = x_v.shape
    lb = d // N_LRU_BLOCKS
    gw = d // len(POOL_WINDOWS)
    early = (POOL_PAST - ts) // SUBLANES * SUBLANES
    stage_out = stage_in.at[:, pl.ds(0, d)]

    sources = (("x", x_hbm, x_v), ("awin", awin_f.at[0], stage_in), ("sconv", sconv_hbm, sconv_v),
               ("wr", wr_f.at[0], stage_r), ("wi", wi_f.at[0], stage_i), ("slru", slru_hbm, slru_v),
               ("awout", awout_f.at[0], stage_out), ("spool", spool_hbm.at[0], pool_n),
               ("wgrp", wgrp_f.at[0], stage_g),
               ("bwin", bwin_f.at[0], stage_in), ("bwout", bwout_f.at[0], stage_out))
    loads = {name: pltpu.make_async_copy(src, dst, load_sem.at[i])
             for i, (name, src, dst) in enumerate(sources)}
    for name in ("x", "awin", "sconv", "wr", "wi", "slru", "spool", "wgrp"):
        loads[name].start()
    sinks = (("conv", conv_v, conv_hbm), ("lru", lru_v, lru_hbm),
             ("pool_early", pool_n.at[:, pl.ds(0, early), :], pool_hbm.at[0, :, pl.ds(0, early), :]),
             ("pool_late", pool_n.at[:, pl.ds(early, POOL_PAST - early), :],
              pool_hbm.at[0, :, pl.ds(early, POOL_PAST - early), :]), ("y", y_v, y_hbm),
             ("awin", awin_v, awin_o), ("wri", wri_v, wri_o), ("awout", awout_v, awout_o),
             ("bwin", bwin_v, bwin_o), ("wgrp", wgrp_v, wgrp_o), ("bwout", bwout_v, bwout_o))
    stores = {name: pltpu.make_async_copy(src, dst, store_sem.at[i])
              for i, (name, src, dst) in enumerate(sinks)}
    gate_half = jnp.where(lax.broadcasted_iota(jnp.int32, (1, 2 * d), 1) < d, 1.0, 0.5)

    def slab(v, i):
        return v[i * nb:(i + 1) * nb]

    loads["x"].wait()
    x = x_v[...].reshape(ts * nb, d)
    u = _rmsnorm(x, an_ref[...]).astype(BF16)
    loads["awin"].wait()
    _cast_rows(awin_v, stage_in, gate_half)
    loads["awout"].start()
    stores["awin"].start()
    proj = jnp.dot(u, awin_v[...], preferred_element_type=F32)
    xb = proj[:, :d]
    for name in ("sconv", "wr", "wi", "slru", "awout"):
        loads[name].wait()
    for n in range(N_LRU_BLOCKS):
        wri_v[n, :, :lb] = stage_r[n].astype(BF16)
        wri_v[n, :, lb:] = stage_i[n].astype(BF16)
    _cast_rows(awout_v, stage_out)
    loads["bwin"].start()
    for name in ("wri", "awout"):
        stores[name].start()
    ext = [sconv_v[k] for k in range(CONV_W - 1)] + [slab(xb, i) for i in range(ts)]
    hcw, hcb = 0.5 * cw_ref[...], 0.5 * cb_ref[...]
    hx_slabs = []
    for i in range(ts):
        acc = hcb
        for k in range(CONV_W):
            acc = acc + ext[i + k] * hcw[k:k + 1, :]
        hx_slabs.append(acc)
    hx = jnp.concatenate(hx_slabs, axis=0)
    hx_bf16 = hx.astype(BF16)
    zr, zi = [], []
    for n in range(N_LRU_BLOCKS):
        z = jnp.dot(hx_bf16[:, n * lb:(n + 1) * lb], wri_v[n], preferred_element_type=F32)
        zr.append(z[:, :lb])
        zi.append(z[:, lb:])
    a, mult, gated = _lru_terms(hx, jnp.concatenate(zr, axis=1), jnp.concatenate(zi, axis=1),
                                0.5 * br_ref[...], 0.5 * bi_ref[...],
                                _half_neg_c_softplus_neg(lam_ref[...]))
    h = slru_v[...]
    hs_slabs = []
    for i in range(ts):
        m = slab(mult, i)
        if PAST_LEN + i == 0:
            m = jnp.ones_like(m)
        h = slab(a, i) * h + m * slab(gated, i)
        hs_slabs.append(h)
    hs = jnp.concatenate(hs_slabs, axis=0)
    y0 = jnp.dot((hs * _silu_from_half(proj[:, d:])).astype(BF16), awout_v[...],
                 preferred_element_type=F32)
    h1 = x + y0
    lru_v[...] = h
    for k in range(CONV_W - 1):
        conv_v[k] = ext[ts + k]
    for name in ("bwin", "spool", "wgrp"):
        loads[name].wait()
    _cast_rows(bwin_v, stage_in, gate_half)
    loads["bwout"].start()
    for g in range(len(POOL_WINDOWS)):
        wgrp_v[g] = stage_g[g].astype(BF16)
    old_rows = [pool_n[:, k, :] for k in range(POOL_PAST)]
    for k in range(early):
        pool_n[:, k, :] = old_rows[ts + k]
    for name in ("conv", "lru", "pool_early", "bwin", "wgrp"):
        stores[name].start()

    u1 = _rmsnorm(h1, bn_ref[...]).astype(BF16)
    proj1 = jnp.dot(u1, bwin_v[...], preferred_element_type=F32)
    xb1 = proj1[:, :d]
    ext2 = old_rows + [slab(xb1, i) for i in range(ts)]
    pooled_slabs = []
    for i in range(ts):
        means = []
        for gi, w in enumerate(POOL_WINDOWS):
            lanes = slice(gi * gw, (gi + 1) * gw)
            s = ext2[POOL_PAST + i][:, lanes]
            for j in range(1, w):
                s = s + ext2[POOL_PAST + i - j][:, lanes]
            means.append(s * (1.0 / min(PAST_LEN + i + 1, w)))
        pooled_slabs.append(jnp.concatenate(means, axis=1) - ext2[POOL_PAST + i])
    pooled = jnp.concatenate(pooled_slabs, axis=0).astype(BF16)
    z1 = jnp.concatenate(
        [jnp.dot(pooled[:, g * gw:(g + 1) * gw], wgrp_v[g], preferred_element_type=F32)
         for g in range(len(POOL_WINDOWS))], axis=1)
    z1 = (z1 + bgrp_ref[...]) * bscale_ref[...]
    loads["bwout"].wait()
    _cast_rows(bwout_v, stage_out)
    stores["bwout"].start()
    y1 = jnp.dot((z1 * _silu_from_half(proj1[:, d:])).astype(BF16), bwout_v[...],
                 preferred_element_type=F32)
    h2 = h1 + y1
    for k in range(early, POOL_PAST):
        pool_n[:, k, :] = ext2[ts + k]
    y_v[...] = _rmsnorm(h2, fn_ref[...]).reshape(ts, nb, d)
    for name in ("pool_late", "y"):
        stores[name].start()
    for store in stores.values():
        store.wait()


def _whole(shape):
    return pl.BlockSpec(shape, lambda *_: (0,) * len(shape), pipeline_mode=pl.Buffered(1))


def kernel(x_prompt, x_sample, state_conv, state_lru, state_pool, a_norm, a_w_in, a_conv_w, a_conv_b, a_w_r, a_b_r, a_w_i, a_b_i, a_lam, a_w_out, b_norm, b_w_in, b_w_grp, b_b_grp, b_scale, b_w_out, final_norm):
    bp, tp, d = x_prompt.shape
    bs, ts, _ = x_sample.shape
    tb = PROMPT_BLOCK_ROWS
    assert a_norm.shape[0] == 1 and b_norm.shape[0] == 1, "one layer of each mixer type"
    assert d % MXU_COLS == 0 and d // N_LRU_BLOCKS == LANES
    assert tp % tb == 0 and tb % (SUBLANES * SUBLANES) == 0 and tb >= 2 * SUBLANES
    assert CONV_W - 1 <= ts <= POOL_PAST

    row = lambda v: v.reshape(1, d)
    n_slabs = d // LANES
    n_chunks = d // MXU_COLS
    lb = d // N_LRU_BLOCKS
    gw = d // len(POOL_WINDOWS)

    xs = jnp.swapaxes(x_sample, 0, 1)
    sconv = jnp.swapaxes(state_conv[0], 0, 1)
    pool_shape = state_pool.shape[1:]
    hbm = pl.BlockSpec(memory_space=pl.ANY)
    act = lambda shape: pltpu.VMEM(shape, F32)
    matrices = (a_w_in, a_w_r, a_w_i, a_w_out, b_w_in, b_w_grp, b_w_out)
    vectors = (row(a_norm[0]), a_conv_w[0], row(a_conv_b[0]), row(a_b_r[0]), row(a_b_i[0]),
               row(a_lam[0]), row(b_norm[0]), row(b_b_grp[0]), row(b_scale[0]), row(final_norm))
    bf16_shapes = ((d, 2 * d), (N_LRU_BLOCKS, lb, 2 * lb), (d, d), (d, 2 * d),
                   (len(POOL_WINDOWS), gw, gw), (d, d))
    (y_s, conv_s, lru_s, pool_s,
     awin, wri, awout, bwin, wgrp, bwout) = pl.pallas_call(
        _sample_kernel,
        in_specs=[hbm] * (4 + len(matrices)) + [_whole(v.shape) for v in vectors],
        out_specs=[hbm] * (4 + len(bf16_shapes)),
        out_shape=[jax.ShapeDtypeStruct(xs.shape, F32),
                   jax.ShapeDtypeStruct(sconv.shape, F32),
                   jax.ShapeDtypeStruct((bs, d), F32),
                   jax.ShapeDtypeStruct(state_pool.shape, F32)]
        + [jax.ShapeDtypeStruct(shape, BF16) for shape in bf16_shapes],
        scratch_shapes=[act(xs.shape), act(sconv.shape), act((bs, d)), act(pool_shape),
                        act(xs.shape), act(sconv.shape), act((bs, d))]
        + [pltpu.VMEM(shape, BF16) for shape in bf16_shapes]
        + [act((d, 2 * d)), act((N_LRU_BLOCKS, lb, lb)), act((N_LRU_BLOCKS, lb, lb)),
           act((len(POOL_WINDOWS), gw, gw)),
           pltpu.SemaphoreType.DMA((11,)), pltpu.SemaphoreType.DMA((11,))],
        compiler_params=pltpu.CompilerParams(vmem_limit_bytes=VMEM_LIMIT_BYTES),
        name="sample_step",
    )(xs, sconv, state_lru[0], state_pool, *matrices, *vectors)

    weights = (row(a_norm[0]), awin, a_conv_w[0], row(a_conv_b[0]), wri, row(a_b_r[0]),
               row(a_b_i[0]), row(a_lam[0]), awout, row(b_norm[0]), bwin, wgrp, row(b_b_grp[0]),
               row(b_scale[0]), bwout, row(final_norm))
    weight_specs = [_whole(w.shape) for w in weights]
    seg_rows = SUBLANES * (tb // SUBLANES + 4)
    nt = tp // tb
    n_blocks = bp * nt
    blk0 = lambda n: jnp.minimum(n, n_blocks - 1)
    blk1 = lambda n: jnp.maximum(n - 1, 0)
    y_p, conv_p, lru_p, pool_p = pl.pallas_call(
        functools.partial(_prompt_kernel, blocks_per_seq=nt),
        grid=(n_blocks + 1,),
        in_specs=[pl.BlockSpec((1, tb, d), lambda n: (blk0(n) // nt, blk0(n) % nt, 0))] + weight_specs,
        out_specs=[pl.BlockSpec((1, tb, d), lambda n: (blk1(n) // nt, blk1(n) % nt, 0)),
                   pl.BlockSpec((1, CONV_W - 1, d), lambda n: (blk0(n) // nt, 0, 0)),
                   pl.BlockSpec((1, 1, d), lambda n: (blk0(n) // nt, 0, 0)),
                   pl.BlockSpec((1, POOL_PAST, d), lambda n: (blk1(n) // nt, 0, 0))],
        out_shape=[jax.ShapeDtypeStruct((bp, tp, d), F32),
                   jax.ShapeDtypeStruct((bp, CONV_W - 1, d), F32),
                   jax.ShapeDtypeStruct((bp, 1, d), F32),
                   jax.ShapeDtypeStruct((bp, POOL_PAST, d), F32)],
        scratch_shapes=[pltpu.VMEM((n_slabs, CONV_PAD + tb, LANES), F32),
                        pltpu.VMEM((n_slabs, seg_rows, LANES), F32),
                        pltpu.VMEM((n_slabs, seg_rows, LANES), F32),
                        pltpu.VMEM((n_slabs, seg_rows, LANES), F32),
                        pltpu.VMEM((1, d), F32),
                        pltpu.VMEM((2, tb, d), F32),
                        pltpu.VMEM((n_slabs, POOL_PAD + tb, LANES), F32),
                        pltpu.VMEM((2, MXU_COLS // LANES, POOL_PAD + tb, LANES), F32),
                        pltpu.VMEM((tb, d), BF16),
                        pltpu.VMEM((CONV_AHEAD, tb, LANES), F32),
                        pltpu.VMEM((CONV_AHEAD, tb, LANES), BF16),
                        pltpu.VMEM((2, tb, 2 * LANES), F32),
                        pltpu.VMEM((n_chunks, tb, MXU_COLS), F32),
                        pltpu.VMEM((tb, d), BF16),
                        pltpu.VMEM((tb, d), BF16),
                        pltpu.VMEM((tb, d), BF16),
                        pltpu.VMEM((tb, d), F32),
                        pltpu.VMEM((n_chunks, tb, MXU_COLS), F32),
                        pltpu.VMEM((tb, d), BF16)],
        compiler_params=pltpu.CompilerParams(
            vmem_limit_bytes=VMEM_LIMIT_BYTES,
            dimension_semantics=("arbitrary",)),
        name="prompt_step",
    )(x_prompt, *weights)

    return (y_p, jnp.swapaxes(y_s, 0, 1),
            conv_p[None], lru_p.reshape(1, bp, d), pool_p[None],
            jnp.swapaxes(conv_s, 0, 1)[None], lru_s[None], pool_s)
```

```python
import functools

import jax
import jax.numpy as jnp
from jax import lax
from jax.experimental import pallas as pl
from jax.experimental.pallas import tpu as pltpu

PAST_LEN = 16384
N_LRU_BLOCKS = 8
CONV_W = 4
LRU_C = 8.0
POOL_WINDOWS = (2, 4, 8, 16)
POOL_PAST = max(POOL_WINDOWS) - 1
EPS = 1e-6

LANES = 128
SUBLANES = 8
MXU_COLS = 256
CONV_PAD = SUBLANES
POOL_LEVELS = max(POOL_WINDOWS).bit_length() - 1
POOL_PAD = POOL_LEVELS * SUBLANES
PROMPT_BLOCK_ROWS = 512
CONV_AHEAD = 4
VMEM_LIMIT_BYTES = 56 * 1024 * 1024

F32 = jnp.float32
BF16 = jnp.bfloat16


def _rmsnorm(x, g):
    return x * lax.rsqrt(jnp.mean(x * x, axis=-1, keepdims=True) + EPS) * g


def _silu_from_half(hg):
    return hg + hg * jnp.tanh(hg)


def _half_neg_c_softplus_neg(lam):
    z = -lam
    return (-0.5 * LRU_C) * (jnp.maximum(z, 0.0) + jnp.log1p(jnp.exp(-jnp.abs(z))))


def _lru_terms(hx, zr, zi, hbr, hbi, hn):
    log_a = hn + hn * jnp.tanh(zr + hbr)
    a = jnp.exp(log_a)
    y = jnp.tanh(log_a) * (-1.0 - a * a)
    mult = jnp.where(y > 0.0, y * lax.rsqrt(y), 0.0)
    gated = hx + hx * jnp.tanh(zi + hbi)
    return a, mult, gated


def _lane_slab(ref, c):
    return ref[:, c * LANES:(c + 1) * LANES]


def _cols(q):
    return slice(q * MXU_COLS, (q + 1) * MXU_COLS)


def _prompt_stage_order(n_chunks, slabs_per_chunk):
    assert (n_chunks, slabs_per_chunk) == (4, 2)
    return ("aN aX0 bN aX1 aC0 bX0 aC1 aC2 aC3 aR0 aR1 aK0 aX2 aS0 aK1 bX1 aS1 bP0 aR2 aC4 aR3 bM0 aC5 aX3 "
            "aK2 aS2 bX2 aK3 aS3 aR4 bP1 aR5 bM1 aC6 aG0 aC7 aK4 bX3 aS4 aK5 aR6 aR7 aS5 bP2 bM2 aK6 aG1 aS6 "
            "aK7 bG0 aS7 bP3 bM3 aH0 aG2 aH1 bZ0 bG1 aH2 aG3 bZ1 bG2 aH3 bG3 bZ2 bZ3 bO0 bO1 bO2 bO3 aO0 bF "
            "aO1 aO2 aO3").split()


def _prompt_kernel(x_ref, an_ref, awin_ref, cw_ref, cb_ref, wri_ref, br_ref, bi_ref, lam_ref,
                   awout_ref, bn_ref, bwin_ref, wgrp_ref, bgrp_ref, bscale_ref, bwout_ref, fn_ref,
                   y_ref, conv_ref, lru_ref, pool_ref,
                   ext_s, sa_s, sb_s, sh_s, h_s, h1_s, px_s, pt_s,
                   ua_s, hx_s, hxb_s, zr_s, ga_s, hsg_s, ub_s, pooled_s, z_s, gq_s, zg_s, *,
                   blocks_per_seq):
    n = pl.program_id(0)
    n_blocks = pl.num_programs(0) - 1
    ta = jnp.minimum(n, n_blocks - 1) % blocks_per_seq
    tbk = jnp.maximum(n - 1, 0) % blocks_per_seq
    tb, d = x_ref.shape[1], x_ref.shape[2]
    n_slabs = d // LANES
    n_chunks = d // MXU_COLS
    slabs_per_chunk = MXU_COLS // LANES
    assert d // len(POOL_WINDOWS) == MXU_COLS, "one pooling group per matmul column chunk"
    seg = tb // SUBLANES
    pitch = seg + 4
    conv_ahead = hx_s.shape[0]

    @pl.when(n == 0)
    def _():
        h1_s[...] = jnp.zeros_like(h1_s)

    @pl.when(ta == 0)
    def _():
        ext_s[:, 0:CONV_PAD, :] = jnp.zeros((n_slabs, CONV_PAD, LANES), F32)
        h_s[...] = jnp.zeros_like(h_s)

    @pl.when(tbk == 0)
    def _():
        px_s[:, 0:POOL_PAD, :] = jnp.zeros((n_slabs, POOL_PAD, LANES), F32)

    h1_in = h1_s.at[(n + 1) % 2]
    h1_out = h1_s.at[n % 2]
    sub = lax.broadcasted_iota(jnp.int32, (SUBLANES, 1), 0)
    first_token = jnp.logical_and(ta == 0, sub == 0)
    row16 = lax.broadcasted_iota(jnp.int32, (2 * SUBLANES, 1), 0)
    pos16 = tbk * tb + row16
    hn_row = _half_neg_c_softplus_neg(lam_ref[...])
    hcw, hcb = 0.5 * cw_ref[...], 0.5 * cb_ref[...]
    hbr, hbi = 0.5 * br_ref[...], 0.5 * bi_ref[...]
    carries = [None] * n_slabs
    stages = {}

    def a_norm():
        ua_s[...] = _rmsnorm(x_ref[0], an_ref[...]).astype(BF16)

    def a_xb_chunk(q):
        pq = jnp.dot(ua_s[...], awin_ref[:, _cols(q)], preferred_element_type=F32)
        for i in range(slabs_per_chunk):
            ext_s[q * slabs_per_chunk + i, CONV_PAD:CONV_PAD + tb, :] = pq[:, i * LANES:(i + 1) * LANES]

    def a_gate_chunk(q):
        ga_s[q] = jnp.dot(ua_s[...], awin_ref[:, _cols(n_chunks + q)], preferred_element_type=F32)

    def a_conv(c):
        lanes = slice(c * LANES, (c + 1) * LANES)
        for j in range(SUBLANES):
            r0 = CONV_PAD + j * seg
            hx = ext_s[c, r0:r0 + seg, :] * hcw[CONV_W - 1:CONV_W, lanes] + hcb[:, lanes]
            for k in range(CONV_W - 1):
                hx = hx + ext_s[c, pl.ds(r0 - (CONV_W - 1) + k, seg), :] * hcw[k:k + 1, lanes]
            hx_s[c % conv_ahead, j * seg:(j + 1) * seg, :] = hx
            hxb_s[c % conv_ahead, j * seg:(j + 1) * seg, :] = hx.astype(BF16)

    def a_gate_proj(c):
        zr_s[c % 2] = jnp.dot(hxb_s[c % conv_ahead], wri_ref[c], preferred_element_type=F32)

    def a_coeffs(c):
        lanes = slice(c * LANES, (c + 1) * LANES)
        for j in range(SUBLANES):
            rows = slice(j * seg, (j + 1) * seg)
            a, mult, gated = _lru_terms(hx_s[c % conv_ahead, rows, :], zr_s[c % 2, rows, :LANES],
                                        zr_s[c % 2, rows, LANES:], hbr[:, lanes], hbi[:, lanes],
                                        hn_row[:, lanes])
            if j == 0:
                mult = jnp.concatenate(
                    [jnp.where(first_token, 1.0, mult[0:SUBLANES]), mult[SUBLANES:]], axis=0)
            sa_s[c, pl.ds(j * pitch, seg), :] = a
            sb_s[c, pl.ds(j * pitch, seg), :] = mult * gated

    def a_scan(c):
        hl = sb_s[c, pl.ds(0, SUBLANES, stride=pitch), :]
        pp = sa_s[c, pl.ds(0, SUBLANES, stride=pitch), :]
        for i in range(1, seg):
            av = sa_s[c, pl.ds(i, SUBLANES, stride=pitch), :]
            hl = av * hl + sb_s[c, pl.ds(i, SUBLANES, stride=pitch), :]
            pp = av * pp
        h_in = jnp.broadcast_to(_lane_slab(h_s, c), (SUBLANES, LANES))
        cin = h_in
        for _ in range(SUBLANES - 1):
            cin = jnp.where(sub == 0, h_in, pltpu.roll(hl + pp * cin, 1, 0))
        carries[c] = (hl + pp * cin)[SUBLANES - 1:SUBLANES]
        h = cin
        for i in range(seg):
            h = sa_s[c, pl.ds(i, SUBLANES, stride=pitch), :] * h + sb_s[c, pl.ds(i, SUBLANES, stride=pitch), :]
            sh_s[c, pl.ds(i, SUBLANES, stride=pitch), :] = h

    def a_gated_out(q):
        hs = jnp.concatenate(
            [jnp.concatenate([sh_s[c, pl.ds(j * pitch, seg), :] for j in range(SUBLANES)], axis=0)
             for c in range(q * slabs_per_chunk, (q + 1) * slabs_per_chunk)], axis=1)
        hsg_s[:, _cols(q)] = (hs * _silu_from_half(ga_s[q])).astype(BF16)

    def a_out_proj(q):
        y0 = jnp.dot(hsg_s[...], awout_ref[:, _cols(q)], preferred_element_type=F32)
        h1_out[:, _cols(q)] = x_ref[0, :, _cols(q)] + y0

    def b_norm():
        ub_s[...] = _rmsnorm(h1_in[...], bn_ref[...]).astype(BF16)

    def b_xb_chunk(q):
        pq = jnp.dot(ub_s[...], bwin_ref[:, _cols(q)], preferred_element_type=F32)
        for i in range(slabs_per_chunk):
            px_s[q * slabs_per_chunk + i, POOL_PAD:POOL_PAD + tb, :] = pq[:, i * LANES:(i + 1) * LANES]

    def pool_slab(c, w):
        levels = w.bit_length() - 1
        tmp = c % slabs_per_chunk
        src = px_s.at[c]
        s = None
        for k in range(levels):
            back = (levels - 1 - k) * SUBLANES
            rows = tb + back
            s = (src[pl.ds(POOL_PAD - back, rows), :]
                 + src[pl.ds(POOL_PAD - back - (1 << k), rows), :])
            if k < levels - 1:
                pt_s[k % 2, tmp, POOL_PAD - back:POOL_PAD + tb, :] = s
                src = pt_s.at[k % 2, tmp]
        inv_cnt = 1.0 / jnp.minimum(pos16 + 1, w).astype(F32)
        mean = jnp.concatenate([s[0:2 * SUBLANES] * inv_cnt, s[2 * SUBLANES:] * (1.0 / w)], axis=0)
        return mean - px_s[c, POOL_PAD:POOL_PAD + tb, :]

    def b_pool(q):
        pooled_s[:, _cols(q)] = jnp.concatenate(
            [pool_slab(q * slabs_per_chunk + i, POOL_WINDOWS[q]) for i in range(slabs_per_chunk)],
            axis=1).astype(BF16)

    def b_group_proj(q):
        z_s[:, _cols(q)] = jnp.dot(pooled_s[:, _cols(q)], wgrp_ref[q], preferred_element_type=F32)

    def b_gate_chunk(q):
        gq_s[q] = jnp.dot(ub_s[...], bwin_ref[:, _cols(n_chunks + q)], preferred_element_type=F32)

    def b_gated_proj(q):
        z1 = (z_s[:, _cols(q)] + bgrp_ref[:, _cols(q)]) * bscale_ref[:, _cols(q)]
        zg_s[:, _cols(q)] = (z1 * _silu_from_half(gq_s[q])).astype(BF16)

    def b_out_proj(q):
        y1 = jnp.dot(zg_s[...], bwout_ref[:, _cols(q)], preferred_element_type=F32)
        y_ref[0, :, _cols(q)] = h1_in[:, _cols(q)] + y1

    def b_final_norm():
        y_ref[0] = _rmsnorm(y_ref[0], fn_ref[...])

    stages["aN"] = a_norm
    stages["bN"] = b_norm
    stages["bF"] = b_final_norm
    for q in range(n_chunks):
        for name, fn in (("aX", a_xb_chunk), ("aG", a_gate_chunk), ("aH", a_gated_out),
                         ("aO", a_out_proj), ("bX", b_xb_chunk), ("bP", b_pool),
                         ("bM", b_group_proj), ("bG", b_gate_chunk), ("bZ", b_gated_proj),
                         ("bO", b_out_proj)):
            stages[f"{name}{q}"] = functools.partial(fn, q)
    for c in range(n_slabs):
        for name, fn in (("aC", a_conv), ("aR", a_gate_proj), ("aK", a_coeffs), ("aS", a_scan)):
            stages[f"{name}{c}"] = functools.partial(fn, c)

    order = _prompt_stage_order(n_chunks, slabs_per_chunk)
    assert sorted(order) == sorted(stages), "every stage is issued exactly once"
    at = order.index
    for c in range(n_slabs):
        assert c < conv_ahead or at(f"aK{c - conv_ahead}") < at(f"aC{c}")
        assert c < zr_s.shape[0] or at(f"aK{c - zr_s.shape[0]}") < at(f"aR{c}")
    for name in order:
        stages[name]()
    h_s[...] = jnp.concatenate(carries, axis=1)

    @pl.when(jnp.logical_and(ta == blocks_per_seq - 1, n < n_blocks))
    def _():
        for c in range(n_slabs):
            conv_ref[0, :, c * LANES:(c + 1) * LANES] = ext_s[c, tb + CONV_PAD - (CONV_W - 1):tb + CONV_PAD, :]
        lru_ref[0] = h_s[...]

    @pl.when(jnp.logical_and(tbk == blocks_per_seq - 1, n > 0))
    def _():
        for c in range(n_slabs):
            pool_ref[0, :, c * LANES:(c + 1) * LANES] = px_s[c, tb + POOL_PAD - POOL_PAST:tb + POOL_PAD, :]

    ext_s[:, 0:CONV_PAD, :] = ext_s[:, tb:tb + CONV_PAD, :]
    px_s[:, 0:POOL_PAD, :] = px_s[:, tb:tb + POOL_PAD, :]


def _cast_rows(dst, src, scale=None, rows_per_step=128):
    for r in range(0, src.shape[0], rows_per_step):
        v = src[r:r + rows_per_step, :]
        dst[r:r + rows_per_step, :] = (v if scale is None else v * scale).astype(dst.dtype)


def _sample_kernel(x_hbm, sconv_hbm, slru_hbm, spool_hbm,
                   awin_f, wr_f, wi_f, awout_f, bwin_f, wgrp_f, bwout_f,
                   an_ref, cw_ref, cb_ref, br_ref, bi_ref, lam_ref, bn_ref, bgrp_ref, bscale_ref, fn_ref,
                   y_hbm, conv_hbm, lru_hbm, pool_hbm,
                   awin_o, wri_o, awout_o, bwin_o, wgrp_o, bwout_o,
                   x_v, sconv_v, slru_v, spool_v, y_v, conv_v, lru_v, xb1_v,
                   awin_v, wri_v, awout_v, bwin_v, wgrp_v, bwout_v,
                   stage_in, stage_out, stage_r, stage_i, stage_g, load_sem, store_sem):
    ts, nb, d = x_v.shape
    lb = d // N_LRU_BLOCKS
    gw = d // len(POOL_WINDOWS)
    kept = POOL_PAST - ts

    sources = (("x", x_hbm, x_v), ("awin", awin_f.at[0], stage_in), ("sconv", sconv_hbm, sconv_v),
               ("wr", wr_f.at[0], stage_r), ("wi", wi_f.at[0], stage_i), ("slru", slru_hbm, slru_v),
               ("awout", awout_f.at[0], stage_out), ("spool", spool_hbm, spool_v),
               ("wgrp", wgrp_f.at[0], stage_g),
               ("bwin", bwin_f.at[0], stage_in), ("bwout", bwout_f.at[0], stage_out))
    loads = {name: pltpu.make_async_copy(src, dst, load_sem.at[i])
             for i, (name, src, dst) in enumerate(sources)}
    for name in ("x", "awin", "sconv", "wr", "wi", "slru", "awout", "wgrp"):
        loads[name].start()
    sinks = (("conv", conv_v, conv_hbm), ("lru", lru_v, lru_hbm),
             ("pool_kept", spool_v.at[pl.ds(ts, kept)], pool_hbm.at[pl.ds(0, kept)]),
             ("pool_new", xb1_v, pool_hbm.at[pl.ds(kept, ts)]), ("y", y_v, y_hbm),
             ("awin", awin_v, awin_o), ("wri", wri_v, wri_o), ("awout", awout_v, awout_o),
             ("bwin", bwin_v, bwin_o), ("wgrp", wgrp_v, wgrp_o), ("bwout", bwout_v, bwout_o))
    stores = {name: pltpu.make_async_copy(src, dst, store_sem.at[i])
              for i, (name, src, dst) in enumerate(sinks)}
    gate_half = jnp.where(lax.broadcasted_iota(jnp.int32, (1, 2 * d), 1) < d, 1.0, 0.5)

    def slab(v, i):
        return v[i * nb:(i + 1) * nb]

    loads["x"].wait()
    x = x_v[...].reshape(ts * nb, d)
    u = _rmsnorm(x, an_ref[...]).astype(BF16)
    loads["awin"].wait()
    _cast_rows(awin_v, stage_in, gate_half)
    for name in ("bwin", "spool"):
        loads[name].start()
    stores["awin"].start()
    proj = jnp.dot(u, awin_v[...], preferred_element_type=F32)
    xb = proj[:, :d]
    for name in ("sconv", "wr", "wi", "slru", "awout"):
        loads[name].wait()
    for n in range(N_LRU_BLOCKS):
        wri_v[n, :, :lb] = stage_r[n].astype(BF16)
        wri_v[n, :, lb:] = stage_i[n].astype(BF16)
    _cast_rows(awout_v, stage_out)
    loads["bwout"].start()
    for name in ("wri", "awout"):
        stores[name].start()
    ext = [sconv_v[k] for k in range(CONV_W - 1)] + [slab(xb, i) for i in range(ts)]
    hcw, hcb = 0.5 * cw_ref[...], 0.5 * cb_ref[...]
    hx_slabs = []
    for i in range(ts):
        acc = hcb
        for k in range(CONV_W):
            acc = acc + ext[i + k] * hcw[k:k + 1, :]
        hx_slabs.append(acc)
    hx = jnp.concatenate(hx_slabs, axis=0)
    hx_bf16 = hx.astype(BF16)
    zr, zi = [], []
    for n in range(N_LRU_BLOCKS):
        z = jnp.dot(hx_bf16[:, n * lb:(n + 1) * lb], wri_v[n], preferred_element_type=F32)
        zr.append(z[:, :lb])
        zi.append(z[:, lb:])
    a, mult, gated = _lru_terms(hx, jnp.concatenate(zr, axis=1), jnp.concatenate(zi, axis=1),
                                0.5 * br_ref[...], 0.5 * bi_ref[...],
                                _half_neg_c_softplus_neg(lam_ref[...]))
    h = slru_v[...]
    hs_slabs = []
    for i in range(ts):
        m = slab(mult, i)
        if PAST_LEN + i == 0:
            m = jnp.ones_like(m)
        h = slab(a, i) * h + m * slab(gated, i)
        hs_slabs.append(h)
    hs = jnp.concatenate(hs_slabs, axis=0)
    y0 = jnp.dot((hs * _silu_from_half(proj[:, d:])).astype(BF16), awout_v[...],
                 preferred_element_type=F32)
    h1 = x + y0
    lru_v[...] = h
    for k in range(CONV_W - 1):
        conv_v[k] = ext[ts + k]
    for name in ("bwin", "spool", "wgrp", "bwout"):
        loads[name].wait()
    _cast_rows(bwin_v, stage_in, gate_half)
    for g in range(len(POOL_WINDOWS)):
        wgrp_v[g] = stage_g[g].astype(BF16)
    _cast_rows(bwout_v, stage_out)
    for name in ("conv", "lru", "pool_kept", "bwin", "wgrp", "bwout"):
        stores[name].start()

    u1 = _rmsnorm(h1, bn_ref[...]).astype(BF16)
    proj1 = jnp.dot(u1, bwin_v[...], preferred_element_type=F32)
    xb1 = proj1[:, :d]
    ext2 = [spool_v[k] for k in range(POOL_PAST)] + [slab(xb1, i) for i in range(ts)]
    pooled_slabs = []
    for i in range(ts):
        means = []
        for gi, w in enumerate(POOL_WINDOWS):
            lanes = slice(gi * gw, (gi + 1) * gw)
            s = ext2[POOL_PAST + i][:, lanes]
            for j in range(1, w):
                s = s + ext2[POOL_PAST + i - j][:, lanes]
            means.append(s * (1.0 / min(PAST_LEN + i + 1, w)))
        pooled_slabs.append(jnp.concatenate(means, axis=1) - ext2[POOL_PAST + i])
    pooled = jnp.concatenate(pooled_slabs, axis=0).astype(BF16)
    z1 = jnp.concatenate(
        [jnp.dot(pooled[:, g * gw:(g + 1) * gw], wgrp_v[g], preferred_element_type=F32)
         for g in range(len(POOL_WINDOWS))], axis=1)
    z1 = (z1 + bgrp_ref[...]) * bscale_ref[...]
    y1 = jnp.dot((z1 * _silu_from_half(proj1[:, d:])).astype(BF16), bwout_v[...],
                 preferred_element_type=F32)
    h2 = h1 + y1
    xb1_v[...] = xb1.reshape(ts, nb, d)
    y_v[...] = _rmsnorm(h2, fn_ref[...]).reshape(ts, nb, d)
    for name in ("pool_new", "y"):
        stores[name].start()
    for store in stores.values():
        store.wait()


def _whole(shape):
    return pl.BlockSpec(shape, lambda *_: (0,) * len(shape), pipeline_mode=pl.Buffered(1))


def kernel(x_prompt, x_sample, state_conv, state_lru, state_pool, a_norm, a_w_in, a_conv_w, a_conv_b, a_w_r, a_b_r, a_w_i, a_b_i, a_lam, a_w_out, b_norm, b_w_in, b_w_grp, b_b_grp, b_scale, b_w_out, final_norm):
    bp, tp, d = x_prompt.shape
    bs, ts, _ = x_sample.shape
    tb = PROMPT_BLOCK_ROWS
    assert a_norm.shape[0] == 1 and b_norm.shape[0] == 1, "one layer of each mixer type"
    assert d % MXU_COLS == 0 and d // N_LRU_BLOCKS == LANES
    assert tp % tb == 0 and tb % (SUBLANES * SUBLANES) == 0 and tb >= 2 * SUBLANES
    assert CONV_W - 1 <= ts <= POOL_PAST

    row = lambda v: v.reshape(1, d)
    n_slabs = d // LANES
    n_chunks = d // MXU_COLS
    lb = d // N_LRU_BLOCKS
    gw = d // len(POOL_WINDOWS)

    xs = jnp.swapaxes(x_sample, 0, 1)
    sconv = jnp.swapaxes(state_conv[0], 0, 1)
    spool = jnp.swapaxes(state_pool[0], 0, 1)
    hbm = pl.BlockSpec(memory_space=pl.ANY)
    act = lambda shape: pltpu.VMEM(shape, F32)
    matrices = (a_w_in, a_w_r, a_w_i, a_w_out, b_w_in, b_w_grp, b_w_out)
    vectors = (row(a_norm[0]), a_conv_w[0], row(a_conv_b[0]), row(a_b_r[0]), row(a_b_i[0]),
               row(a_lam[0]), row(b_norm[0]), row(b_b_grp[0]), row(b_scale[0]), row(final_norm))
    bf16_shapes = ((d, 2 * d), (N_LRU_BLOCKS, lb, 2 * lb), (d, d), (d, 2 * d),
                   (len(POOL_WINDOWS), gw, gw), (d, d))
    (y_s, conv_s, lru_s, pool_s,
     awin, wri, awout, bwin, wgrp, bwout) = pl.pallas_call(
        _sample_kernel,
        in_specs=[hbm] * (4 + len(matrices)) + [_whole(v.shape) for v in vectors],
        out_specs=[hbm] * (4 + len(bf16_shapes)),
        out_shape=[jax.ShapeDtypeStruct(xs.shape, F32),
                   jax.ShapeDtypeStruct(sconv.shape, F32),
                   jax.ShapeDtypeStruct((bs, d), F32),
                   jax.ShapeDtypeStruct(spool.shape, F32)]
        + [jax.ShapeDtypeStruct(shape, BF16) for shape in bf16_shapes],
        scratch_shapes=[act(xs.shape), act(sconv.shape), act((bs, d)), act(spool.shape),
                        act(xs.shape), act(sconv.shape), act((bs, d)), act(xs.shape)]
        + [pltpu.VMEM(shape, BF16) for shape in bf16_shapes]
        + [act((d, 2 * d)), act((d, d)), act((N_LRU_BLOCKS, lb, lb)), act((N_LRU_BLOCKS, lb, lb)),
           act((len(POOL_WINDOWS), gw, gw)),
           pltpu.SemaphoreType.DMA((11,)), pltpu.SemaphoreType.DMA((11,))],
        compiler_params=pltpu.CompilerParams(vmem_limit_bytes=VMEM_LIMIT_BYTES),
        name="sample_step",
    )(xs, sconv, state_lru[0], spool, *matrices, *vectors)

    weights = (row(a_norm[0]), awin, a_conv_w[0], row(a_conv_b[0]), wri, row(a_b_r[0]),
               row(a_b_i[0]), row(a_lam[0]), awout, row(b_norm[0]), bwin, wgrp, row(b_b_grp[0]),
               row(b_scale[0]), bwout, row(final_norm))
    weight_specs = [_whole(w.shape) for w in weights]
    seg_rows = SUBLANES * (tb // SUBLANES + 4)
    nt = tp // tb
    n_blocks = bp * nt
    blk0 = lambda n: jnp.minimum(n, n_blocks - 1)
    blk1 = lambda n: jnp.maximum(n - 1, 0)
    y_p, conv_p, lru_p, pool_p = pl.pallas_call(
        functools.partial(_prompt_kernel, blocks_per_seq=nt),
        grid=(n_blocks + 1,),
        in_specs=[pl.BlockSpec((1, tb, d), lambda n: (blk0(n) // nt, blk0(n) % nt, 0))] + weight_specs,
        out_specs=[pl.BlockSpec((1, tb, d), lambda n: (blk1(n) // nt, blk1(n) % nt, 0)),
                   pl.BlockSpec((1, CONV_W - 1, d), lambda n: (blk0(n) // nt, 0, 0)),
                   pl.BlockSpec((1, 1, d), lambda n: (blk0(n) // nt, 0, 0)),
                   pl.BlockSpec((1, POOL_PAST, d), lambda n: (blk1(n) // nt, 0, 0))],
        out_shape=[jax.ShapeDtypeStruct((bp, tp, d), F32),
                   jax.ShapeDtypeStruct((bp, CONV_W - 1, d), F32),
                   jax.ShapeDtypeStruct((bp, 1, d), F32),
                   jax.ShapeDtypeStruct((bp, POOL_PAST, d), F32)],
        scratch_shapes=[pltpu.VMEM((n_slabs, CONV_PAD + tb, LANES), F32),
                        pltpu.VMEM((n_slabs, seg_rows, LANES), F32),
                        pltpu.VMEM((n_slabs, seg_rows, LANES), F32),
                        pltpu.VMEM((n_slabs, seg_rows, LANES), F32),
                        pltpu.VMEM((1, d), F32),
                        pltpu.VMEM((2, tb, d), F32),
                        pltpu.VMEM((n_slabs, POOL_PAD + tb, LANES), F32),
                        pltpu.VMEM((2, MXU_COLS // LANES, POOL_PAD + tb, LANES), F32),
                        pltpu.VMEM((tb, d), BF16),
                        pltpu.VMEM((CONV_AHEAD, tb, LANES), F32),
                        pltpu.VMEM((CONV_AHEAD, tb, LANES), BF16),
                        pltpu.VMEM((2, tb, 2 * LANES), F32),
                        pltpu.VMEM((n_chunks, tb, MXU_COLS), F32),
                        pltpu.VMEM((tb, d), BF16),
                        pltpu.VMEM((tb, d), BF16),
                        pltpu.VMEM((tb, d), BF16),
                        pltpu.VMEM((tb, d), F32),
                        pltpu.VMEM((n_chunks, tb, MXU_COLS), F32),
                        pltpu.VMEM((tb, d), BF16)],
        compiler_params=pltpu.CompilerParams(
            vmem_limit_bytes=VMEM_LIMIT_BYTES,
            dimension_semantics=("arbitrary",)),
        name="prompt_step",
    )(x_prompt, *weights)

    return (y_p, jnp.swapaxes(y_s, 0, 1),
            conv_p[None], lru_p.reshape(1, bp, d), pool_p[None],
            jnp.swapaxes(conv_s, 0, 1)[None], lru_s[None], jnp.swapaxes(pool_s, 0, 1)[None])
```

```python
import functools

import jax
import jax.numpy as jnp
from jax import lax
from jax.experimental import pallas as pl
from jax.experimental.pallas import tpu as pltpu

PAST_LEN = 16384
N_LRU_BLOCKS = 8
CONV_W = 4
LRU_C = 8.0
POOL_WINDOWS = (2, 4, 8, 16)
POOL_PAST = max(POOL_WINDOWS) - 1
EPS = 1e-6

LANES = 128
SUBLANES = 8
MXU_COLS = 256
CONV_PAD = SUBLANES
POOL_LEVELS = max(POOL_WINDOWS).bit_length() - 1
POOL_PAD = POOL_LEVELS * SUBLANES
PROMPT_BLOCK_ROWS = 512
CONV_AHEAD = 4
VMEM_LIMIT_BYTES = 56 * 1024 * 1024

F32 = jnp.float32
BF16 = jnp.bfloat16


def _rmsnorm(x, g):
    return x * lax.rsqrt(jnp.mean(x * x, axis=-1, keepdims=True) + EPS) * g


def _silu_from_half(hg):
    return hg + hg * jnp.tanh(hg)


def _half_neg_c_softplus_neg(lam):
    z = -lam
    return (-0.5 * LRU_C) * (jnp.maximum(z, 0.0) + jnp.log1p(jnp.exp(-jnp.abs(z))))


def _lru_terms(hx, zr, zi, hbr, hbi, hn):
    log_a = hn + hn * jnp.tanh(zr + hbr)
    a = jnp.exp(log_a)
    y = jnp.tanh(log_a) * (-1.0 - a * a)
    mult = jnp.where(y > 0.0, y * lax.rsqrt(y), 0.0)
    gated = hx + hx * jnp.tanh(zi + hbi)
    return a, mult, gated


def _lane_slab(ref, c):
    return ref[:, c * LANES:(c + 1) * LANES]


def _cols(q):
    return slice(q * MXU_COLS, (q + 1) * MXU_COLS)


def _prompt_stage_order(n_chunks, slabs_per_chunk):
    assert (n_chunks, slabs_per_chunk) == (4, 2)
    return ("aN aX0 bN aX1 aC0 bX0 aC1 aC2 aC3 aR0 aR1 aK0 aX2 aS0 aK1 bX1 aS1 bP0 aR2 aC4 aR3 bM0 aC5 aX3 "
            "aK2 aS2 bX2 aK3 aS3 aR4 bP1 aR5 bM1 aC6 aG0 aC7 aK4 bX3 aS4 aK5 aR6 aR7 aS5 bP2 bM2 aK6 aG1 aS6 "
            "aK7 bG0 aS7 bP3 bM3 aH0 aG2 aH1 bZ0 bG1 aH2 aG3 bZ1 bG2 aH3 bG3 bZ2 bZ3 bO0 bO1 bO2 bO3 aO0 bF "
            "aO1 aO2 aO3").split()


def _prompt_kernel(x_ref, an_ref, awin_ref, cw_ref, cb_ref, wri_ref, br_ref, bi_ref, lam_ref,
                   awout_ref, bn_ref, bwin_ref, wgrp_ref, bgrp_ref, bscale_ref, bwout_ref, fn_ref,
                   y_ref, conv_ref, lru_ref, pool_ref,
                   ext_s, sa_s, sb_s, sh_s, h_s, h1_s, px_s, pt_s,
                   ua_s, hx_s, hxb_s, zr_s, ga_s, hsg_s, ub_s, pooled_s, z_s, gq_s, zg_s, *,
                   blocks_per_seq):
    n = pl.program_id(0)
    n_blocks = pl.num_programs(0) - 1
    ta = jnp.minimum(n, n_blocks - 1) % blocks_per_seq
    tbk = jnp.maximum(n - 1, 0) % blocks_per_seq
    tb, d = x_ref.shape[1], x_ref.shape[2]
    n_slabs = d // LANES
    n_chunks = d // MXU_COLS
    slabs_per_chunk = MXU_COLS // LANES
    assert d // len(POOL_WINDOWS) == MXU_COLS, "one pooling group per matmul column chunk"
    seg = tb // SUBLANES
    pitch = seg + 4
    conv_ahead = hx_s.shape[0]

    @pl.when(n == 0)
    def _():
        h1_s[...] = jnp.zeros_like(h1_s)

    @pl.when(ta == 0)
    def _():
        ext_s[:, 0:CONV_PAD, :] = jnp.zeros((n_slabs, CONV_PAD, LANES), F32)
        h_s[...] = jnp.zeros_like(h_s)

    @pl.when(tbk == 0)
    def _():
        px_s[:, 0:POOL_PAD, :] = jnp.zeros((n_slabs, POOL_PAD, LANES), F32)

    h1_in = h1_s.at[(n + 1) % 2]
    h1_out = h1_s.at[n % 2]
    sub = lax.broadcasted_iota(jnp.int32, (SUBLANES, 1), 0)
    first_token = jnp.logical_and(ta == 0, sub == 0)
    row16 = lax.broadcasted_iota(jnp.int32, (2 * SUBLANES, 1), 0)
    pos16 = tbk * tb + row16
    hn_row = _half_neg_c_softplus_neg(lam_ref[...])
    hcw, hcb = 0.5 * cw_ref[...], 0.5 * cb_ref[...]
    hbr, hbi = 0.5 * br_ref[...], 0.5 * bi_ref[...]
    carries = [None] * n_slabs
    stages = {}

    def a_norm():
        ua_s[...] = _rmsnorm(x_ref[0], an_ref[...]).astype(BF16)

    def a_xb_chunk(q):
        pq = jnp.dot(ua_s[...], awin_ref[:, _cols(q)], preferred_element_type=F32)
        for i in range(slabs_per_chunk):
            ext_s[q * slabs_per_chunk + i, CONV_PAD:CONV_PAD + tb, :] = pq[:, i * LANES:(i + 1) * LANES]

    def a_gate_chunk(q):
        ga_s[q] = jnp.dot(ua_s[...], awin_ref[:, _cols(n_chunks + q)], preferred_element_type=F32)

    def a_conv(c):
        lanes = slice(c * LANES, (c + 1) * LANES)
        for j in range(SUBLANES):
            r0 = CONV_PAD + j * seg
            hx = ext_s[c, r0:r0 + seg, :] * hcw[CONV_W - 1:CONV_W, lanes] + hcb[:, lanes]
            for k in range(CONV_W - 1):
                hx = hx + ext_s[c, pl.ds(r0 - (CONV_W - 1) + k, seg), :] * hcw[k:k + 1, lanes]
            hx_s[c % conv_ahead, j * seg:(j + 1) * seg, :] = hx
            hxb_s[c % conv_ahead, j * seg:(j + 1) * seg, :] = hx.astype(BF16)

    def a_gate_proj(c):
        zr_s[c % 2] = jnp.dot(hxb_s[c % conv_ahead], wri_ref[c], preferred_element_type=F32)

    def a_coeffs(c):
        lanes = slice(c * LANES, (c + 1) * LANES)
        for j in range(SUBLANES):
            rows = slice(j * seg, (j + 1) * seg)
            a, mult, gated = _lru_terms(hx_s[c % conv_ahead, rows, :], zr_s[c % 2, rows, :LANES],
                                        zr_s[c % 2, rows, LANES:], hbr[:, lanes], hbi[:, lanes],
                                        hn_row[:, lanes])
            if j == 0:
                mult = jnp.concatenate(
                    [jnp.where(first_token, 1.0, mult[0:SUBLANES]), mult[SUBLANES:]], axis=0)
            sa_s[c, pl.ds(j * pitch, seg), :] = a
            sb_s[c, pl.ds(j * pitch, seg), :] = mult * gated

    def a_scan(c):
        hl = sb_s[c, pl.ds(0, SUBLANES, stride=pitch), :]
        pp = sa_s[c, pl.ds(0, SUBLANES, stride=pitch), :]
        for i in range(1, seg):
            av = sa_s[c, pl.ds(i, SUBLANES, stride=pitch), :]
            hl = av * hl + sb_s[c, pl.ds(i, SUBLANES, stride=pitch), :]
            pp = av * pp
        h_in = jnp.broadcast_to(_lane_slab(h_s, c), (SUBLANES, LANES))
        cin = h_in
        for _ in range(SUBLANES - 1):
            cin = jnp.where(sub == 0, h_in, pltpu.roll(hl + pp * cin, 1, 0))
        carries[c] = (hl + pp * cin)[SUBLANES - 1:SUBLANES]
        h = cin
        for i in range(seg):
            h = sa_s[c, pl.ds(i, SUBLANES, stride=pitch), :] * h + sb_s[c, pl.ds(i, SUBLANES, stride=pitch), :]
            sh_s[c, pl.ds(i, SUBLANES, stride=pitch), :] = h

    def a_gated_out(q):
        hs = jnp.concatenate(
            [jnp.concatenate([sh_s[c, pl.ds(j * pitch, seg), :] for j in range(SUBLANES)], axis=0)
             for c in range(q * slabs_per_chunk, (q + 1) * slabs_per_chunk)], axis=1)
        hsg_s[:, _cols(q)] = (hs * _silu_from_half(ga_s[q])).astype(BF16)

    def a_out_proj(q):
        y0 = jnp.dot(hsg_s[...], awout_ref[:, _cols(q)], preferred_element_type=F32)
        h1_out[:, _cols(q)] = x_ref[0, :, _cols(q)] + y0

    def b_norm():
        ub_s[...] = _rmsnorm(h1_in[...], bn_ref[...]).astype(BF16)

    def b_xb_chunk(q):
        pq = jnp.dot(ub_s[...], bwin_ref[:, _cols(q)], preferred_element_type=F32)
        for i in range(slabs_per_chunk):
            px_s[q * slabs_per_chunk + i, POOL_PAD:POOL_PAD + tb, :] = pq[:, i * LANES:(i + 1) * LANES]

    def pool_slab(c, w):
        levels = w.bit_length() - 1
        tmp = c % slabs_per_chunk
        src = px_s.at[c]
        s = None
        for k in range(levels):
            back = (levels - 1 - k) * SUBLANES
            rows = tb + back
            s = (src[pl.ds(POOL_PAD - back, rows), :]
                 + src[pl.ds(POOL_PAD - back - (1 << k), rows), :])
            if k < levels - 1:
                pt_s[k % 2, tmp, POOL_PAD - back:POOL_PAD + tb, :] = s
                src = pt_s.at[k % 2, tmp]
        inv_cnt = 1.0 / jnp.minimum(pos16 + 1, w).astype(F32)
        mean = jnp.concatenate([s[0:2 * SUBLANES] * inv_cnt, s[2 * SUBLANES:] * (1.0 / w)], axis=0)
        return mean - px_s[c, POOL_PAD:POOL_PAD + tb, :]

    def b_pool(q):
        pooled_s[:, _cols(q)] = jnp.concatenate(
            [pool_slab(q * slabs_per_chunk + i, POOL_WINDOWS[q]) for i in range(slabs_per_chunk)],
            axis=1).astype(BF16)

    def b_group_proj(q):
        z_s[:, _cols(q)] = jnp.dot(pooled_s[:, _cols(q)], wgrp_ref[q], preferred_element_type=F32)

    def b_gate_chunk(q):
        gq_s[q] = jnp.dot(ub_s[...], bwin_ref[:, _cols(n_chunks + q)], preferred_element_type=F32)

    def b_gated_proj(q):
        z1 = (z_s[:, _cols(q)] + bgrp_ref[:, _cols(q)]) * bscale_ref[:, _cols(q)]
        zg_s[:, _cols(q)] = (z1 * _silu_from_half(gq_s[q])).astype(BF16)

    def b_out_proj(q):
        y1 = jnp.dot(zg_s[...], bwout_ref[:, _cols(q)], preferred_element_type=F32)
        y_ref[0, :, _cols(q)] = h1_in[:, _cols(q)] + y1

    def b_final_norm():
        y_ref[0] = _rmsnorm(y_ref[0], fn_ref[...])

    stages["aN"] = a_norm
    stages["bN"] = b_norm
    stages["bF"] = b_final_norm
    for q in range(n_chunks):
        for name, fn in (("aX", a_xb_chunk), ("aG", a_gate_chunk), ("aH", a_gated_out),
                         ("aO", a_out_proj), ("bX", b_xb_chunk), ("bP", b_pool),
                         ("bM", b_group_proj), ("bG", b_gate_chunk), ("bZ", b_gated_proj),
                         ("bO", b_out_proj)):
            stages[f"{name}{q}"] = functools.partial(fn, q)
    for c in range(n_slabs):
        for name, fn in (("aC", a_conv), ("aR", a_gate_proj), ("aK", a_coeffs), ("aS", a_scan)):
            stages[f"{name}{c}"] = functools.partial(fn, c)

    order = _prompt_stage_order(n_chunks, slabs_per_chunk)
    assert sorted(order) == sorted(stages), "every stage is issued exactly once"
    at = order.index
    for c in range(n_slabs):
        assert c < conv_ahead or at(f"aK{c - conv_ahead}") < at(f"aC{c}")
        assert c < zr_s.shape[0] or at(f"aK{c - zr_s.shape[0]}") < at(f"aR{c}")
    for name in order:
        stages[name]()
    h_s[...] = jnp.concatenate(carries, axis=1)

    @pl.when(jnp.logical_and(ta == blocks_per_seq - 1, n < n_blocks))
    def _():
        for c in range(n_slabs):
            conv_ref[0, :, c * LANES:(c + 1) * LANES] = ext_s[c, tb + CONV_PAD - (CONV_W - 1):tb + CONV_PAD, :]
        lru_ref[0] = h_s[...]

    @pl.when(jnp.logical_and(tbk == blocks_per_seq - 1, n > 0))
    def _():
        for c in range(n_slabs):
            pool_ref[0, :, c * LANES:(c + 1) * LANES] = px_s[c, tb + POOL_PAD - POOL_PAST:tb + POOL_PAD, :]

    ext_s[:, 0:CONV_PAD, :] = ext_s[:, tb:tb + CONV_PAD, :]
    px_s[:, 0:POOL_PAD, :] = px_s[:, tb:tb + POOL_PAD, :]


def _cast_rows(dst, src, scale=None, rows_per_step=128):
    for r in range(0, src.shape[0], rows_per_step):
        v = src[r:r + rows_per_step, :]
        dst[r:r + rows_per_step, :] = (v if scale is None else v * scale).astype(dst.dtype)


def _sample_kernel(x_hbm, sconv_hbm, slru_hbm, spool_hbm,
                   awin_f, wr_f, wi_f, awout_f, bwin_f, wgrp_f, bwout_f,
                   an_ref, cw_ref, cb_ref, br_ref, bi_ref, lam_ref, bn_ref, bgrp_ref, bscale_ref, fn_ref,
                   y_hbm, conv_hbm, lru_hbm, pool_hbm,
                   awin_o, wri_o, awout_o, bwin_o, wgrp_o, bwout_o,
                   x_v, sconv_v, slru_v, spool_v, y_v, conv_v, lru_v, xb1_v,
                   awin_v, wri_v, awout_v, bwin_v, wgrp_v, bwout_v,
                   stage_in, stage_out, stage_r, stage_i, stage_g, load_sem, store_sem):
    ts, nb, d = x_v.shape
    lb = d // N_LRU_BLOCKS
    gw = d // len(POOL_WINDOWS)
    kept = POOL_PAST - ts

    sources = (("x", x_hbm, x_v), ("awin", awin_f.at[0], stage_in), ("sconv", sconv_hbm, sconv_v),
               ("wr", wr_f.at[0], stage_r), ("wi", wi_f.at[0], stage_i), ("slru", slru_hbm, slru_v),
               ("awout", awout_f.at[0], stage_out), ("spool", spool_hbm, spool_v),
               ("wgrp", wgrp_f.at[0], stage_g),
               ("bwin", bwin_f.at[0], stage_in), ("bwout", bwout_f.at[0], stage_out))
    loads = {name: pltpu.make_async_copy(src, dst, load_sem.at[i])
             for i, (name, src, dst) in enumerate(sources)}
    for name in ("x", "awin", "sconv", "wr", "wi", "slru", "awout", "spool", "wgrp"):
        loads[name].start()
    sinks = (("conv", conv_v, conv_hbm), ("lru", lru_v, lru_hbm),
             ("pool_kept", spool_v.at[pl.ds(ts, kept)], pool_hbm.at[pl.ds(0, kept)]),
             ("pool_new", xb1_v, pool_hbm.at[pl.ds(kept, ts)]), ("y", y_v, y_hbm),
             ("awin", awin_v, awin_o), ("wri", wri_v, wri_o), ("awout", awout_v, awout_o),
             ("bwin", bwin_v, bwin_o), ("wgrp", wgrp_v, wgrp_o), ("bwout", bwout_v, bwout_o))
    stores = {name: pltpu.make_async_copy(src, dst, store_sem.at[i])
              for i, (name, src, dst) in enumerate(sinks)}
    gate_half = jnp.where(lax.broadcasted_iota(jnp.int32, (1, 2 * d), 1) < d, 1.0, 0.5)

    def slab(v, i):
        return v[i * nb:(i + 1) * nb]

    loads["x"].wait()
    x = x_v[...].reshape(ts * nb, d)
    u = _rmsnorm(x, an_ref[...]).astype(BF16)
    loads["awin"].wait()
    _cast_rows(awin_v, stage_in, gate_half)
    loads["bwin"].start()
    stores["awin"].start()
    proj = jnp.dot(u, awin_v[...], preferred_element_type=F32)
    xb = proj[:, :d]
    for name in ("sconv", "wr", "wi", "slru"):
        loads[name].wait()
    for n in range(N_LRU_BLOCKS):
        wri_v[n, :, :lb] = stage_r[n].astype(BF16)
        wri_v[n, :, lb:] = stage_i[n].astype(BF16)
    stores["wri"].start()
    ext = [sconv_v[k] for k in range(CONV_W - 1)] + [slab(xb, i) for i in range(ts)]
    hcw, hcb = 0.5 * cw_ref[...], 0.5 * cb_ref[...]
    hx_slabs = []
    for i in range(ts):
        acc = hcb
        for k in range(CONV_W):
            acc = acc + ext[i + k] * hcw[k:k + 1, :]
        hx_slabs.append(acc)
    hx = jnp.concatenate(hx_slabs, axis=0)
    hx_bf16 = hx.astype(BF16)
    zr, zi = [], []
    for n in range(N_LRU_BLOCKS):
        z = jnp.dot(hx_bf16[:, n * lb:(n + 1) * lb], wri_v[n], preferred_element_type=F32)
        zr.append(z[:, :lb])
        zi.append(z[:, lb:])
    a, mult, gated = _lru_terms(hx, jnp.concatenate(zr, axis=1), jnp.concatenate(zi, axis=1),
                                0.5 * br_ref[...], 0.5 * bi_ref[...],
                                _half_neg_c_softplus_neg(lam_ref[...]))
    h = slru_v[...]
    hs_slabs = []
    for i in range(ts):
        m = slab(mult, i)
        if PAST_LEN + i == 0:
            m = jnp.ones_like(m)
        h = slab(a, i) * h + m * slab(gated, i)
        hs_slabs.append(h)
    hs = jnp.concatenate(hs_slabs, axis=0)
    loads["awout"].wait()
    _cast_rows(awout_v, stage_out)
    loads["bwout"].start()
    stores["awout"].start()
    y0 = jnp.dot((hs * _silu_from_half(proj[:, d:])).astype(BF16), awout_v[...],
                 preferred_element_type=F32)
    h1 = x + y0
    lru_v[...] = h
    for k in range(CONV_W - 1):
        conv_v[k] = ext[ts + k]
    for name in ("bwin", "spool", "wgrp"):
        loads[name].wait()
    _cast_rows(bwin_v, stage_in, gate_half)
    for g in range(len(POOL_WINDOWS)):
        wgrp_v[g] = stage_g[g].astype(BF16)
    for name in ("conv", "lru", "pool_kept", "bwin", "wgrp"):
        stores[name].start()

    u1 = _rmsnorm(h1, bn_ref[...]).astype(BF16)
    proj1 = jnp.dot(u1, bwin_v[...], preferred_element_type=F32)
    xb1 = proj1[:, :d]
    ext2 = [spool_v[k] for k in range(POOL_PAST)] + [slab(xb1, i) for i in range(ts)]
    pooled_slabs = []
    for i in range(ts):
        means = []
        for gi, w in enumerate(POOL_WINDOWS):
            lanes = slice(gi * gw, (gi + 1) * gw)
            s = ext2[POOL_PAST + i][:, lanes]
            for j in range(1, w):
                s = s + ext2[POOL_PAST + i - j][:, lanes]
            means.append(s * (1.0 / min(PAST_LEN + i + 1, w)))
        pooled_slabs.append(jnp.concatenate(means, axis=1) - ext2[POOL_PAST + i])
    pooled = jnp.concatenate(pooled_slabs, axis=0).astype(BF16)
    z1 = jnp.concatenate(
        [jnp.dot(pooled[:, g * gw:(g + 1) * gw], wgrp_v[g], preferred_element_type=F32)
         for g in range(len(POOL_WINDOWS))], axis=1)
    z1 = (z1 + bgrp_ref[...]) * bscale_ref[...]
    loads["bwout"].wait()
    _cast_rows(bwout_v, stage_out)
    stores["bwout"].start()
    y1 = jnp.dot((z1 * _silu_from_half(proj1[:, d:])).astype(BF16), bwout_v[...],
                 preferred_element_type=F32)
    h2 = h1 + y1
    xb1_v[...] = xb1.reshape(ts, nb, d)
    y_v[...] = _rmsnorm(h2, fn_ref[...]).reshape(ts, nb, d)
    for name in ("pool_new", "y"):
        stores[name].start()
    for store in stores.values():
        store.wait()


def _whole(shape):
    return pl.BlockSpec(shape, lambda *_: (0,) * len(shape), pipeline_mode=pl.Buffered(1))


def kernel(x_prompt, x_sample, state_conv, state_lru, state_pool, a_norm, a_w_in, a_conv_w, a_conv_b, a_w_r, a_b_r, a_w_i, a_b_i, a_lam, a_w_out, b_norm, b_w_in, b_w_grp, b_b_grp, b_scale, b_w_out, final_norm):
    bp, tp, d = x_prompt.shape
    bs, ts, _ = x_sample.shape
    tb = PROMPT_BLOCK_ROWS
    assert a_norm.shape[0] == 1 and b_norm.shape[0] == 1, "one layer of each mixer type"
    assert d % MXU_COLS == 0 and d // N_LRU_BLOCKS == LANES
    assert tp % tb == 0 and tb % (SUBLANES * SUBLANES) == 0 and tb >= 2 * SUBLANES
    assert CONV_W - 1 <= ts <= POOL_PAST

    row = lambda v: v.reshape(1, d)
    n_slabs = d // LANES
    n_chunks = d // MXU_COLS
    lb = d // N_LRU_BLOCKS
    gw = d // len(POOL_WINDOWS)

    xs = jnp.swapaxes(x_sample, 0, 1)
    sconv = jnp.swapaxes(state_conv[0], 0, 1)
    spool = jnp.swapaxes(state_pool[0], 0, 1)
    hbm = pl.BlockSpec(memory_space=pl.ANY)
    act = lambda shape: pltpu.VMEM(shape, F32)
    matrices = (a_w_in, a_w_r, a_w_i, a_w_out, b_w_in, b_w_grp, b_w_out)
    vectors = (row(a_norm[0]), a_conv_w[0], row(a_conv_b[0]), row(a_b_r[0]), row(a_b_i[0]),
               row(a_lam[0]), row(b_norm[0]), row(b_b_grp[0]), row(b_scale[0]), row(final_norm))
    bf16_shapes = ((d, 2 * d), (N_LRU_BLOCKS, lb, 2 * lb), (d, d), (d, 2 * d),
                   (len(POOL_WINDOWS), gw, gw), (d, d))
    (y_s, conv_s, lru_s, pool_s,
     awin, wri, awout, bwin, wgrp, bwout) = pl.pallas_call(
        _sample_kernel,
        in_specs=[hbm] * (4 + len(matrices)) + [_whole(v.shape) for v in vectors],
        out_specs=[hbm] * (4 + len(bf16_shapes)),
        out_shape=[jax.ShapeDtypeStruct(xs.shape, F32),
                   jax.ShapeDtypeStruct(sconv.shape, F32),
                   jax.ShapeDtypeStruct((bs, d), F32),
                   jax.ShapeDtypeStruct(spool.shape, F32)]
        + [jax.ShapeDtypeStruct(shape, BF16) for shape in bf16_shapes],
        scratch_shapes=[act(xs.shape), act(sconv.shape), act((bs, d)), act(spool.shape),
                        act(xs.shape), act(sconv.shape), act((bs, d)), act(xs.shape)]
        + [pltpu.VMEM(shape, BF16) for shape in bf16_shapes]
        + [act((d, 2 * d)), act((d, d)), act((N_LRU_BLOCKS, lb, lb)), act((N_LRU_BLOCKS, lb, lb)),
           act((len(POOL_WINDOWS), gw, gw)),
           pltpu.SemaphoreType.DMA((11,)), pltpu.SemaphoreType.DMA((11,))],
        compiler_params=pltpu.CompilerParams(vmem_limit_bytes=VMEM_LIMIT_BYTES),
        name="sample_step",
    )(xs, sconv, state_lru[0], spool, *matrices, *vectors)

    weights = (row(a_norm[0]), awin, a_conv_w[0], row(a_conv_b[0]), wri, row(a_b_r[0]),
               row(a_b_i[0]), row(a_lam[0]), awout, row(b_norm[0]), bwin, wgrp, row(b_b_grp[0]),
               row(b_scale[0]), bwout, row(final_norm))
    weight_specs = [_whole(w.shape) for w in weights]
    seg_rows = SUBLANES * (tb // SUBLANES + 4)
    nt = tp // tb
    n_blocks = bp * nt
    blk0 = lambda n: jnp.minimum(n, n_blocks - 1)
    blk1 = lambda n: jnp.maximum(n - 1, 0)
    y_p, conv_p, lru_p, pool_p = pl.pallas_call(
        functools.partial(_prompt_kernel, blocks_per_seq=nt),
        grid=(n_blocks + 1,),
        in_specs=[pl.BlockSpec((1, tb, d), lambda n: (blk0(n) // nt, blk0(n) % nt, 0))] + weight_specs,
        out_specs=[pl.BlockSpec((1, tb, d), lambda n: (blk1(n) // nt, blk1(n) % nt, 0)),
                   pl.BlockSpec((1, CONV_W - 1, d), lambda n: (blk0(n) // nt, 0, 0)),
                   pl.BlockSpec((1, 1, d), lambda n: (blk0(n) // nt, 0, 0)),
                   pl.BlockSpec((1, POOL_PAST, d), lambda n: (blk1(n) // nt, 0, 0))],
        out_shape=[jax.ShapeDtypeStruct((bp, tp, d), F32),
                   jax.ShapeDtypeStruct((bp, CONV_W - 1, d), F32),
                   jax.ShapeDtypeStruct((bp, 1, d), F32),
                   jax.ShapeDtypeStruct((bp, POOL_PAST, d), F32)],
        scratch_shapes=[pltpu.VMEM((n_slabs, CONV_PAD + tb, LANES), F32),
                        pltpu.VMEM((n_slabs, seg_rows, LANES), F32),
                        pltpu.VMEM((n_slabs, seg_rows, LANES), F32),
                        pltpu.VMEM((n_slabs, seg_rows, LANES), F32),
                        pltpu.VMEM((1, d), F32),
                        pltpu.VMEM((2, tb, d), F32),
                        pltpu.VMEM((n_slabs, POOL_PAD + tb, LANES), F32),
                        pltpu.VMEM((2, MXU_COLS // LANES, POOL_PAD + tb, LANES), F32),
                        pltpu.VMEM((tb, d), BF16),
                        pltpu.VMEM((CONV_AHEAD, tb, LANES), F32),
                        pltpu.VMEM((CONV_AHEAD, tb, LANES), BF16),
                        pltpu.VMEM((2, tb, 2 * LANES), F32),
                        pltpu.VMEM((n_chunks, tb, MXU_COLS), F32),
                        pltpu.VMEM((tb, d), BF16),
                        pltpu.VMEM((tb, d), BF16),
                        pltpu.VMEM((tb, d), BF16),
                        pltpu.VMEM((tb, d), F32),
                        pltpu.VMEM((n_chunks, tb, MXU_COLS), F32),
                        pltpu.VMEM((tb, d), BF16)],
        compiler_params=pltpu.CompilerParams(
            vmem_limit_bytes=VMEM_LIMIT_BYTES,
            dimension_semantics=("arbitrary",)),
        name="prompt_step",
    )(x_prompt, *weights)

    return (y_p, jnp.swapaxes(y_s, 0, 1),
            conv_p[None], lru_p.reshape(1, bp, d), pool_p[None],
            jnp.swapaxes(conv_s, 0, 1)[None], lru_s[None], jnp.swapaxes(pool_s, 0, 1)[None])
```

```python
import functools

import jax
import jax.numpy as jnp
from jax import lax
from jax.experimental import pallas as pl
from jax.experimental.pallas import tpu as pltpu

PAST_LEN = 16384
N_LRU_BLOCKS = 8
CONV_W = 4
LRU_C = 8.0
POOL_WINDOWS = (2, 4, 8, 16)
POOL_PAST = max(POOL_WINDOWS) - 1
EPS = 1e-6

LANES = 128
SUBLANES = 8
MXU_COLS = 256
CONV_PAD = SUBLANES
POOL_LEVELS = max(POOL_WINDOWS).bit_length() - 1
POOL_PAD = POOL_LEVELS * SUBLANES
PROMPT_BLOCK_ROWS = 512
CONV_AHEAD = 4
VMEM_LIMIT_BYTES = 56 * 1024 * 1024

F32 = jnp.float32
BF16 = jnp.bfloat16


def _rmsnorm(x, g=None):
    y = x * lax.rsqrt(jnp.mean(x * x, axis=-1, keepdims=True) + EPS)
    return y if g is None else y * g


def _silu_from_half(hg):
    return hg + hg * jnp.tanh(hg)


def _half_neg_c_softplus_neg(lam):
    z = -lam
    return (-0.5 * LRU_C) * (jnp.maximum(z, 0.0) + jnp.log1p(jnp.exp(-jnp.abs(z))))


def _lru_terms(hx, zr, zi, hbr, hbi, hn):
    log_a = hn + hn * jnp.tanh(zr + hbr)
    a = jnp.exp(log_a)
    y = jnp.tanh(log_a) * (-1.0 - a * a)
    mult = jnp.where(y > 0.0, y * lax.rsqrt(y), 0.0)
    gated = hx + hx * jnp.tanh(zi + hbi)
    return a, mult, gated


def _lane_slab(ref, c):
    return ref[:, c * LANES:(c + 1) * LANES]


def _cols(q):
    return slice(q * MXU_COLS, (q + 1) * MXU_COLS)


def _prompt_stage_order(n_chunks, slabs_per_chunk):
    assert (n_chunks, slabs_per_chunk) == (4, 2)
    return ("aN aX0 bN aX1 aC0 bX0 aC1 aC2 aC3 aR0 aR1 aK0 aX2 aS0 aK1 bX1 aS1 bP0 aR2 aC4 aR3 bM0 aC5 aX3 "
            "aK2 aS2 bX2 aK3 aS3 aR4 bP1 aR5 bM1 aC6 aG0 aC7 aK4 bX3 aS4 aK5 aR6 aR7 aS5 bP2 bM2 aK6 aG1 aS6 "
            "aK7 bG0 aS7 bP3 bM3 aH0 aG2 aH1 bZ0 bG1 aH2 aG3 bZ1 bG2 aH3 bG3 bZ2 bZ3 bO0 bO1 bO2 bO3 aO0 bF "
            "aO1 aO2 aO3").split()


def _prompt_kernel(x_ref, awin_ref, cw_ref, cb_ref, wri_ref, br_ref, bi_ref, lam_ref,
                   awout_ref, bwin_ref, wgrp_ref, bgrp_ref, bscale_ref, bwout_ref, fn_ref,
                   y_ref, conv_ref, lru_ref, pool_ref,
                   ext_s, sa_s, sb_s, sh_s, h_s, h1_s, px_s, pt_s,
                   ua_s, hx_s, hxb_s, zr_s, ga_s, hsg_s, ub_s, pooled_s, z_s, gq_s, zg_s, *,
                   blocks_per_seq):
    n = pl.program_id(0)
    n_blocks = pl.num_programs(0) - 1
    ta = jnp.minimum(n, n_blocks - 1) % blocks_per_seq
    tbk = jnp.maximum(n - 1, 0) % blocks_per_seq
    tb, d = x_ref.shape[1], x_ref.shape[2]
    n_slabs = d // LANES
    n_chunks = d // MXU_COLS
    slabs_per_chunk = MXU_COLS // LANES
    assert d // len(POOL_WINDOWS) == MXU_COLS, "one pooling group per matmul column chunk"
    seg = tb // SUBLANES
    pitch = seg + 4
    conv_ahead = hx_s.shape[0]

    @pl.when(n == 0)
    def _():
        h1_s[...] = jnp.zeros_like(h1_s)

    @pl.when(ta == 0)
    def _():
        ext_s[:, 0:CONV_PAD, :] = jnp.zeros((n_slabs, CONV_PAD, LANES), F32)
        h_s[...] = jnp.zeros_like(h_s)

    @pl.when(tbk == 0)
    def _():
        px_s[:, 0:POOL_PAD, :] = jnp.zeros((n_slabs, POOL_PAD, LANES), F32)

    h1_in = h1_s.at[(n + 1) % 2]
    h1_out = h1_s.at[n % 2]
    sub = lax.broadcasted_iota(jnp.int32, (SUBLANES, 1), 0)
    first_token = jnp.logical_and(ta == 0, sub == 0)
    row16 = lax.broadcasted_iota(jnp.int32, (2 * SUBLANES, 1), 0)
    pos16 = tbk * tb + row16
    hn_row = _half_neg_c_softplus_neg(lam_ref[...])
    hcw, hcb = 0.5 * cw_ref[...], 0.5 * cb_ref[...]
    hbr, hbi = 0.5 * br_ref[...], 0.5 * bi_ref[...]
    carries = [None] * n_slabs
    stages = {}

    def a_norm():
        ua_s[...] = _rmsnorm(x_ref[0]).astype(BF16)

    def a_xb_chunk(q):
        pq = jnp.dot(ua_s[...], awin_ref[:, _cols(q)], preferred_element_type=F32)
        for i in range(slabs_per_chunk):
            ext_s[q * slabs_per_chunk + i, CONV_PAD:CONV_PAD + tb, :] = pq[:, i * LANES:(i + 1) * LANES]

    def a_gate_chunk(q):
        ga_s[q] = jnp.dot(ua_s[...], awin_ref[:, _cols(n_chunks + q)], preferred_element_type=F32)

    def a_conv(c):
        lanes = slice(c * LANES, (c + 1) * LANES)
        for j in range(SUBLANES):
            r0 = CONV_PAD + j * seg
            hx = ext_s[c, r0:r0 + seg, :] * hcw[CONV_W - 1:CONV_W, lanes] + hcb[:, lanes]
            for k in range(CONV_W - 1):
                hx = hx + ext_s[c, pl.ds(r0 - (CONV_W - 1) + k, seg), :] * hcw[k:k + 1, lanes]
            hx_s[c % conv_ahead, j * seg:(j + 1) * seg, :] = hx
            hxb_s[c % conv_ahead, j * seg:(j + 1) * seg, :] = hx.astype(BF16)

    def a_gate_proj(c):
        zr_s[c % 2] = jnp.dot(hxb_s[c % conv_ahead], wri_ref[c], preferred_element_type=F32)

    def a_coeffs(c):
        lanes = slice(c * LANES, (c + 1) * LANES)
        for j in range(SUBLANES):
            rows = slice(j * seg, (j + 1) * seg)
            a, mult, gated = _lru_terms(hx_s[c % conv_ahead, rows, :], zr_s[c % 2, rows, :LANES],
                                        zr_s[c % 2, rows, LANES:], hbr[:, lanes], hbi[:, lanes],
                                        hn_row[:, lanes])
            if j == 0:
                mult = jnp.concatenate(
                    [jnp.where(first_token, 1.0, mult[0:SUBLANES]), mult[SUBLANES:]], axis=0)
            sa_s[c, pl.ds(j * pitch, seg), :] = a
            sb_s[c, pl.ds(j * pitch, seg), :] = mult * gated

    def a_scan(c):
        hl = sb_s[c, pl.ds(0, SUBLANES, stride=pitch), :]
        pp = sa_s[c, pl.ds(0, SUBLANES, stride=pitch), :]
        for i in range(1, seg):
            av = sa_s[c, pl.ds(i, SUBLANES, stride=pitch), :]
            hl = av * hl + sb_s[c, pl.ds(i, SUBLANES, stride=pitch), :]
            pp = av * pp
        h_in = jnp.broadcast_to(_lane_slab(h_s, c), (SUBLANES, LANES))
        cin = h_in
        for _ in range(SUBLANES - 1):
            cin = jnp.where(sub == 0, h_in, pltpu.roll(hl + pp * cin, 1, 0))
        carries[c] = (hl + pp * cin)[SUBLANES - 1:SUBLANES]
        h = cin
        for i in range(seg):
            h = sa_s[c, pl.ds(i, SUBLANES, stride=pitch), :] * h + sb_s[c, pl.ds(i, SUBLANES, stride=pitch), :]
            sh_s[c, pl.ds(i, SUBLANES, stride=pitch), :] = h

    def a_gated_out(q):
        hs = jnp.concatenate(
            [jnp.concatenate([sh_s[c, pl.ds(j * pitch, seg), :] for j in range(SUBLANES)], axis=0)
             for c in range(q * slabs_per_chunk, (q + 1) * slabs_per_chunk)], axis=1)
        hsg_s[:, _cols(q)] = (hs * _silu_from_half(ga_s[q])).astype(BF16)

    def a_out_proj(q):
        y0 = jnp.dot(hsg_s[...], awout_ref[:, _cols(q)], preferred_element_type=F32)
        h1_out[:, _cols(q)] = x_ref[0, :, _cols(q)] + y0

    def b_norm():
        ub_s[...] = _rmsnorm(h1_in[...]).astype(BF16)

    def b_xb_chunk(q):
        pq = jnp.dot(ub_s[...], bwin_ref[:, _cols(q)], preferred_element_type=F32)
        for i in range(slabs_per_chunk):
            px_s[q * slabs_per_chunk + i, POOL_PAD:POOL_PAD + tb, :] = pq[:, i * LANES:(i + 1) * LANES]

    def pool_slab(c, w):
        levels = w.bit_length() - 1
        tmp = c % slabs_per_chunk
        src = px_s.at[c]
        s = None
        for k in range(levels):
            back = (levels - 1 - k) * SUBLANES
            rows = tb + back
            s = (src[pl.ds(POOL_PAD - back, rows), :]
                 + src[pl.ds(POOL_PAD - back - (1 << k), rows), :])
            if k < levels - 1:
                pt_s[k % 2, tmp, POOL_PAD - back:POOL_PAD + tb, :] = s
                src = pt_s.at[k % 2, tmp]
        inv_cnt = 1.0 / jnp.minimum(pos16 + 1, w).astype(F32)
        mean = jnp.concatenate([s[0:2 * SUBLANES] * inv_cnt, s[2 * SUBLANES:] * (1.0 / w)], axis=0)
        return mean - px_s[c, POOL_PAD:POOL_PAD + tb, :]

    def b_pool(q):
        pooled_s[:, _cols(q)] = jnp.concatenate(
            [pool_slab(q * slabs_per_chunk + i, POOL_WINDOWS[q]) for i in range(slabs_per_chunk)],
            axis=1).astype(BF16)

    def b_group_proj(q):
        z_s[:, _cols(q)] = jnp.dot(pooled_s[:, _cols(q)], wgrp_ref[q], preferred_element_type=F32)

    def b_gate_chunk(q):
        gq_s[q] = jnp.dot(ub_s[...], bwin_ref[:, _cols(n_chunks + q)], preferred_element_type=F32)

    def b_gated_proj(q):
        z1 = (z_s[:, _cols(q)] + bgrp_ref[:, _cols(q)]) * bscale_ref[:, _cols(q)]
        zg_s[:, _cols(q)] = (z1 * _silu_from_half(gq_s[q])).astype(BF16)

    def b_out_proj(q):
        y1 = jnp.dot(zg_s[...], bwout_ref[:, _cols(q)], preferred_element_type=F32)
        y_ref[0, :, _cols(q)] = h1_in[:, _cols(q)] + y1

    def b_final_norm():
        y_ref[0] = _rmsnorm(y_ref[0], fn_ref[...])

    stages["aN"] = a_norm
    stages["bN"] = b_norm
    stages["bF"] = b_final_norm
    for q in range(n_chunks):
        for name, fn in (("aX", a_xb_chunk), ("aG", a_gate_chunk), ("aH", a_gated_out),
                         ("aO", a_out_proj), ("bX", b_xb_chunk), ("bP", b_pool),
                         ("bM", b_group_proj), ("bG", b_gate_chunk), ("bZ", b_gated_proj),
                         ("bO", b_out_proj)):
            stages[f"{name}{q}"] = functools.partial(fn, q)
    for c in range(n_slabs):
        for name, fn in (("aC", a_conv), ("aR", a_gate_proj), ("aK", a_coeffs), ("aS", a_scan)):
            stages[f"{name}{c}"] = functools.partial(fn, c)

    order = _prompt_stage_order(n_chunks, slabs_per_chunk)
    assert sorted(order) == sorted(stages), "every stage is issued exactly once"
    at = order.index
    for c in range(n_slabs):
        assert c < conv_ahead or at(f"aK{c - conv_ahead}") < at(f"aC{c}")
        assert c < zr_s.shape[0] or at(f"aK{c - zr_s.shape[0]}") < at(f"aR{c}")
    for name in order:
        stages[name]()
    h_s[...] = jnp.concatenate(carries, axis=1)

    @pl.when(jnp.logical_and(ta == blocks_per_seq - 1, n < n_blocks))
    def _():
        for c in range(n_slabs):
            conv_ref[0, :, c * LANES:(c + 1) * LANES] = ext_s[c, tb + CONV_PAD - (CONV_W - 1):tb + CONV_PAD, :]
        lru_ref[0] = h_s[...]

    @pl.when(jnp.logical_and(tbk == blocks_per_seq - 1, n > 0))
    def _():
        for c in range(n_slabs):
            pool_ref[0, :, c * LANES:(c + 1) * LANES] = px_s[c, tb + POOL_PAD - POOL_PAST:tb + POOL_PAD, :]

    ext_s[:, 0:CONV_PAD, :] = ext_s[:, tb:tb + CONV_PAD, :]
    px_s[:, 0:POOL_PAD, :] = px_s[:, tb:tb + POOL_PAD, :]


def _cast_rows(dst, src, col_scale=None, row_scale=None, rows_per_step=128):
    for r in range(0, src.shape[0], rows_per_step):
        v = src[r:r + rows_per_step, :]
        if col_scale is not None:
            v = v * col_scale
        if row_scale is not None:
            v = v * row_scale[r:r + rows_per_step, :]
        dst[r:r + rows_per_step, :] = v.astype(dst.dtype)


def _sample_kernel(x_hbm, sconv_hbm, slru_hbm, spool_hbm,
                   awin_f, wr_f, wi_f, awout_f, bwin_f, wgrp_f, bwout_f,
                   an_ref, cw_ref, cb_ref, br_ref, bi_ref, lam_ref, bn_ref, bgrp_ref, bscale_ref, fn_ref,
                   y_hbm, conv_hbm, lru_hbm, pool_hbm,
                   awin_o, wri_o, awout_o, bwin_o, wgrp_o, bwout_o,
                   x_v, sconv_v, slru_v, spool_v, y_v, conv_v, lru_v, xb1_v,
                   awin_v, wri_v, awout_v, bwin_v, wgrp_v, bwout_v,
                   stage_in, stage_out, stage_r, stage_i, stage_g, load_sem, store_sem):
    ts, nb, d = x_v.shape
    lb = d // N_LRU_BLOCKS
    gw = d // len(POOL_WINDOWS)
    kept = POOL_PAST - ts

    sources = (("x", x_hbm, x_v), ("awin", awin_f.at[0], stage_in), ("sconv", sconv_hbm, sconv_v),
               ("wr", wr_f.at[0], stage_r), ("wi", wi_f.at[0], stage_i), ("slru", slru_hbm, slru_v),
               ("awout", awout_f.at[0], stage_out), ("spool", spool_hbm, spool_v),
               ("wgrp", wgrp_f.at[0], stage_g),
               ("bwin", bwin_f.at[0], stage_in), ("bwout", bwout_f.at[0], stage_out))
    loads = {name: pltpu.make_async_copy(src, dst, load_sem.at[i])
             for i, (name, src, dst) in enumerate(sources)}
    for name in ("x", "awin", "sconv", "wr", "wi", "slru", "awout", "spool", "wgrp"):
        loads[name].start()
    sinks = (("conv", conv_v, conv_hbm), ("lru", lru_v, lru_hbm),
             ("pool_kept", spool_v.at[pl.ds(ts, kept)], pool_hbm.at[pl.ds(0, kept)]),
             ("pool_new", xb1_v, pool_hbm.at[pl.ds(kept, ts)]), ("y", y_v, y_hbm),
             ("awin", awin_v, awin_o), ("wri", wri_v, wri_o), ("awout", awout_v, awout_o),
             ("bwin", bwin_v, bwin_o), ("wgrp", wgrp_v, wgrp_o), ("bwout", bwout_v, bwout_o))
    stores = {name: pltpu.make_async_copy(src, dst, store_sem.at[i])
              for i, (name, src, dst) in enumerate(sinks)}
    gate_half = jnp.where(lax.broadcasted_iota(jnp.int32, (1, 2 * d), 1) < d, 1.0, 0.5)

    def slab(v, i):
        return v[i * nb:(i + 1) * nb]

    loads["x"].wait()
    x = x_v[...].reshape(ts * nb, d)
    u = _rmsnorm(x).astype(BF16)
    loads["awin"].wait()
    _cast_rows(awin_v, stage_in, gate_half, an_ref[...])
    loads["bwin"].start()
    stores["awin"].start()
    proj = jnp.dot(u, awin_v[...], preferred_element_type=F32)
    xb = proj[:, :d]
    for name in ("sconv", "wr", "wi", "slru"):
        loads[name].wait()
    for n in range(N_LRU_BLOCKS):
        wri_v[n, :, :lb] = stage_r[n].astype(BF16)
        wri_v[n, :, lb:] = stage_i[n].astype(BF16)
    stores["wri"].start()
    ext = [sconv_v[k] for k in range(CONV_W - 1)] + [slab(xb, i) for i in range(ts)]
    hcw, hcb = 0.5 * cw_ref[...], 0.5 * cb_ref[...]
    hx_slabs = []
    for i in range(ts):
        acc = hcb
        for k in range(CONV_W):
            acc = acc + ext[i + k] * hcw[k:k + 1, :]
        hx_slabs.append(acc)
    hx = jnp.concatenate(hx_slabs, axis=0)
    hx_bf16 = hx.astype(BF16)
    zr, zi = [], []
    for n in range(N_LRU_BLOCKS):
        z = jnp.dot(hx_bf16[:, n * lb:(n + 1) * lb], wri_v[n], preferred_element_type=F32)
        zr.append(z[:, :lb])
        zi.append(z[:, lb:])
    a, mult, gated = _lru_terms(hx, jnp.concatenate(zr, axis=1), jnp.concatenate(zi, axis=1),
                                0.5 * br_ref[...], 0.5 * bi_ref[...],
                                _half_neg_c_softplus_neg(lam_ref[...]))
    h = slru_v[...]
    hs_slabs = []
    for i in range(ts):
        m = slab(mult, i)
        if PAST_LEN + i == 0:
            m = jnp.ones_like(m)
        h = slab(a, i) * h + m * slab(gated, i)
        hs_slabs.append(h)
    hs = jnp.concatenate(hs_slabs, axis=0)
    loads["awout"].wait()
    _cast_rows(awout_v, stage_out)
    loads["bwout"].start()
    stores["awout"].start()
    y0 = jnp.dot((hs * _silu_from_half(proj[:, d:])).astype(BF16), awout_v[...],
                 preferred_element_type=F32)
    h1 = x + y0
    lru_v[...] = h
    for k in range(CONV_W - 1):
        conv_v[k] = ext[ts + k]
    for name in ("bwin", "spool", "wgrp"):
        loads[name].wait()
    _cast_rows(bwin_v, stage_in, gate_half, bn_ref[...])
    for g in range(len(POOL_WINDOWS)):
        wgrp_v[g] = stage_g[g].astype(BF16)
    for name in ("conv", "lru", "pool_kept", "bwin", "wgrp"):
        stores[name].start()

    u1 = _rmsnorm(h1).astype(BF16)
    proj1 = jnp.dot(u1, bwin_v[...], preferred_element_type=F32)
    xb1 = proj1[:, :d]
    ext2 = [spool_v[k] for k in range(POOL_PAST)] + [slab(xb1, i) for i in range(ts)]
    pooled_slabs = []
    for i in range(ts):
        means = []
        for gi, w in enumerate(POOL_WINDOWS):
            lanes = slice(gi * gw, (gi + 1) * gw)
            s = ext2[POOL_PAST + i][:, lanes]
            for j in range(1, w):
                s = s + ext2[POOL_PAST + i - j][:, lanes]
            means.append(s * (1.0 / min(PAST_LEN + i + 1, w)))
        pooled_slabs.append(jnp.concatenate(means, axis=1) - ext2[POOL_PAST + i])
    pooled = jnp.concatenate(pooled_slabs, axis=0).astype(BF16)
    z1 = jnp.concatenate(
        [jnp.dot(pooled[:, g * gw:(g + 1) * gw], wgrp_v[g], preferred_element_type=F32)
         for g in range(len(POOL_WINDOWS))], axis=1)
    z1 = (z1 + bgrp_ref[...]) * bscale_ref[...]
    loads["bwout"].wait()
    _cast_rows(bwout_v, stage_out)
    stores["bwout"].start()
    y1 = jnp.dot((z1 * _silu_from_half(proj1[:, d:])).astype(BF16), bwout_v[...],
                 preferred_element_type=F32)
    h2 = h1 + y1
    xb1_v[...] = xb1.reshape(ts, nb, d)
    y_v[...] = _rmsnorm(h2, fn_ref[...]).reshape(ts, nb, d)
    for name in ("pool_new", "y"):
        stores[name].start()
    for store in stores.values():
        store.wait()


def _whole(shape):
    return pl.BlockSpec(shape, lambda *_: (0,) * len(shape), pipeline_mode=pl.Buffered(1))


def kernel(x_prompt, x_sample, state_conv, state_lru, state_pool, a_norm, a_w_in, a_conv_w, a_conv_b, a_w_r, a_b_r, a_w_i, a_b_i, a_lam, a_w_out, b_norm, b_w_in, b_w_grp, b_b_grp, b_scale, b_w_out, final_norm):
    bp, tp, d = x_prompt.shape
    bs, ts, _ = x_sample.shape
    tb = PROMPT_BLOCK_ROWS
    assert a_norm.shape[0] == 1 and b_norm.shape[0] == 1, "one layer of each mixer type"
    assert d % MXU_COLS == 0 and d // N_LRU_BLOCKS == LANES
    assert tp % tb == 0 and tb % (SUBLANES * SUBLANES) == 0 and tb >= 2 * SUBLANES
    assert CONV_W - 1 <= ts <= POOL_PAST

    row = lambda v: v.reshape(1, d)
    n_slabs = d // LANES
    n_chunks = d // MXU_COLS
    lb = d // N_LRU_BLOCKS
    gw = d // len(POOL_WINDOWS)

    xs = jnp.swapaxes(x_sample, 0, 1)
    sconv = jnp.swapaxes(state_conv[0], 0, 1)
    spool = jnp.swapaxes(state_pool[0], 0, 1)
    hbm = pl.BlockSpec(memory_space=pl.ANY)
    act = lambda shape: pltpu.VMEM(shape, F32)
    matrices = (a_w_in, a_w_r, a_w_i, a_w_out, b_w_in, b_w_grp, b_w_out)
    col = lambda v: v.reshape(d, 1)
    vectors = (col(a_norm[0]), a_conv_w[0], row(a_conv_b[0]), row(a_b_r[0]), row(a_b_i[0]),
               row(a_lam[0]), col(b_norm[0]), row(b_b_grp[0]), row(b_scale[0]), row(final_norm))
    bf16_shapes = ((d, 2 * d), (N_LRU_BLOCKS, lb, 2 * lb), (d, d), (d, 2 * d),
                   (len(POOL_WINDOWS), gw, gw), (d, d))
    (y_s, conv_s, lru_s, pool_s,
     awin, wri, awout, bwin, wgrp, bwout) = pl.pallas_call(
        _sample_kernel,
        in_specs=[hbm] * (4 + len(matrices)) + [_whole(v.shape) for v in vectors],
        out_specs=[hbm] * (4 + len(bf16_shapes)),
        out_shape=[jax.ShapeDtypeStruct(xs.shape, F32),
                   jax.ShapeDtypeStruct(sconv.shape, F32),
                   jax.ShapeDtypeStruct((bs, d), F32),
                   jax.ShapeDtypeStruct(spool.shape, F32)]
        + [jax.ShapeDtypeStruct(shape, BF16) for shape in bf16_shapes],
        scratch_shapes=[act(xs.shape), act(sconv.shape), act((bs, d)), act(spool.shape),
                        act(xs.shape), act(sconv.shape), act((bs, d)), act(xs.shape)]
        + [pltpu.VMEM(shape, BF16) for shape in bf16_shapes]
        + [act((d, 2 * d)), act((d, d)), act((N_LRU_BLOCKS, lb, lb)), act((N_LRU_BLOCKS, lb, lb)),
           act((len(POOL_WINDOWS), gw, gw)),
           pltpu.SemaphoreType.DMA((11,)), pltpu.SemaphoreType.DMA((11,))],
        compiler_params=pltpu.CompilerParams(vmem_limit_bytes=VMEM_LIMIT_BYTES),
        name="sample_step",
    )(xs, sconv, state_lru[0], spool, *matrices, *vectors)

    weights = (awin, a_conv_w[0], row(a_conv_b[0]), wri, row(a_b_r[0]), row(a_b_i[0]),
               row(a_lam[0]), awout, bwin, wgrp, row(b_b_grp[0]), row(b_scale[0]), bwout,
               row(final_norm))
    weight_specs = [_whole(w.shape) for w in weights]
    seg_rows = SUBLANES * (tb // SUBLANES + 4)
    nt = tp // tb
    n_blocks = bp * nt
    blk0 = lambda n: jnp.minimum(n, n_blocks - 1)
    blk1 = lambda n: jnp.maximum(n - 1, 0)
    y_p, conv_p, lru_p, pool_p = pl.pallas_call(
        functools.partial(_prompt_kernel, blocks_per_seq=nt),
        grid=(n_blocks + 1,),
        in_specs=[pl.BlockSpec((1, tb, d), lambda n: (blk0(n) // nt, blk0(n) % nt, 0))] + weight_specs,
        out_specs=[pl.BlockSpec((1, tb, d), lambda n: (blk1(n) // nt, blk1(n) % nt, 0)),
                   pl.BlockSpec((1, CONV_W - 1, d), lambda n: (blk0(n) // nt, 0, 0)),
                   pl.BlockSpec((1, 1, d), lambda n: (blk0(n) // nt, 0, 0)),
                   pl.BlockSpec((1, POOL_PAST, d), lambda n: (blk1(n) // nt, 0, 0))],
        out_shape=[jax.ShapeDtypeStruct((bp, tp, d), F32),
                   jax.ShapeDtypeStruct((bp, CONV_W - 1, d), F32),
                   jax.ShapeDtypeStruct((bp, 1, d), F32),
                   jax.ShapeDtypeStruct((bp, POOL_PAST, d), F32)],
        scratch_shapes=[pltpu.VMEM((n_slabs, CONV_PAD + tb, LANES), F32),
                        pltpu.VMEM((n_slabs, seg_rows, LANES), F32),
                        pltpu.VMEM((n_slabs, seg_rows, LANES), F32),
                        pltpu.VMEM((n_slabs, seg_rows, LANES), F32),
                        pltpu.VMEM((1, d), F32),
                        pltpu.VMEM((2, tb, d), F32),
                        pltpu.VMEM((n_slabs, POOL_PAD + tb, LANES), F32),
                        pltpu.VMEM((2, MXU_COLS // LANES, POOL_PAD + tb, LANES), F32),
                        pltpu.VMEM((tb, d), BF16),
                        pltpu.VMEM((CONV_AHEAD, tb, LANES), F32),
                        pltpu.VMEM((CONV_AHEAD, tb, LANES), BF16),
                        pltpu.VMEM((2, tb, 2 * LANES), F32),
                        pltpu.VMEM((n_chunks, tb, MXU_COLS), F32),
                        pltpu.VMEM((tb, d), BF16),
                        pltpu.VMEM((tb, d), BF16),
                        pltpu.VMEM((tb, d), BF16),
                        pltpu.VMEM((tb, d), F32),
                        pltpu.VMEM((n_chunks, tb, MXU_COLS), F32),
                        pltpu.VMEM((tb, d), BF16)],
        compiler_params=pltpu.CompilerParams(
            vmem_limit_bytes=VMEM_LIMIT_BYTES,
            dimension_semantics=("arbitrary",)),
        name="prompt_step",
    )(x_prompt, *weights)

    return (y_p, jnp.swapaxes(y_s, 0, 1),
            conv_p[None], lru_p.reshape(1, bp, d), pool_p[None],
            jnp.swapaxes(conv_s, 0, 1)[None], lru_s[None], jnp.swapaxes(pool_s, 0, 1)[None])
```

```python
import functools

import jax
import jax.numpy as jnp
from jax import lax
from jax.experimental import pallas as pl
from jax.experimental.pallas import tpu as pltpu

PAST_LEN = 16384
N_LRU_BLOCKS = 8
CONV_W = 4
LRU_C = 8.0
POOL_WINDOWS = (2, 4, 8, 16)
POOL_PAST = max(POOL_WINDOWS) - 1
EPS = 1e-6

LANES = 128
SUBLANES = 8
MXU_COLS = 256
CONV_PAD = SUBLANES
POOL_LEVELS = max(POOL_WINDOWS).bit_length() - 1
POOL_PAD = POOL_LEVELS * SUBLANES
PROMPT_BLOCK_ROWS = 512
CONV_AHEAD = 4
VMEM_LIMIT_BYTES = 56 * 1024 * 1024

F32 = jnp.float32
BF16 = jnp.bfloat16


def _rmsnorm(x, g):
    return x * lax.rsqrt(jnp.mean(x * x, axis=-1, keepdims=True) + EPS) * g


def _silu_from_half(hg):
    return hg + hg * jnp.tanh(hg)


def _half_neg_c_softplus_neg(lam):
    z = -lam
    return (-0.5 * LRU_C) * (jnp.maximum(z, 0.0) + jnp.log1p(jnp.exp(-jnp.abs(z))))


def _lru_terms(hx, zr, zi, hbr, hbi, hn):
    log_a = hn + hn * jnp.tanh(zr + hbr)
    a = jnp.exp(log_a)
    y = jnp.tanh(log_a) * (-1.0 - a * a)
    mult = jnp.where(y > 0.0, y * lax.rsqrt(y), 0.0)
    gated = hx + hx * jnp.tanh(zi + hbi)
    return a, mult, gated


def _lane_slab(ref, c):
    return ref[:, c * LANES:(c + 1) * LANES]


def _cols(q):
    return slice(q * MXU_COLS, (q + 1) * MXU_COLS)


def _prompt_stage_order(n_chunks, slabs_per_chunk):
    assert (n_chunks, slabs_per_chunk) == (4, 2)
    return ("aN aX0 bN aX1 aC0 bX0 aC1 aC2 aC3 aR0 aR1 aK0 aX2 aS0 aK1 bX1 aS1 bP0 aR2 aC4 aR3 bM0 aC5 aX3 "
            "aK2 aS2 bX2 aK3 aS3 aR4 bP1 aR5 bM1 aC6 aG0 aC7 aK4 bX3 aS4 aK5 aR6 aR7 aS5 bP2 bM2 aK6 aG1 aS6 "
            "aK7 bG0 aS7 bP3 bM3 aH0 aG2 aH1 bZ0 bG1 aH2 aG3 bZ1 bG2 aH3 bG3 bZ2 bZ3 bO0 bO1 bO2 bO3 aO0 bF "
            "aO1 aO2 aO3").split()


def _prompt_kernel(x_ref, an_ref, awin_ref, cw_ref, cb_ref, wri_ref, br_ref, bi_ref, lam_ref,
                   awout_ref, bn_ref, bwin_ref, wgrp_ref, bgrp_ref, bscale_ref, bwout_ref, fn_ref,
                   y_ref, conv_ref, lru_ref, pool_ref,
                   ext_s, sa_s, sb_s, sh_s, h_s, h1_s, px_s, pt_s,
                   ua_s, hx_s, hxb_s, zr_s, ga_s, hsg_s, ub_s, pooled_s, z_s, gq_s, zg_s, *,
                   blocks_per_seq):
    n = pl.program_id(0)
    n_blocks = pl.num_programs(0) - 1
    ta = jnp.minimum(n, n_blocks - 1) % blocks_per_seq
    tbk = jnp.maximum(n - 1, 0) % blocks_per_seq
    tb, d = x_ref.shape[1], x_ref.shape[2]
    n_slabs = d // LANES
    n_chunks = d // MXU_COLS
    slabs_per_chunk = MXU_COLS // LANES
    assert d // len(POOL_WINDOWS) == MXU_COLS, "one pooling group per matmul column chunk"
    seg = tb // SUBLANES
    pitch = seg + 4
    conv_ahead = hx_s.shape[0]

    @pl.when(n == 0)
    def _():
        h1_s[...] = jnp.zeros_like(h1_s)

    @pl.when(ta == 0)
    def _():
        ext_s[:, 0:CONV_PAD, :] = jnp.zeros((n_slabs, CONV_PAD, LANES), F32)
        h_s[...] = jnp.zeros_like(h_s)

    @pl.when(tbk == 0)
    def _():
        px_s[:, 0:POOL_PAD, :] = jnp.zeros((n_slabs, POOL_PAD, LANES), F32)

    h1_in = h1_s.at[(n + 1) % 2]
    h1_out = h1_s.at[n % 2]
    sub = lax.broadcasted_iota(jnp.int32, (SUBLANES, 1), 0)
    first_token = jnp.logical_and(ta == 0, sub == 0)
    row16 = lax.broadcasted_iota(jnp.int32, (2 * SUBLANES, 1), 0)
    pos16 = tbk * tb + row16
    hn_row = _half_neg_c_softplus_neg(lam_ref[...])
    hcw, hcb = 0.5 * cw_ref[...], 0.5 * cb_ref[...]
    hbr, hbi = 0.5 * br_ref[...], 0.5 * bi_ref[...]
    carries = [None] * n_slabs
    stages = {}

    def a_norm():
        ua_s[...] = _rmsnorm(x_ref[0], an_ref[...]).astype(BF16)

    def a_xb_chunk(q):
        pq = jnp.dot(ua_s[...], awin_ref[:, _cols(q)], preferred_element_type=F32)
        for i in range(slabs_per_chunk):
            ext_s[q * slabs_per_chunk + i, CONV_PAD:CONV_PAD + tb, :] = pq[:, i * LANES:(i + 1) * LANES]

    def a_gate_chunk(q):
        ga_s[q] = jnp.dot(ua_s[...], awin_ref[:, _cols(n_chunks + q)], preferred_element_type=F32)

    def a_conv(c):
        lanes = slice(c * LANES, (c + 1) * LANES)
        for j in range(SUBLANES):
            r0 = CONV_PAD + j * seg
            hx = ext_s[c, r0:r0 + seg, :] * hcw[CONV_W - 1:CONV_W, lanes] + hcb[:, lanes]
            for k in range(CONV_W - 1):
                hx = hx + ext_s[c, pl.ds(r0 - (CONV_W - 1) + k, seg), :] * hcw[k:k + 1, lanes]
            hx_s[c % conv_ahead, j * seg:(j + 1) * seg, :] = hx
            hxb_s[c % conv_ahead, j * seg:(j + 1) * seg, :] = hx.astype(BF16)

    def a_gate_proj(c):
        zr_s[c % 2] = jnp.dot(hxb_s[c % conv_ahead], wri_ref[c], preferred_element_type=F32)

    def a_coeffs(c):
        lanes = slice(c * LANES, (c + 1) * LANES)
        for j in range(SUBLANES):
            rows = slice(j * seg, (j + 1) * seg)
            a, mult, gated = _lru_terms(hx_s[c % conv_ahead, rows, :], zr_s[c % 2, rows, :LANES],
                                        zr_s[c % 2, rows, LANES:], hbr[:, lanes], hbi[:, lanes],
                                        hn_row[:, lanes])
            if j == 0:
                mult = jnp.concatenate(
                    [jnp.where(first_token, 1.0, mult[0:SUBLANES]), mult[SUBLANES:]], axis=0)
            sa_s[c, pl.ds(j * pitch, seg), :] = a
            sb_s[c, pl.ds(j * pitch, seg), :] = mult * gated

    def a_scan(c):
        hl = sb_s[c, pl.ds(0, SUBLANES, stride=pitch), :]
        pp = sa_s[c, pl.ds(0, SUBLANES, stride=pitch), :]
        for i in range(1, seg):
            av = sa_s[c, pl.ds(i, SUBLANES, stride=pitch), :]
            hl = av * hl + sb_s[c, pl.ds(i, SUBLANES, stride=pitch), :]
            pp = av * pp
        h_in = jnp.broadcast_to(_lane_slab(h_s, c), (SUBLANES, LANES))
        cin = h_in
        for _ in range(SUBLANES - 1):
            cin = jnp.where(sub == 0, h_in, pltpu.roll(hl + pp * cin, 1, 0))
        carries[c] = (hl + pp * cin)[SUBLANES - 1:SUBLANES]
        h = cin
        for i in range(seg):
            h = sa_s[c, pl.ds(i, SUBLANES, stride=pitch), :] * h + sb_s[c, pl.ds(i, SUBLANES, stride=pitch), :]
            sh_s[c, pl.ds(i, SUBLANES, stride=pitch), :] = h

    def a_gated_out(q):
        hs = jnp.concatenate(
            [jnp.concatenate([sh_s[c, pl.ds(j * pitch, seg), :] for j in range(SUBLANES)], axis=0)
             for c in range(q * slabs_per_chunk, (q + 1) * slabs_per_chunk)], axis=1)
        hsg_s[:, _cols(q)] = (hs * _silu_from_half(ga_s[q])).astype(BF16)

    def a_out_proj(q):
        y0 = jnp.dot(hsg_s[...], awout_ref[:, _cols(q)], preferred_element_type=F32)
        h1_out[:, _cols(q)] = x_ref[0, :, _cols(q)] + y0

    def b_norm():
        ub_s[...] = _rmsnorm(h1_in[...], bn_ref[...]).astype(BF16)

    def b_xb_chunk(q):
        pq = jnp.dot(ub_s[...], bwin_ref[:, _cols(q)], preferred_element_type=F32)
        for i in range(slabs_per_chunk):
            px_s[q * slabs_per_chunk + i, POOL_PAD:POOL_PAD + tb, :] = pq[:, i * LANES:(i + 1) * LANES]

    def pool_slab(c, w):
        levels = w.bit_length() - 1
        tmp = c % slabs_per_chunk
        src = px_s.at[c]
        s = None
        for k in range(levels):
            back = (levels - 1 - k) * SUBLANES
            rows = tb + back
            s = (src[pl.ds(POOL_PAD - back, rows), :]
                 + src[pl.ds(POOL_PAD - back - (1 << k), rows), :])
            if k < levels - 1:
                pt_s[k % 2, tmp, POOL_PAD - back:POOL_PAD + tb, :] = s
                src = pt_s.at[k % 2, tmp]
        inv_cnt = 1.0 / jnp.minimum(pos16 + 1, w).astype(F32)
        mean = jnp.concatenate([s[0:2 * SUBLANES] * inv_cnt, s[2 * SUBLANES:] * (1.0 / w)], axis=0)
        return mean - px_s[c, POOL_PAD:POOL_PAD + tb, :]

    def b_pool(q):
        pooled_s[:, _cols(q)] = jnp.concatenate(
            [pool_slab(q * slabs_per_chunk + i, POOL_WINDOWS[q]) for i in range(slabs_per_chunk)],
            axis=1).astype(BF16)

    def b_group_proj(q):
        z_s[:, _cols(q)] = jnp.dot(pooled_s[:, _cols(q)], wgrp_ref[q], preferred_element_type=F32)

    def b_gate_chunk(q):
        gq_s[q] = jnp.dot(ub_s[...], bwin_ref[:, _cols(n_chunks + q)], preferred_element_type=F32)

    def b_gated_proj(q):
        z1 = (z_s[:, _cols(q)] + bgrp_ref[:, _cols(q)]) * bscale_ref[:, _cols(q)]
        zg_s[:, _cols(q)] = (z1 * _silu_from_half(gq_s[q])).astype(BF16)

    def b_out_proj(q):
        y1 = jnp.dot(zg_s[...], bwout_ref[:, _cols(q)], preferred_element_type=F32)
        y_ref[0, :, _cols(q)] = h1_in[:, _cols(q)] + y1

    def b_final_norm():
        y_ref[0] = _rmsnorm(y_ref[0], fn_ref[...])

    stages["aN"] = a_norm
    stages["bN"] = b_norm
    stages["bF"] = b_final_norm
    for q in range(n_chunks):
        for name, fn in (("aX", a_xb_chunk), ("aG", a_gate_chunk), ("aH", a_gated_out),
                         ("aO", a_out_proj), ("bX", b_xb_chunk), ("bP", b_pool),
                         ("bM", b_group_proj), ("bG", b_gate_chunk), ("bZ", b_gated_proj),
                         ("bO", b_out_proj)):
            stages[f"{name}{q}"] = functools.partial(fn, q)
    for c in range(n_slabs):
        for name, fn in (("aC", a_conv), ("aR", a_gate_proj), ("aK", a_coeffs), ("aS", a_scan)):
            stages[f"{name}{c}"] = functools.partial(fn, c)

    order = _prompt_stage_order(n_chunks, slabs_per_chunk)
    assert sorted(order) == sorted(stages), "every stage is issued exactly once"
    at = order.index
    for c in range(n_slabs):
        assert c < conv_ahead or at(f"aK{c - conv_ahead}") < at(f"aC{c}")
        assert c < zr_s.shape[0] or at(f"aK{c - zr_s.shape[0]}") < at(f"aR{c}")
    for name in order:
        stages[name]()
    h_s[...] = jnp.concatenate(carries, axis=1)

    @pl.when(jnp.logical_and(ta == blocks_per_seq - 1, n < n_blocks))
    def _():
        for c in range(n_slabs):
            conv_ref[0, :, c * LANES:(c + 1) * LANES] = ext_s[c, tb + CONV_PAD - (CONV_W - 1):tb + CONV_PAD, :]
        lru_ref[0] = h_s[...]

    @pl.when(jnp.logical_and(tbk == blocks_per_seq - 1, n > 0))
    def _():
        for c in range(n_slabs):
            pool_ref[0, :, c * LANES:(c + 1) * LANES] = px_s[c, tb + POOL_PAD - POOL_PAST:tb + POOL_PAD, :]

    ext_s[:, 0:CONV_PAD, :] = ext_s[:, tb:tb + CONV_PAD, :]
    px_s[:, 0:POOL_PAD, :] = px_s[:, tb:tb + POOL_PAD, :]


def _cast_rows(dst, src, scale=None, rows_per_step=128):
    for r in range(0, src.shape[0], rows_per_step):
        v = src[r:r + rows_per_step, :]
        dst[r:r + rows_per_step, :] = (v if scale is None else v * scale).astype(dst.dtype)


def _sample_kernel(x_hbm, sconv_hbm, slru_hbm, spool_hbm,
                   awin_f, wr_f, wi_f, awout_f, bwin_f, wgrp_f, bwout_f,
                   an_ref, cw_ref, cb_ref, br_ref, bi_ref, lam_ref, bn_ref, bgrp_ref, bscale_ref, fn_ref,
                   y_hbm, conv_hbm, lru_hbm, pool_hbm,
                   awin_o, wri_o, awout_o, bwin_o, wgrp_o, bwout_o,
                   x_v, sconv_v, slru_v, spool_v, y_v, conv_v, lru_v, xb1_v,
                   awin_v, wri_v, awout_v, bwin_v, wgrp_v, bwout_v,
                   stage_in, stage_out, stage_r, stage_i, stage_g, load_sem, store_sem):
    ts, nb, d = x_v.shape
    lb = d // N_LRU_BLOCKS
    gw = d // len(POOL_WINDOWS)
    kept = POOL_PAST - ts

    sources = (("x", x_hbm, x_v), ("awin", awin_f.at[0], stage_in), ("sconv", sconv_hbm, sconv_v),
               ("wr", wr_f.at[0], stage_r), ("wi", wi_f.at[0], stage_i), ("slru", slru_hbm, slru_v),
               ("awout", awout_f.at[0], stage_out), ("spool", spool_hbm, spool_v),
               ("wgrp", wgrp_f.at[0], stage_g),
               ("bwin", bwin_f.at[0], stage_in), ("bwout", bwout_f.at[0], stage_out))
    loads = {name: pltpu.make_async_copy(src, dst, load_sem.at[i])
             for i, (name, src, dst) in enumerate(sources)}
    for name in ("x", "awin", "sconv", "wr", "wi", "slru", "awout", "spool", "wgrp"):
        loads[name].start()
    sinks = (("conv", conv_v, conv_hbm), ("lru", lru_v, lru_hbm),
             ("pool_kept", spool_v.at[pl.ds(ts, kept)], pool_hbm.at[pl.ds(0, kept)]),
             ("pool_new", xb1_v, pool_hbm.at[pl.ds(kept, ts)]), ("y", y_v, y_hbm),
             ("awin", awin_v, awin_o), ("wri", wri_v, wri_o), ("awout", awout_v, awout_o),
             ("bwin", bwin_v, bwin_o), ("wgrp", wgrp_v, wgrp_o), ("bwout", bwout_v, bwout_o))
    stores = {name: pltpu.make_async_copy(src, dst, store_sem.at[i])
              for i, (name, src, dst) in enumerate(sinks)}
    gate_half = jnp.where(lax.broadcasted_iota(jnp.int32, (1, 2 * d), 1) < d, 1.0, 0.5)

    def slab(v, i):
        return v[i * nb:(i + 1) * nb]

    loads["x"].wait()
    x = x_v[...].reshape(ts * nb, d)
    u = _rmsnorm(x, an_ref[...]).astype(BF16)
    loads["awin"].wait()
    _cast_rows(awin_v, stage_in, gate_half)
    loads["bwin"].start()
    proj = jnp.dot(u, awin_v[...], preferred_element_type=F32)
    xb = proj[:, :d]
    for name in ("sconv", "wr", "wi", "slru"):
        loads[name].wait()
    for n in range(N_LRU_BLOCKS):
        wri_v[n, :, :lb] = stage_r[n].astype(BF16)
        wri_v[n, :, lb:] = stage_i[n].astype(BF16)
    for name in ("awin", "wri"):
        stores[name].start()
    ext = [sconv_v[k] for k in range(CONV_W - 1)] + [slab(xb, i) for i in range(ts)]
    hcw, hcb = 0.5 * cw_ref[...], 0.5 * cb_ref[...]
    hx_slabs = []
    for i in range(ts):
        acc = hcb
        for k in range(CONV_W):
            acc = acc + ext[i + k] * hcw[k:k + 1, :]
        hx_slabs.append(acc)
    hx = jnp.concatenate(hx_slabs, axis=0)
    hx_bf16 = hx.astype(BF16)
    zr, zi = [], []
    for n in range(N_LRU_BLOCKS):
        z = jnp.dot(hx_bf16[:, n * lb:(n + 1) * lb], wri_v[n], preferred_element_type=F32)
        zr.append(z[:, :lb])
        zi.append(z[:, lb:])
    a, mult, gated = _lru_terms(hx, jnp.concatenate(zr, axis=1), jnp.concatenate(zi, axis=1),
                                0.5 * br_ref[...], 0.5 * bi_ref[...],
                                _half_neg_c_softplus_neg(lam_ref[...]))
    h = slru_v[...]
    hs_slabs = []
    for i in range(ts):
        m = slab(mult, i)
        if PAST_LEN + i == 0:
            m = jnp.ones_like(m)
        h = slab(a, i) * h + m * slab(gated, i)
        hs_slabs.append(h)
    hs = jnp.concatenate(hs_slabs, axis=0)
    loads["awout"].wait()
    _cast_rows(awout_v, stage_out)
    loads["bwout"].start()
    stores["awout"].start()
    y0 = jnp.dot((hs * _silu_from_half(proj[:, d:])).astype(BF16), awout_v[...],
                 preferred_element_type=F32)
    h1 = x + y0
    lru_v[...] = h
    for k in range(CONV_W - 1):
        conv_v[k] = ext[ts + k]
    for name in ("bwin", "spool", "wgrp"):
        loads[name].wait()
    _cast_rows(bwin_v, stage_in, gate_half)
    for g in range(len(POOL_WINDOWS)):
        wgrp_v[g] = stage_g[g].astype(BF16)
    for name in ("conv", "lru", "pool_kept", "bwin", "wgrp"):
        stores[name].start()

    u1 = _rmsnorm(h1, bn_ref[...]).astype(BF16)
    proj1 = jnp.dot(u1, bwin_v[...], preferred_element_type=F32)
    xb1 = proj1[:, :d]
    ext2 = [spool_v[k] for k in range(POOL_PAST)] + [slab(xb1, i) for i in range(ts)]
    pooled_slabs = []
    for i in range(ts):
        means = []
        for gi, w in enumerate(POOL_WINDOWS):
            lanes = slice(gi * gw, (gi + 1) * gw)
            s = ext2[POOL_PAST + i][:, lanes]
            for j in range(1, w):
                s = s + ext2[POOL_PAST + i - j][:, lanes]
            means.append(s * (1.0 / min(PAST_LEN + i + 1, w)))
        pooled_slabs.append(jnp.concatenate(means, axis=1) - ext2[POOL_PAST + i])
    pooled = jnp.concatenate(pooled_slabs, axis=0).astype(BF16)
    z1 = jnp.concatenate(
        [jnp.dot(pooled[:, g * gw:(g + 1) * gw], wgrp_v[g], preferred_element_type=F32)
         for g in range(len(POOL_WINDOWS))], axis=1)
    z1 = (z1 + bgrp_ref[...]) * bscale_ref[...]
    loads["bwout"].wait()
    _cast_rows(bwout_v, stage_out)
    stores["bwout"].start()
    y1 = jnp.dot((z1 * _silu_from_half(proj1[:, d:])).astype(BF16), bwout_v[...],
                 preferred_element_type=F32)
    h2 = h1 + y1
    xb1_v[...] = xb1.reshape(ts, nb, d)
    y_v[...] = _rmsnorm(h2, fn_ref[...]).reshape(ts, nb, d)
    for name in ("pool_new", "y"):
        stores[name].start()
    for store in stores.values():
        store.wait()


def _whole(shape):
    return pl.BlockSpec(shape, lambda *_: (0,) * len(shape), pipeline_mode=pl.Buffered(1))


def kernel(x_prompt, x_sample, state_conv, state_lru, state_pool, a_norm, a_w_in, a_conv_w, a_conv_b, a_w_r, a_b_r, a_w_i, a_b_i, a_lam, a_w_out, b_norm, b_w_in, b_w_grp, b_b_grp, b_scale, b_w_out, final_norm):
    bp, tp, d = x_prompt.shape
    bs, ts, _ = x_sample.shape
    tb = PROMPT_BLOCK_ROWS
    assert a_norm.shape[0] == 1 and b_norm.shape[0] == 1, "one layer of each mixer type"
    assert d % MXU_COLS == 0 and d // N_LRU_BLOCKS == LANES
    assert tp % tb == 0 and tb % (SUBLANES * SUBLANES) == 0 and tb >= 2 * SUBLANES
    assert CONV_W - 1 <= ts <= POOL_PAST

    row = lambda v: v.reshape(1, d)
    n_slabs = d // LANES
    n_chunks = d // MXU_COLS
    lb = d // N_LRU_BLOCKS
    gw = d // len(POOL_WINDOWS)

    xs = jnp.swapaxes(x_sample, 0, 1)
    sconv = jnp.swapaxes(state_conv[0], 0, 1)
    spool = jnp.swapaxes(state_pool[0], 0, 1)
    hbm = pl.BlockSpec(memory_space=pl.ANY)
    act = lambda shape: pltpu.VMEM(shape, F32)
    matrices = (a_w_in, a_w_r, a_w_i, a_w_out, b_w_in, b_w_grp, b_w_out)
    vectors = (row(a_norm[0]), a_conv_w[0], row(a_conv_b[0]), row(a_b_r[0]), row(a_b_i[0]),
               row(a_lam[0]), row(b_norm[0]), row(b_b_grp[0]), row(b_scale[0]), row(final_norm))
    bf16_shapes = ((d, 2 * d), (N_LRU_BLOCKS, lb, 2 * lb), (d, d), (d, 2 * d),
                   (len(POOL_WINDOWS), gw, gw), (d, d))
    (y_s, conv_s, lru_s, pool_s,
     awin, wri, awout, bwin, wgrp, bwout) = pl.pallas_call(
        _sample_kernel,
        in_specs=[hbm] * (4 + len(matrices)) + [_whole(v.shape) for v in vectors],
        out_specs=[hbm] * (4 + len(bf16_shapes)),
        out_shape=[jax.ShapeDtypeStruct(xs.shape, F32),
                   jax.ShapeDtypeStruct(sconv.shape, F32),
                   jax.ShapeDtypeStruct((bs, d), F32),
                   jax.ShapeDtypeStruct(spool.shape, F32)]
        + [jax.ShapeDtypeStruct(shape, BF16) for shape in bf16_shapes],
        scratch_shapes=[act(xs.shape), act(sconv.shape), act((bs, d)), act(spool.shape),
                        act(xs.shape), act(sconv.shape), act((bs, d)), act(xs.shape)]
        + [pltpu.VMEM(shape, BF16) for shape in bf16_shapes]
        + [act((d, 2 * d)), act((d, d)), act((N_LRU_BLOCKS, lb, lb)), act((N_LRU_BLOCKS, lb, lb)),
           act((len(POOL_WINDOWS), gw, gw)),
           pltpu.SemaphoreType.DMA((11,)), pltpu.SemaphoreType.DMA((11,))],
        compiler_params=pltpu.CompilerParams(vmem_limit_bytes=VMEM_LIMIT_BYTES),
        name="sample_step",
    )(xs, sconv, state_lru[0], spool, *matrices, *vectors)

    weights = (row(a_norm[0]), awin, a_conv_w[0], row(a_conv_b[0]), wri, row(a_b_r[0]),
               row(a_b_i[0]), row(a_lam[0]), awout, row(b_norm[0]), bwin, wgrp, row(b_b_grp[0]),
               row(b_scale[0]), bwout, row(final_norm))
    weight_specs = [_whole(w.shape) for w in weights]
    seg_rows = SUBLANES * (tb // SUBLANES + 4)
    nt = tp // tb
    n_blocks = bp * nt
    blk0 = lambda n: jnp.minimum(n, n_blocks - 1)
    blk1 = lambda n: jnp.maximum(n - 1, 0)
    y_p, conv_p, lru_p, pool_p = pl.pallas_call(
        functools.partial(_prompt_kernel, blocks_per_seq=nt),
        grid=(n_blocks + 1,),
        in_specs=[pl.BlockSpec((1, tb, d), lambda n: (blk0(n) // nt, blk0(n) % nt, 0))] + weight_specs,
        out_specs=[pl.BlockSpec((1, tb, d), lambda n: (blk1(n) // nt, blk1(n) % nt, 0)),
                   pl.BlockSpec((1, CONV_W - 1, d), lambda n: (blk0(n) // nt, 0, 0)),
                   pl.BlockSpec((1, 1, d), lambda n: (blk0(n) // nt, 0, 0)),
                   pl.BlockSpec((1, POOL_PAST, d), lambda n: (blk1(n) // nt, 0, 0))],
        out_shape=[jax.ShapeDtypeStruct((bp, tp, d), F32),
                   jax.ShapeDtypeStruct((bp, CONV_W - 1, d), F32),
                   jax.ShapeDtypeStruct((bp, 1, d), F32),
                   jax.ShapeDtypeStruct((bp, POOL_PAST, d), F32)],
        scratch_shapes=[pltpu.VMEM((n_slabs, CONV_PAD + tb, LANES), F32),
                        pltpu.VMEM((n_slabs, seg_rows, LANES), F32),
                        pltpu.VMEM((n_slabs, seg_rows, LANES), F32),
                        pltpu.VMEM((n_slabs, seg_rows, LANES), F32),
                        pltpu.VMEM((1, d), F32),
                        pltpu.VMEM((2, tb, d), F32),
                        pltpu.VMEM((n_slabs, POOL_PAD + tb, LANES), F32),
                        pltpu.VMEM((2, MXU_COLS // LANES, POOL_PAD + tb, LANES), F32),
                        pltpu.VMEM((tb, d), BF16),
                        pltpu.VMEM((CONV_AHEAD, tb, LANES), F32),
                        pltpu.VMEM((CONV_AHEAD, tb, LANES), BF16),
                        pltpu.VMEM((2, tb, 2 * LANES), F32),
                        pltpu.VMEM((n_chunks, tb, MXU_COLS), F32),
                        pltpu.VMEM((tb, d), BF16),
                        pltpu.VMEM((tb, d), BF16),
                        pltpu.VMEM((tb, d), BF16),
                        pltpu.VMEM((tb, d), F32),
                        pltpu.VMEM((n_chunks, tb, MXU_COLS), F32),
                        pltpu.VMEM((tb, d), BF16)],
        compiler_params=pltpu.CompilerParams(
            vmem_limit_bytes=VMEM_LIMIT_BYTES,
            dimension_semantics=("arbitrary",)),
        name="prompt_step",
    )(x_prompt, *weights)

    return (y_p, jnp.swapaxes(y_s, 0, 1),
            conv_p[None], lru_p.reshape(1, bp, d), pool_p[None],
            jnp.swapaxes(conv_s, 0, 1)[None], lru_s[None], jnp.swapaxes(pool_s, 0, 1)[None])
```

```python
import functools

import jax
import jax.numpy as jnp
from jax import lax
from jax.experimental import pallas as pl
from jax.experimental.pallas import tpu as pltpu

PAST_LEN = 16384
N_LRU_BLOCKS = 8
CONV_W = 4
LRU_C = 8.0
POOL_WINDOWS = (2, 4, 8, 16)
POOL_PAST = max(POOL_WINDOWS) - 1
EPS = 1e-6

LANES = 128
SUBLANES = 8
MXU_COLS = 256
CONV_PAD = SUBLANES
POOL_LEVELS = max(POOL_WINDOWS).bit_length() - 1
POOL_PAD = POOL_LEVELS * SUBLANES
PROMPT_BLOCK_ROWS = 512
CONV_AHEAD = 4
VMEM_LIMIT_BYTES = 56 * 1024 * 1024

F32 = jnp.float32
BF16 = jnp.bfloat16


def _rmsnorm(x, g):
    return x * lax.rsqrt(jnp.mean(x * x, axis=-1, keepdims=True) + EPS) * g


def _silu_from_half(hg):
    return hg + hg * jnp.tanh(hg)


def _half_neg_c_softplus_neg(lam):
    z = -lam
    return (-0.5 * LRU_C) * (jnp.maximum(z, 0.0) + jnp.log1p(jnp.exp(-jnp.abs(z))))


def _lru_terms(hx, zr, zi, hbr, hbi, hn):
    log_a = hn + hn * jnp.tanh(zr + hbr)
    a = jnp.exp(log_a)
    y = jnp.tanh(log_a) * (-1.0 - a * a)
    mult = jnp.where(y > 0.0, y * lax.rsqrt(y), 0.0)
    gated = hx + hx * jnp.tanh(zi + hbi)
    return a, mult, gated


def _lane_slab(ref, c):
    return ref[:, c * LANES:(c + 1) * LANES]


def _cols(q):
    return slice(q * MXU_COLS, (q + 1) * MXU_COLS)


def _prompt_stage_order(n_chunks, slabs_per_chunk):
    assert (n_chunks, slabs_per_chunk) == (4, 2)
    return ("aN aX0 bN aX1 aC0 bX0 aC1 aC2 aC3 aR0 aR1 aK0 aX2 aS0 aK1 bX1 aS1 bP0 aR2 aC4 aR3 bM0 aC5 aX3 "
            "aK2 aS2 bX2 aK3 aS3 aR4 bP1 aR5 bM1 aC6 aG0 aC7 aK4 bX3 aS4 aK5 aR6 aR7 aS5 bP2 bM2 aK6 aG1 aS6 "
            "aK7 bG0 aS7 bP3 bM3 aH0 aG2 aH1 bZ0 bG1 aH2 aG3 bZ1 bG2 aH3 bG3 bZ2 bZ3 bO0 bO1 bO2 bO3 aO0 bF "
            "aO1 aO2 aO3").split()


def _prompt_kernel(x_ref, an_ref, awin_ref, cw_ref, cb_ref, wri_ref, br_ref, bi_ref, lam_ref,
                   awout_ref, bn_ref, bwin_ref, wgrp_ref, bgrp_ref, bscale_ref, bwout_ref, fn_ref,
                   y_ref, conv_ref, lru_ref, pool_ref,
                   ext_s, sa_s, sb_s, sh_s, h_s, h1_s, px_s, pt_s,
                   ua_s, hx_s, hxb_s, zr_s, ga_s, hsg_s, ub_s, pooled_s, z_s, gq_s, zg_s, *,
                   blocks_per_seq):
    n = pl.program_id(0)
    n_blocks = pl.num_programs(0) - 1
    ta = jnp.minimum(n, n_blocks - 1) % blocks_per_seq
    tbk = jnp.maximum(n - 1, 0) % blocks_per_seq
    tb, d = x_ref.shape[1], x_ref.shape[2]
    n_slabs = d // LANES
    n_chunks = d // MXU_COLS
    slabs_per_chunk = MXU_COLS // LANES
    assert d // len(POOL_WINDOWS) == MXU_COLS, "one pooling group per matmul column chunk"
    seg = tb // SUBLANES
    pitch = seg + 4
    conv_ahead = hx_s.shape[0]

    @pl.when(n == 0)
    def _():
        h1_s[...] = jnp.zeros_like(h1_s)

    @pl.when(ta == 0)
    def _():
        ext_s[:, 0:CONV_PAD, :] = jnp.zeros((n_slabs, CONV_PAD, LANES), F32)
        h_s[...] = jnp.zeros_like(h_s)

    @pl.when(tbk == 0)
    def _():
        px_s[:, 0:POOL_PAD, :] = jnp.zeros((n_slabs, POOL_PAD, LANES), F32)

    h1_in = h1_s.at[(n + 1) % 2]
    h1_out = h1_s.at[n % 2]
    sub = lax.broadcasted_iota(jnp.int32, (SUBLANES, 1), 0)
    first_token = jnp.logical_and(ta == 0, sub == 0)
    row16 = lax.broadcasted_iota(jnp.int32, (2 * SUBLANES, 1), 0)
    pos16 = tbk * tb + row16
    hn_row = _half_neg_c_softplus_neg(lam_ref[...])
    hcw, hcb = 0.5 * cw_ref[...], 0.5 * cb_ref[...]
    hbr, hbi = 0.5 * br_ref[...], 0.5 * bi_ref[...]
    carries = [None] * n_slabs
    stages = {}

    def a_norm():
        ua_s[...] = _rmsnorm(x_ref[0], an_ref[...]).astype(BF16)

    def a_xb_chunk(q):
        pq = jnp.dot(ua_s[...], awin_ref[:, _cols(q)], preferred_element_type=F32)
        for i in range(slabs_per_chunk):
            ext_s[q * slabs_per_chunk + i, CONV_PAD:CONV_PAD + tb, :] = pq[:, i * LANES:(i + 1) * LANES]

    def a_gate_chunk(q):
        ga_s[q] = jnp.dot(ua_s[...], awin_ref[:, _cols(n_chunks + q)], preferred_element_type=F32)

    def a_conv(c):
        lanes = slice(c * LANES, (c + 1) * LANES)
        for j in range(SUBLANES):
            r0 = CONV_PAD + j * seg
            hx = ext_s[c, r0:r0 + seg, :] * hcw[CONV_W - 1:CONV_W, lanes] + hcb[:, lanes]
            for k in range(CONV_W - 1):
                hx = hx + ext_s[c, pl.ds(r0 - (CONV_W - 1) + k, seg), :] * hcw[k:k + 1, lanes]
            hx_s[c % conv_ahead, j * seg:(j + 1) * seg, :] = hx
            hxb_s[c % conv_ahead, j * seg:(j + 1) * seg, :] = hx.astype(BF16)

    def a_gate_proj(c):
        zr_s[c % 2] = jnp.dot(hxb_s[c % conv_ahead], wri_ref[c], preferred_element_type=F32)

    def a_coeffs(c):
        lanes = slice(c * LANES, (c + 1) * LANES)
        for j in range(SUBLANES):
            rows = slice(j * seg, (j + 1) * seg)
            a, mult, gated = _lru_terms(hx_s[c % conv_ahead, rows, :], zr_s[c % 2, rows, :LANES],
                                        zr_s[c % 2, rows, LANES:], hbr[:, lanes], hbi[:, lanes],
                                        hn_row[:, lanes])
            if j == 0:
                mult = jnp.concatenate(
                    [jnp.where(first_token, 1.0, mult[0:SUBLANES]), mult[SUBLANES:]], axis=0)
            sa_s[c, pl.ds(j * pitch, seg), :] = a
            sb_s[c, pl.ds(j * pitch, seg), :] = mult * gated

    def a_scan(c):
        hl = sb_s[c, pl.ds(0, SUBLANES, stride=pitch), :]
        pp = sa_s[c, pl.ds(0, SUBLANES, stride=pitch), :]
        for i in range(1, seg):
            av = sa_s[c, pl.ds(i, SUBLANES, stride=pitch), :]
            hl = av * hl + sb_s[c, pl.ds(i, SUBLANES, stride=pitch), :]
            pp = av * pp
        h_in = jnp.broadcast_to(_lane_slab(h_s, c), (SUBLANES, LANES))
        cin = h_in
        for _ in range(SUBLANES - 1):
            cin = jnp.where(sub == 0, h_in, pltpu.roll(hl + pp * cin, 1, 0))
        carries[c] = (hl + pp * cin)[SUBLANES - 1:SUBLANES]
        h = cin
        for i in range(seg):
            h = sa_s[c, pl.ds(i, SUBLANES, stride=pitch), :] * h + sb_s[c, pl.ds(i, SUBLANES, stride=pitch), :]
            sh_s[c, pl.ds(i, SUBLANES, stride=pitch), :] = h

    def a_gated_out(q):
        hs = jnp.concatenate(
            [jnp.concatenate([sh_s[c, pl.ds(j * pitch, seg), :] for j in range(SUBLANES)], axis=0)
             for c in range(q * slabs_per_chunk, (q + 1) * slabs_per_chunk)], axis=1)
        hsg_s[:, _cols(q)] = (hs * _silu_from_half(ga_s[q])).astype(BF16)

    def a_out_proj(q):
        y0 = jnp.dot(hsg_s[...], awout_ref[:, _cols(q)], preferred_element_type=F32)
        h1_out[:, _cols(q)] = x_ref[0, :, _cols(q)] + y0

    def b_norm():
        ub_s[...] = _rmsnorm(h1_in[...], bn_ref[...]).astype(BF16)

    def b_xb_chunk(q):
        pq = jnp.dot(ub_s[...], bwin_ref[:, _cols(q)], preferred_element_type=F32)
        for i in range(slabs_per_chunk):
            px_s[q * slabs_per_chunk + i, POOL_PAD:POOL_PAD + tb, :] = pq[:, i * LANES:(i + 1) * LANES]

    def pool_slab(c, w):
        levels = w.bit_length() - 1
        tmp = c % slabs_per_chunk
        src = px_s.at[c]
        s = None
        for k in range(levels):
            back = (levels - 1 - k) * SUBLANES
            rows = tb + back
            s = (src[pl.ds(POOL_PAD - back, rows), :]
                 + src[pl.ds(POOL_PAD - back - (1 << k), rows), :])
            if k < levels - 1:
                pt_s[k % 2, tmp, POOL_PAD - back:POOL_PAD + tb, :] = s
                src = pt_s.at[k % 2, tmp]
        inv_cnt = 1.0 / jnp.minimum(pos16 + 1, w).astype(F32)
        mean = jnp.concatenate([s[0:2 * SUBLANES] * inv_cnt, s[2 * SUBLANES:] * (1.0 / w)], axis=0)
        return mean - px_s[c, POOL_PAD:POOL_PAD + tb, :]

    def b_pool(q):
        pooled_s[:, _cols(q)] = jnp.concatenate(
            [pool_slab(q * slabs_per_chunk + i, POOL_WINDOWS[q]) for i in range(slabs_per_chunk)],
            axis=1).astype(BF16)

    def b_group_proj(q):
        z_s[:, _cols(q)] = jnp.dot(pooled_s[:, _cols(q)], wgrp_ref[q], preferred_element_type=F32)

    def b_gate_chunk(q):
        gq_s[q] = jnp.dot(ub_s[...], bwin_ref[:, _cols(n_chunks + q)], preferred_element_type=F32)

    def b_gated_proj(q):
        z1 = (z_s[:, _cols(q)] + bgrp_ref[:, _cols(q)]) * bscale_ref[:, _cols(q)]
        zg_s[:, _cols(q)] = (z1 * _silu_from_half(gq_s[q])).astype(BF16)

    def b_out_proj(q):
        y1 = jnp.dot(zg_s[...], bwout_ref[:, _cols(q)], preferred_element_type=F32)
        y_ref[0, :, _cols(q)] = h1_in[:, _cols(q)] + y1

    def b_final_norm():
        y_ref[0] = _rmsnorm(y_ref[0], fn_ref[...])

    stages["aN"] = a_norm
    stages["bN"] = b_norm
    stages["bF"] = b_final_norm
    for q in range(n_chunks):
        for name, fn in (("aX", a_xb_chunk), ("aG", a_gate_chunk), ("aH", a_gated_out),
                         ("aO", a_out_proj), ("bX", b_xb_chunk), ("bP", b_pool),
                         ("bM", b_group_proj), ("bG", b_gate_chunk), ("bZ", b_gated_proj),
                         ("bO", b_out_proj)):
            stages[f"{name}{q}"] = functools.partial(fn, q)
    for c in range(n_slabs):
        for name, fn in (("aC", a_conv), ("aR", a_gate_proj), ("aK", a_coeffs), ("aS", a_scan)):
            stages[f"{name}{c}"] = functools.partial(fn, c)

    order = _prompt_stage_order(n_chunks, slabs_per_chunk)
    assert sorted(order) == sorted(stages), "every stage is issued exactly once"
    at = order.index
    for c in range(n_slabs):
        assert c < conv_ahead or at(f"aK{c - conv_ahead}") < at(f"aC{c}")
        assert c < zr_s.shape[0] or at(f"aK{c - zr_s.shape[0]}") < at(f"aR{c}")
    for name in order:
        stages[name]()
    h_s[...] = jnp.concatenate(carries, axis=1)

    @pl.when(jnp.logical_and(ta == blocks_per_seq - 1, n < n_blocks))
    def _():
        for c in range(n_slabs):
            conv_ref[0, :, c * LANES:(c + 1) * LANES] = ext_s[c, tb + CONV_PAD - (CONV_W - 1):tb + CONV_PAD, :]
        lru_ref[0] = h_s[...]

    @pl.when(jnp.logical_and(tbk == blocks_per_seq - 1, n > 0))
    def _():
        for c in range(n_slabs):
            pool_ref[0, :, c * LANES:(c + 1) * LANES] = px_s[c, tb + POOL_PAD - POOL_PAST:tb + POOL_PAD, :]

    ext_s[:, 0:CONV_PAD, :] = ext_s[:, tb:tb + CONV_PAD, :]
    px_s[:, 0:POOL_PAD, :] = px_s[:, tb:tb + POOL_PAD, :]


def _cast_rows(dst, src, scale=None, rows_per_step=128):
    for r in range(0, src.shape[0], rows_per_step):
        v = src[r:r + rows_per_step, :]
        dst[r:r + rows_per_step, :] = (v if scale is None else v * scale).astype(dst.dtype)


def _sample_kernel(x_hbm, sconv_hbm, slru_hbm, spool_hbm,
                   awin_f, wr_f, wi_f, awout_f, bwin_f, wgrp_f, bwout_f,
                   an_ref, cw_ref, cb_ref, br_ref, bi_ref, lam_ref, bn_ref, bgrp_ref, bscale_ref, fn_ref,
                   y_hbm, conv_hbm, lru_hbm, pool_hbm,
                   awin_o, wri_o, awout_o, bwin_o, wgrp_o, bwout_o,
                   x_v, sconv_v, slru_v, spool_v, y_v, conv_v, lru_v, xb1_v,
                   awin_v, wri_v, awout_v, bwin_v, wgrp_v, bwout_v,
                   stage_in, stage_out, stage_r, stage_i, stage_g, load_sem, store_sem):
    ts, nb, d = x_v.shape
    lb = d // N_LRU_BLOCKS
    gw = d // len(POOL_WINDOWS)
    kept = POOL_PAST - ts

    sources = (("x", x_hbm, x_v), ("awin", awin_f.at[0], stage_in), ("sconv", sconv_hbm, sconv_v),
               ("wr", wr_f.at[0], stage_r), ("wi", wi_f.at[0], stage_i), ("slru", slru_hbm, slru_v),
               ("awout", awout_f.at[0], stage_out), ("spool", spool_hbm, spool_v),
               ("wgrp", wgrp_f.at[0], stage_g),
               ("bwin", bwin_f.at[0], stage_in), ("bwout", bwout_f.at[0], stage_out))
    loads = {name: pltpu.make_async_copy(src, dst, load_sem.at[i])
             for i, (name, src, dst) in enumerate(sources)}
    for name in ("x", "awin", "sconv", "wr", "wi", "slru", "awout", "spool", "wgrp"):
        loads[name].start()
    sinks = (("conv", conv_v, conv_hbm), ("lru", lru_v, lru_hbm),
             ("pool_kept", spool_v.at[pl.ds(ts, kept)], pool_hbm.at[pl.ds(0, kept)]),
             ("pool_new", xb1_v, pool_hbm.at[pl.ds(kept, ts)]), ("y", y_v, y_hbm),
             ("awin", awin_v, awin_o), ("wri", wri_v, wri_o), ("awout", awout_v, awout_o),
             ("bwin", bwin_v, bwin_o), ("wgrp", wgrp_v, wgrp_o), ("bwout", bwout_v, bwout_o))
    stores = {name: pltpu.make_async_copy(src, dst, store_sem.at[i])
              for i, (name, src, dst) in enumerate(sinks)}
    gate_half = jnp.where(lax.broadcasted_iota(jnp.int32, (1, 2 * d), 1) < d, 1.0, 0.5)

    def slab(v, i):
        return v[i * nb:(i + 1) * nb]

    loads["x"].wait()
    x = x_v[...].reshape(ts * nb, d)
    u = _rmsnorm(x, an_ref[...]).astype(BF16)
    loads["awin"].wait()
    _cast_rows(awin_v, stage_in, gate_half)
    loads["bwin"].start()
    proj = jnp.dot(u, awin_v[...], preferred_element_type=F32)
    xb = proj[:, :d]
    for name in ("sconv", "wr", "wi", "slru"):
        loads[name].wait()
    for n in range(N_LRU_BLOCKS):
        wri_v[n, :, :lb] = stage_r[n].astype(BF16)
        wri_v[n, :, lb:] = stage_i[n].astype(BF16)
    for name in ("awin", "wri"):
        stores[name].start()
    ext = [sconv_v[k] for k in range(CONV_W - 1)] + [slab(xb, i) for i in range(ts)]
    hcw, hcb = 0.5 * cw_ref[...], 0.5 * cb_ref[...]
    hx_slabs = []
    for i in range(ts):
        acc = hcb
        for k in range(CONV_W):
            acc = acc + ext[i + k] * hcw[k:k + 1, :]
        hx_slabs.append(acc)
    hx = jnp.concatenate(hx_slabs, axis=0)
    hx_bf16 = hx.astype(BF16)
    zr, zi = [], []
    for n in range(N_LRU_BLOCKS):
        z = jnp.dot(hx_bf16[:, n * lb:(n + 1) * lb], wri_v[n], preferred_element_type=F32)
        zr.append(z[:, :lb])
        zi.append(z[:, lb:])
    a, mult, gated = _lru_terms(hx, jnp.concatenate(zr, axis=1), jnp.concatenate(zi, axis=1),
                                0.5 * br_ref[...], 0.5 * bi_ref[...],
                                _half_neg_c_softplus_neg(lam_ref[...]))
    h = slru_v[...]
    hs_slabs = []
    for i in range(ts):
        m = slab(mult, i)
        if PAST_LEN + i == 0:
            m = jnp.ones_like(m)
        h = slab(a, i) * h + m * slab(gated, i)
        hs_slabs.append(h)
    hs = jnp.concatenate(hs_slabs, axis=0)
    loads["awout"].wait()
    _cast_rows(awout_v, stage_out)
    loads["bwout"].start()
    stores["awout"].start()
    y0 = jnp.dot((hs * _silu_from_half(proj[:, d:])).astype(BF16), awout_v[...],
                 preferred_element_type=F32)
    h1 = x + y0
    lru_v[...] = h
    for k in range(CONV_W - 1):
        conv_v[k] = ext[ts + k]
    for name in ("bwin", "spool", "wgrp"):
        loads[name].wait()
    _cast_rows(bwin_v, stage_in, gate_half)
    for g in range(len(POOL_WINDOWS)):
        wgrp_v[g] = stage_g[g].astype(BF16)
    for name in ("conv", "lru", "pool_kept", "bwin", "wgrp"):
        stores[name].start()

    u1 = _rmsnorm(h1, bn_ref[...]).astype(BF16)
    proj1 = jnp.dot(u1, bwin_v[...], preferred_element_type=F32)
    xb1 = proj1[:, :d]
    ext2 = [spool_v[k] for k in range(POOL_PAST)] + [slab(xb1, i) for i in range(ts)]
    pooled_slabs = []
    for i in range(ts):
        means = []
        for gi, w in enumerate(POOL_WINDOWS):
            lanes = slice(gi * gw, (gi + 1) * gw)
            s = ext2[POOL_PAST + i][:, lanes]
            for j in range(1, w):
                s = s + ext2[POOL_PAST + i - j][:, lanes]
            means.append(s * (1.0 / min(PAST_LEN + i + 1, w)))
        pooled_slabs.append(jnp.concatenate(means, axis=1) - ext2[POOL_PAST + i])
    pooled = jnp.concatenate(pooled_slabs, axis=0).astype(BF16)
    z1 = jnp.concatenate(
        [jnp.dot(pooled[:, g * gw:(g + 1) * gw], wgrp_v[g], preferred_element_type=F32)
         for g in range(len(POOL_WINDOWS))], axis=1)
    z1 = (z1 + bgrp_ref[...]) * bscale_ref[...]
    loads["bwout"].wait()
    _cast_rows(bwout_v, stage_out)
    xb1_v[...] = xb1.reshape(ts, nb, d)
    for name in ("bwout", "pool_new"):
        stores[name].start()
    y1 = jnp.dot((z1 * _silu_from_half(proj1[:, d:])).astype(BF16), bwout_v[...],
                 preferred_element_type=F32)
    h2 = h1 + y1
    y_v[...] = _rmsnorm(h2, fn_ref[...]).reshape(ts, nb, d)
    stores["y"].start()
    for store in stores.values():
        store.wait()


def _whole(shape):
    return pl.BlockSpec(shape, lambda *_: (0,) * len(shape), pipeline_mode=pl.Buffered(1))


def kernel(x_prompt, x_sample, state_conv, state_lru, state_pool, a_norm, a_w_in, a_conv_w, a_conv_b, a_w_r, a_b_r, a_w_i, a_b_i, a_lam, a_w_out, b_norm, b_w_in, b_w_grp, b_b_grp, b_scale, b_w_out, final_norm):
    bp, tp, d = x_prompt.shape
    bs, ts, _ = x_sample.shape
    tb = PROMPT_BLOCK_ROWS
    assert a_norm.shape[0] == 1 and b_norm.shape[0] == 1, "one layer of each mixer type"
    assert d % MXU_COLS == 0 and d // N_LRU_BLOCKS == LANES
    assert tp % tb == 0 and tb % (SUBLANES * SUBLANES) == 0 and tb >= 2 * SUBLANES
    assert CONV_W - 1 <= ts <= POOL_PAST

    row = lambda v: v.reshape(1, d)
    n_slabs = d // LANES
    n_chunks = d // MXU_COLS
    lb = d // N_LRU_BLOCKS
    gw = d // len(POOL_WINDOWS)

    xs = jnp.swapaxes(x_sample, 0, 1)
    sconv = jnp.swapaxes(state_conv[0], 0, 1)
    spool = jnp.swapaxes(state_pool[0], 0, 1)
    hbm = pl.BlockSpec(memory_space=pl.ANY)
    act = lambda shape: pltpu.VMEM(shape, F32)
    matrices = (a_w_in, a_w_r, a_w_i, a_w_out, b_w_in, b_w_grp, b_w_out)
    vectors = (row(a_norm[0]), a_conv_w[0], row(a_conv_b[0]), row(a_b_r[0]), row(a_b_i[0]),
               row(a_lam[0]), row(b_norm[0]), row(b_b_grp[0]), row(b_scale[0]), row(final_norm))
    bf16_shapes = ((d, 2 * d), (N_LRU_BLOCKS, lb, 2 * lb), (d, d), (d, 2 * d),
                   (len(POOL_WINDOWS), gw, gw), (d, d))
    (y_s, conv_s, lru_s, pool_s,
     awin, wri, awout, bwin, wgrp, bwout) = pl.pallas_call(
        _sample_kernel,
        in_specs=[hbm] * (4 + len(matrices)) + [_whole(v.shape) for v in vectors],
        out_specs=[hbm] * (4 + len(bf16_shapes)),
        out_shape=[jax.ShapeDtypeStruct(xs.shape, F32),
                   jax.ShapeDtypeStruct(sconv.shape, F32),
                   jax.ShapeDtypeStruct((bs, d), F32),
                   jax.ShapeDtypeStruct(spool.shape, F32)]
        + [jax.ShapeDtypeStruct(shape, BF16) for shape in bf16_shapes],
        scratch_shapes=[act(xs.shape), act(sconv.shape), act((bs, d)), act(spool.shape),
                        act(xs.shape), act(sconv.shape), act((bs, d)), act(xs.shape)]
        + [pltpu.VMEM(shape, BF16) for shape in bf16_shapes]
        + [act((d, 2 * d)), act((d, d)), act((N_LRU_BLOCKS, lb, lb)), act((N_LRU_BLOCKS, lb, lb)),
           act((len(POOL_WINDOWS), gw, gw)),
           pltpu.SemaphoreType.DMA((11,)), pltpu.SemaphoreType.DMA((11,))],
        compiler_params=pltpu.CompilerParams(vmem_limit_bytes=VMEM_LIMIT_BYTES),
        name="sample_step",
    )(xs, sconv, state_lru[0], spool, *matrices, *vectors)

    weights = (row(a_norm[0]), awin, a_conv_w[0], row(a_conv_b[0]), wri, row(a_b_r[0]),
               row(a_b_i[0]), row(a_lam[0]), awout, row(b_norm[0]), bwin, wgrp, row(b_b_grp[0]),
               row(b_scale[0]), bwout, row(final_norm))
    weight_specs = [_whole(w.shape) for w in weights]
    seg_rows = SUBLANES * (tb // SUBLANES + 4)
    nt = tp // tb
    n_blocks = bp * nt
    blk0 = lambda n: jnp.minimum(n, n_blocks - 1)
    blk1 = lambda n: jnp.maximum(n - 1, 0)
    y_p, conv_p, lru_p, pool_p = pl.pallas_call(
        functools.partial(_prompt_kernel, blocks_per_seq=nt),
        grid=(n_blocks + 1,),
        in_specs=[pl.BlockSpec((1, tb, d), lambda n: (blk0(n) // nt, blk0(n) % nt, 0))] + weight_specs,
        out_specs=[pl.BlockSpec((1, tb, d), lambda n: (blk1(n) // nt, blk1(n) % nt, 0)),
                   pl.BlockSpec((1, CONV_W - 1, d), lambda n: (blk0(n) // nt, 0, 0)),
                   pl.BlockSpec((1, 1, d), lambda n: (blk0(n) // nt, 0, 0)),
                   pl.BlockSpec((1, POOL_PAST, d), lambda n: (blk1(n) // nt, 0, 0))],
        out_shape=[jax.ShapeDtypeStruct((bp, tp, d), F32),
                   jax.ShapeDtypeStruct((bp, CONV_W - 1, d), F32),
                   jax.ShapeDtypeStruct((bp, 1, d), F32),
                   jax.ShapeDtypeStruct((bp, POOL_PAST, d), F32)],
        scratch_shapes=[pltpu.VMEM((n_slabs, CONV_PAD + tb, LANES), F32),
                        pltpu.VMEM((n_slabs, seg_rows, LANES), F32),
                        pltpu.VMEM((n_slabs, seg_rows, LANES), F32),
                        pltpu.VMEM((n_slabs, seg_rows, LANES), F32),
                        pltpu.VMEM((1, d), F32),
                        pltpu.VMEM((2, tb, d), F32),
                        pltpu.VMEM((n_slabs, POOL_PAD + tb, LANES), F32),
                        pltpu.VMEM((2, MXU_COLS // LANES, POOL_PAD + tb, LANES), F32),
                        pltpu.VMEM((tb, d), BF16),
                        pltpu.VMEM((CONV_AHEAD, tb, LANES), F32),
                        pltpu.VMEM((CONV_AHEAD, tb, LANES), BF16),
                        pltpu.VMEM((2, tb, 2 * LANES), F32),
                        pltpu.VMEM((n_chunks, tb, MXU_COLS), F32),
                        pltpu.VMEM((tb, d), BF16),
                        pltpu.VMEM((tb, d), BF16),
                        pltpu.VMEM((tb, d), BF16),
                        pltpu.VMEM((tb, d), F32),
                        pltpu.VMEM((n_chunks, tb, MXU_COLS), F32),
                        pltpu.VMEM((tb, d), BF16)],
        compiler_params=pltpu.CompilerParams(
            vmem_limit_bytes=VMEM_LIMIT_BYTES,
            dimension_semantics=("arbitrary",)),
        name="prompt_step",
    )(x_prompt, *weights)

    return (y_p, jnp.swapaxes(y_s, 0, 1),
            conv_p[None], lru_p.reshape(1, bp, d), pool_p[None],
            jnp.swapaxes(conv_s, 0, 1)[None], lru_s[None], jnp.swapaxes(pool_s, 0, 1)[None])
```

```python
import functools

import jax
import jax.numpy as jnp
from jax import lax
from jax.experimental import pallas as pl
from jax.experimental.pallas import tpu as pltpu

PAST_LEN = 16384
N_LRU_BLOCKS = 8
CONV_W = 4
LRU_C = 8.0
POOL_WINDOWS = (2, 4, 8, 16)
POOL_PAST = max(POOL_WINDOWS) - 1
EPS = 1e-6

LANES = 128
SUBLANES = 8
MXU_COLS = 256
CONV_PAD = SUBLANES
POOL_LEVELS = max(POOL_WINDOWS).bit_length() - 1
POOL_PAD = POOL_LEVELS * SUBLANES
PROMPT_BLOCK_ROWS = 512
STORE_DMA_PRIORITY = 1
CONV_AHEAD = 4
VMEM_LIMIT_BYTES = 56 * 1024 * 1024

F32 = jnp.float32
BF16 = jnp.bfloat16


def _rmsnorm(x, g):
    return x * lax.rsqrt(jnp.mean(x * x, axis=-1, keepdims=True) + EPS) * g


def _silu_from_half(hg):
    return hg + hg * jnp.tanh(hg)


def _half_neg_c_softplus_neg(lam):
    z = -lam
    return (-0.5 * LRU_C) * (jnp.maximum(z, 0.0) + jnp.log1p(jnp.exp(-jnp.abs(z))))


def _lru_terms(hx, zr, zi, hbr, hbi, hn):
    log_a = hn + hn * jnp.tanh(zr + hbr)
    a = jnp.exp(log_a)
    y = jnp.tanh(log_a) * (-1.0 - a * a)
    mult = jnp.where(y > 0.0, y * lax.rsqrt(y), 0.0)
    gated = hx + hx * jnp.tanh(zi + hbi)
    return a, mult, gated


def _lane_slab(ref, c):
    return ref[:, c * LANES:(c + 1) * LANES]


def _cols(q):
    return slice(q * MXU_COLS, (q + 1) * MXU_COLS)


def _prompt_stage_order(n_chunks, slabs_per_chunk):
    assert (n_chunks, slabs_per_chunk) == (4, 2)
    return ("aN aX0 bN aX1 aC0 bX0 aC1 aC2 aC3 aR0 aR1 aK0 aX2 aS0 aK1 bX1 aS1 bP0 aR2 aC4 aR3 bM0 aC5 aX3 "
            "aK2 aS2 bX2 aK3 aS3 aR4 bP1 aR5 bM1 aC6 aG0 aC7 aK4 bX3 aS4 aK5 aR6 aR7 aS5 bP2 bM2 aK6 aG1 aS6 "
            "aK7 bG0 aS7 bP3 bM3 aH0 aG2 aH1 bZ0 bG1 aH2 aG3 bZ1 bG2 aH3 bG3 bZ2 bZ3 bO0 bO1 bO2 bO3 aO0 bF "
            "aO1 aO2 aO3").split()


def _prompt_kernel(x_ref, an_ref, awin_ref, cw_ref, cb_ref, wri_ref, br_ref, bi_ref, lam_ref,
                   awout_ref, bn_ref, bwin_ref, wgrp_ref, bgrp_ref, bscale_ref, bwout_ref, fn_ref,
                   y_ref, conv_ref, lru_ref, pool_ref,
                   ext_s, sa_s, sb_s, sh_s, h_s, h1_s, px_s, pt_s,
                   ua_s, hx_s, hxb_s, zr_s, ga_s, hsg_s, ub_s, pooled_s, z_s, gq_s, zg_s, *,
                   blocks_per_seq):
    n = pl.program_id(0)
    n_blocks = pl.num_programs(0) - 1
    ta = jnp.minimum(n, n_blocks - 1) % blocks_per_seq
    tbk = jnp.maximum(n - 1, 0) % blocks_per_seq
    tb, d = x_ref.shape[1], x_ref.shape[2]
    n_slabs = d // LANES
    n_chunks = d // MXU_COLS
    slabs_per_chunk = MXU_COLS // LANES
    assert d // len(POOL_WINDOWS) == MXU_COLS, "one pooling group per matmul column chunk"
    seg = tb // SUBLANES
    pitch = seg + 4
    conv_ahead = hx_s.shape[0]

    @pl.when(n == 0)
    def _():
        h1_s[...] = jnp.zeros_like(h1_s)

    @pl.when(ta == 0)
    def _():
        ext_s[:, 0:CONV_PAD, :] = jnp.zeros((n_slabs, CONV_PAD, LANES), F32)
        h_s[...] = jnp.zeros_like(h_s)

    @pl.when(tbk == 0)
    def _():
        px_s[:, 0:POOL_PAD, :] = jnp.zeros((n_slabs, POOL_PAD, LANES), F32)

    h1_in = h1_s.at[(n + 1) % 2]
    h1_out = h1_s.at[n % 2]
    sub = lax.broadcasted_iota(jnp.int32, (SUBLANES, 1), 0)
    first_token = jnp.logical_and(ta == 0, sub == 0)
    row16 = lax.broadcasted_iota(jnp.int32, (2 * SUBLANES, 1), 0)
    pos16 = tbk * tb + row16
    hn_row = _half_neg_c_softplus_neg(lam_ref[...])
    hcw, hcb = 0.5 * cw_ref[...], 0.5 * cb_ref[...]
    hbr, hbi = 0.5 * br_ref[...], 0.5 * bi_ref[...]
    carries = [None] * n_slabs
    stages = {}

    def a_norm():
        ua_s[...] = _rmsnorm(x_ref[0], an_ref[...]).astype(BF16)

    def a_xb_chunk(q):
        pq = jnp.dot(ua_s[...], awin_ref[:, _cols(q)], preferred_element_type=F32)
        for i in range(slabs_per_chunk):
            ext_s[q * slabs_per_chunk + i, CONV_PAD:CONV_PAD + tb, :] = pq[:, i * LANES:(i + 1) * LANES]

    def a_gate_chunk(q):
        ga_s[q] = jnp.dot(ua_s[...], awin_ref[:, _cols(n_chunks + q)], preferred_element_type=F32)

    def a_conv(c):
        lanes = slice(c * LANES, (c + 1) * LANES)
        for j in range(SUBLANES):
            r0 = CONV_PAD + j * seg
            hx = ext_s[c, r0:r0 + seg, :] * hcw[CONV_W - 1:CONV_W, lanes] + hcb[:, lanes]
            for k in range(CONV_W - 1):
                hx = hx + ext_s[c, pl.ds(r0 - (CONV_W - 1) + k, seg), :] * hcw[k:k + 1, lanes]
            hx_s[c % conv_ahead, j * seg:(j + 1) * seg, :] = hx
            hxb_s[c % conv_ahead, j * seg:(j + 1) * seg, :] = hx.astype(BF16)

    def a_gate_proj(c):
        zr_s[c % 2] = jnp.dot(hxb_s[c % conv_ahead], wri_ref[c], preferred_element_type=F32)

    def a_coeffs(c):
        lanes = slice(c * LANES, (c + 1) * LANES)
        for j in range(SUBLANES):
            rows = slice(j * seg, (j + 1) * seg)
            a, mult, gated = _lru_terms(hx_s[c % conv_ahead, rows, :], zr_s[c % 2, rows, :LANES],
                                        zr_s[c % 2, rows, LANES:], hbr[:, lanes], hbi[:, lanes],
                                        hn_row[:, lanes])
            if j == 0:
                mult = jnp.concatenate(
                    [jnp.where(first_token, 1.0, mult[0:SUBLANES]), mult[SUBLANES:]], axis=0)
            sa_s[c, pl.ds(j * pitch, seg), :] = a
            sb_s[c, pl.ds(j * pitch, seg), :] = mult * gated

    def a_scan(c):
        hl = sb_s[c, pl.ds(0, SUBLANES, stride=pitch), :]
        pp = sa_s[c, pl.ds(0, SUBLANES, stride=pitch), :]
        for i in range(1, seg):
            av = sa_s[c, pl.ds(i, SUBLANES, stride=pitch), :]
            hl = av * hl + sb_s[c, pl.ds(i, SUBLANES, stride=pitch), :]
            pp = av * pp
        h_in = jnp.broadcast_to(_lane_slab(h_s, c), (SUBLANES, LANES))
        cin = h_in
        for _ in range(SUBLANES - 1):
            cin = jnp.where(sub == 0, h_in, pltpu.roll(hl + pp * cin, 1, 0))
        carries[c] = (hl + pp * cin)[SUBLANES - 1:SUBLANES]
        h = cin
        for i in range(seg):
            h = sa_s[c, pl.ds(i, SUBLANES, stride=pitch), :] * h + sb_s[c, pl.ds(i, SUBLANES, stride=pitch), :]
            sh_s[c, pl.ds(i, SUBLANES, stride=pitch), :] = h

    def a_gated_out(q):
        hs = jnp.concatenate(
            [jnp.concatenate([sh_s[c, pl.ds(j * pitch, seg), :] for j in range(SUBLANES)], axis=0)
             for c in range(q * slabs_per_chunk, (q + 1) * slabs_per_chunk)], axis=1)
        hsg_s[:, _cols(q)] = (hs * _silu_from_half(ga_s[q])).astype(BF16)

    def a_out_proj(q):
        y0 = jnp.dot(hsg_s[...], awout_ref[:, _cols(q)], preferred_element_type=F32)
        h1_out[:, _cols(q)] = x_ref[0, :, _cols(q)] + y0

    def b_norm():
        ub_s[...] = _rmsnorm(h1_in[...], bn_ref[...]).astype(BF16)

    def b_xb_chunk(q):
        pq = jnp.dot(ub_s[...], bwin_ref[:, _cols(q)], preferred_element_type=F32)
        for i in range(slabs_per_chunk):
            px_s[q * slabs_per_chunk + i, POOL_PAD:POOL_PAD + tb, :] = pq[:, i * LANES:(i + 1) * LANES]

    def pool_slab(c, w):
        levels = w.bit_length() - 1
        tmp = c % slabs_per_chunk
        src = px_s.at[c]
        s = None
        for k in range(levels):
            back = (levels - 1 - k) * SUBLANES
            rows = tb + back
            s = (src[pl.ds(POOL_PAD - back, rows), :]
                 + src[pl.ds(POOL_PAD - back - (1 << k), rows), :])
            if k < levels - 1:
                pt_s[k % 2, tmp, POOL_PAD - back:POOL_PAD + tb, :] = s
                src = pt_s.at[k % 2, tmp]
        inv_cnt = 1.0 / jnp.minimum(pos16 + 1, w).astype(F32)
        mean = jnp.concatenate([s[0:2 * SUBLANES] * inv_cnt, s[2 * SUBLANES:] * (1.0 / w)], axis=0)
        return mean - px_s[c, POOL_PAD:POOL_PAD + tb, :]

    def b_pool(q):
        pooled_s[:, _cols(q)] = jnp.concatenate(
            [pool_slab(q * slabs_per_chunk + i, POOL_WINDOWS[q]) for i in range(slabs_per_chunk)],
            axis=1).astype(BF16)

    def b_group_proj(q):
        z_s[:, _cols(q)] = jnp.dot(pooled_s[:, _cols(q)], wgrp_ref[q], preferred_element_type=F32)

    def b_gate_chunk(q):
        gq_s[q] = jnp.dot(ub_s[...], bwin_ref[:, _cols(n_chunks + q)], preferred_element_type=F32)

    def b_gated_proj(q):
        z1 = (z_s[:, _cols(q)] + bgrp_ref[:, _cols(q)]) * bscale_ref[:, _cols(q)]
        zg_s[:, _cols(q)] = (z1 * _silu_from_half(gq_s[q])).astype(BF16)

    def b_out_proj(q):
        y1 = jnp.dot(zg_s[...], bwout_ref[:, _cols(q)], preferred_element_type=F32)
        y_ref[0, :, _cols(q)] = h1_in[:, _cols(q)] + y1

    def b_final_norm():
        y_ref[0] = _rmsnorm(y_ref[0], fn_ref[...])

    stages["aN"] = a_norm
    stages["bN"] = b_norm
    stages["bF"] = b_final_norm
    for q in range(n_chunks):
        for name, fn in (("aX", a_xb_chunk), ("aG", a_gate_chunk), ("aH", a_gated_out),
                         ("aO", a_out_proj), ("bX", b_xb_chunk), ("bP", b_pool),
                         ("bM", b_group_proj), ("bG", b_gate_chunk), ("bZ", b_gated_proj),
                         ("bO", b_out_proj)):
            stages[f"{name}{q}"] = functools.partial(fn, q)
    for c in range(n_slabs):
        for name, fn in (("aC", a_conv), ("aR", a_gate_proj), ("aK", a_coeffs), ("aS", a_scan)):
            stages[f"{name}{c}"] = functools.partial(fn, c)

    order = _prompt_stage_order(n_chunks, slabs_per_chunk)
    assert sorted(order) == sorted(stages), "every stage is issued exactly once"
    at = order.index
    for c in range(n_slabs):
        assert c < conv_ahead or at(f"aK{c - conv_ahead}") < at(f"aC{c}")
        assert c < zr_s.shape[0] or at(f"aK{c - zr_s.shape[0]}") < at(f"aR{c}")
    for name in order:
        stages[name]()
    h_s[...] = jnp.concatenate(carries, axis=1)

    @pl.when(jnp.logical_and(ta == blocks_per_seq - 1, n < n_blocks))
    def _():
        for c in range(n_slabs):
            conv_ref[0, :, c * LANES:(c + 1) * LANES] = ext_s[c, tb + CONV_PAD - (CONV_W - 1):tb + CONV_PAD, :]
        lru_ref[0] = h_s[...]

    @pl.when(jnp.logical_and(tbk == blocks_per_seq - 1, n > 0))
    def _():
        for c in range(n_slabs):
            pool_ref[0, :, c * LANES:(c + 1) * LANES] = px_s[c, tb + POOL_PAD - POOL_PAST:tb + POOL_PAD, :]

    ext_s[:, 0:CONV_PAD, :] = ext_s[:, tb:tb + CONV_PAD, :]
    px_s[:, 0:POOL_PAD, :] = px_s[:, tb:tb + POOL_PAD, :]


def _cast_rows(dst, src, scale=None, rows_per_step=128):
    for r in range(0, src.shape[0], rows_per_step):
        v = src[r:r + rows_per_step, :]
        dst[r:r + rows_per_step, :] = (v if scale is None else v * scale).astype(dst.dtype)


def _sample_kernel(x_hbm, sconv_hbm, slru_hbm, spool_hbm,
                   awin_f, wr_f, wi_f, awout_f, bwin_f, wgrp_f, bwout_f,
                   an_ref, cw_ref, cb_ref, br_ref, bi_ref, lam_ref, bn_ref, bgrp_ref, bscale_ref, fn_ref,
                   y_hbm, conv_hbm, lru_hbm, pool_hbm,
                   awin_o, wri_o, awout_o, bwin_o, wgrp_o, bwout_o,
                   x_v, sconv_v, slru_v, spool_v, y_v, conv_v, lru_v, xb1_v,
                   awin_v, wri_v, awout_v, bwin_v, wgrp_v, bwout_v,
                   stage_in, stage_out, stage_r, stage_i, stage_g, load_sem, store_sem):
    ts, nb, d = x_v.shape
    lb = d // N_LRU_BLOCKS
    gw = d // len(POOL_WINDOWS)
    kept = POOL_PAST - ts

    sources = (("x", x_hbm, x_v), ("awin", awin_f.at[0], stage_in), ("sconv", sconv_hbm, sconv_v),
               ("wr", wr_f.at[0], stage_r), ("wi", wi_f.at[0], stage_i), ("slru", slru_hbm, slru_v),
               ("awout", awout_f.at[0], stage_out), ("spool", spool_hbm, spool_v),
               ("wgrp", wgrp_f.at[0], stage_g),
               ("bwin", bwin_f.at[0], stage_in), ("bwout", bwout_f.at[0], stage_out))
    loads = {name: pltpu.make_async_copy(src, dst, load_sem.at[i])
             for i, (name, src, dst) in enumerate(sources)}
    for name in ("x", "awin", "sconv", "wr", "wi", "slru", "awout", "spool", "wgrp"):
        loads[name].start()
    sinks = (("conv", conv_v, conv_hbm), ("lru", lru_v, lru_hbm),
             ("pool_kept", spool_v.at[pl.ds(ts, kept)], pool_hbm.at[pl.ds(0, kept)]),
             ("pool_new", xb1_v, pool_hbm.at[pl.ds(kept, ts)]), ("y", y_v, y_hbm),
             ("awin", awin_v, awin_o), ("wri", wri_v, wri_o), ("awout", awout_v, awout_o),
             ("bwin", bwin_v, bwin_o), ("wgrp", wgrp_v, wgrp_o), ("bwout", bwout_v, bwout_o))
    stores = {name: pltpu.make_async_copy(src, dst, store_sem.at[i])
              for i, (name, src, dst) in enumerate(sinks)}
    gate_half = jnp.where(lax.broadcasted_iota(jnp.int32, (1, 2 * d), 1) < d, 1.0, 0.5)

    def slab(v, i):
        return v[i * nb:(i + 1) * nb]

    loads["x"].wait()
    x = x_v[...].reshape(ts * nb, d)
    u = _rmsnorm(x, an_ref[...]).astype(BF16)
    loads["awin"].wait()
    _cast_rows(awin_v, stage_in, gate_half)
    loads["bwin"].start()
    proj = jnp.dot(u, awin_v[...], preferred_element_type=F32)
    xb = proj[:, :d]
    for name in ("sconv", "wr", "wi", "slru"):
        loads[name].wait()
    for n in range(N_LRU_BLOCKS):
        wri_v[n, :, :lb] = stage_r[n].astype(BF16)
        wri_v[n, :, lb:] = stage_i[n].astype(BF16)
    for name in ("awin", "wri"):
        stores[name].start(priority=STORE_DMA_PRIORITY)
    ext = [sconv_v[k] for k in range(CONV_W - 1)] + [slab(xb, i) for i in range(ts)]
    hcw, hcb = 0.5 * cw_ref[...], 0.5 * cb_ref[...]
    hx_slabs = []
    for i in range(ts):
        acc = hcb
        for k in range(CONV_W):
            acc = acc + ext[i + k] * hcw[k:k + 1, :]
        hx_slabs.append(acc)
    hx = jnp.concatenate(hx_slabs, axis=0)
    hx_bf16 = hx.astype(BF16)
    zr, zi = [], []
    for n in range(N_LRU_BLOCKS):
        z = jnp.dot(hx_bf16[:, n * lb:(n + 1) * lb], wri_v[n], preferred_element_type=F32)
        zr.append(z[:, :lb])
        zi.append(z[:, lb:])
    a, mult, gated = _lru_terms(hx, jnp.concatenate(zr, axis=1), jnp.concatenate(zi, axis=1),
                                0.5 * br_ref[...], 0.5 * bi_ref[...],
                                _half_neg_c_softplus_neg(lam_ref[...]))
    h = slru_v[...]
    hs_slabs = []
    for i in range(ts):
        m = slab(mult, i)
        if PAST_LEN + i == 0:
            m = jnp.ones_like(m)
        h = slab(a, i) * h + m * slab(gated, i)
        hs_slabs.append(h)
    hs = jnp.concatenate(hs_slabs, axis=0)
    loads["awout"].wait()
    _cast_rows(awout_v, stage_out)
    loads["bwout"].start()
    stores["awout"].start(priority=STORE_DMA_PRIORITY)
    y0 = jnp.dot((hs * _silu_from_half(proj[:, d:])).astype(BF16), awout_v[...],
                 preferred_element_type=F32)
    h1 = x + y0
    lru_v[...] = h
    for k in range(CONV_W - 1):
        conv_v[k] = ext[ts + k]
    for name in ("bwin", "spool", "wgrp"):
        loads[name].wait()
    _cast_rows(bwin_v, stage_in, gate_half)
    for g in range(len(POOL_WINDOWS)):
        wgrp_v[g] = stage_g[g].astype(BF16)
    for name in ("conv", "lru", "pool_kept", "bwin", "wgrp"):
        stores[name].start(priority=STORE_DMA_PRIORITY)

    u1 = _rmsnorm(h1, bn_ref[...]).astype(BF16)
    proj1 = jnp.dot(u1, bwin_v[...], preferred_element_type=F32)
    xb1 = proj1[:, :d]
    ext2 = [spool_v[k] for k in range(POOL_PAST)] + [slab(xb1, i) for i in range(ts)]
    pooled_slabs = []
    for i in range(ts):
        means = []
        for gi, w in enumerate(POOL_WINDOWS):
            lanes = slice(gi * gw, (gi + 1) * gw)
            s = ext2[POOL_PAST + i][:, lanes]
            for j in range(1, w):
                s = s + ext2[POOL_PAST + i - j][:, lanes]
            means.append(s * (1.0 / min(PAST_LEN + i + 1, w)))
        pooled_slabs.append(jnp.concatenate(means, axis=1) - ext2[POOL_PAST + i])
    pooled = jnp.concatenate(pooled_slabs, axis=0).astype(BF16)
    z1 = jnp.concatenate(
        [jnp.dot(pooled[:, g * gw:(g + 1) * gw], wgrp_v[g], preferred_element_type=F32)
         for g in range(len(POOL_WINDOWS))], axis=1)
    z1 = (z1 + bgrp_ref[...]) * bscale_ref[...]
    loads["bwout"].wait()
    _cast_rows(bwout_v, stage_out)
    xb1_v[...] = xb1.reshape(ts, nb, d)
    for name in ("bwout", "pool_new"):
        stores[name].start(priority=STORE_DMA_PRIORITY)
    y1 = jnp.dot((z1 * _silu_from_half(proj1[:, d:])).astype(BF16), bwout_v[...],
                 preferred_element_type=F32)
    h2 = h1 + y1
    y_v[...] = _rmsnorm(h2, fn_ref[...]).reshape(ts, nb, d)
    stores["y"].start(priority=STORE_DMA_PRIORITY)
    for store in stores.values():
        store.wait()


def _whole(shape):
    return pl.BlockSpec(shape, lambda *_: (0,) * len(shape), pipeline_mode=pl.Buffered(1))


def kernel(x_prompt, x_sample, state_conv, state_lru, state_pool, a_norm, a_w_in, a_conv_w, a_conv_b, a_w_r, a_b_r, a_w_i, a_b_i, a_lam, a_w_out, b_norm, b_w_in, b_w_grp, b_b_grp, b_scale, b_w_out, final_norm):
    bp, tp, d = x_prompt.shape
    bs, ts, _ = x_sample.shape
    tb = PROMPT_BLOCK_ROWS
    assert a_norm.shape[0] == 1 and b_norm.shape[0] == 1, "one layer of each mixer type"
    assert d % MXU_COLS == 0 and d // N_LRU_BLOCKS == LANES
    assert tp % tb == 0 and tb % (SUBLANES * SUBLANES) == 0 and tb >= 2 * SUBLANES
    assert CONV_W - 1 <= ts <= POOL_PAST

    row = lambda v: v.reshape(1, d)
    n_slabs = d // LANES
    n_chunks = d // MXU_COLS
    lb = d // N_LRU_BLOCKS
    gw = d // len(POOL_WINDOWS)

    xs = jnp.swapaxes(x_sample, 0, 1)
    sconv = jnp.swapaxes(state_conv[0], 0, 1)
    spool = jnp.swapaxes(state_pool[0], 0, 1)
    hbm = pl.BlockSpec(memory_space=pl.ANY)
    act = lambda shape: pltpu.VMEM(shape, F32)
    matrices = (a_w_in, a_w_r, a_w_i, a_w_out, b_w_in, b_w_grp, b_w_out)
    vectors = (row(a_norm[0]), a_conv_w[0], row(a_conv_b[0]), row(a_b_r[0]), row(a_b_i[0]),
               row(a_lam[0]), row(b_norm[0]), row(b_b_grp[0]), row(b_scale[0]), row(final_norm))
    bf16_shapes = ((d, 2 * d), (N_LRU_BLOCKS, lb, 2 * lb), (d, d), (d, 2 * d),
                   (len(POOL_WINDOWS), gw, gw), (d, d))
    (y_s, conv_s, lru_s, pool_s,
     awin, wri, awout, bwin, wgrp, bwout) = pl.pallas_call(
        _sample_kernel,
        in_specs=[hbm] * (4 + len(matrices)) + [_whole(v.shape) for v in vectors],
        out_specs=[hbm] * (4 + len(bf16_shapes)),
        out_shape=[jax.ShapeDtypeStruct(xs.shape, F32),
                   jax.ShapeDtypeStruct(sconv.shape, F32),
                   jax.ShapeDtypeStruct((bs, d), F32),
                   jax.ShapeDtypeStruct(spool.shape, F32)]
        + [jax.ShapeDtypeStruct(shape, BF16) for shape in bf16_shapes],
        scratch_shapes=[act(xs.shape), act(sconv.shape), act((bs, d)), act(spool.shape),
                        act(xs.shape), act(sconv.shape), act((bs, d)), act(xs.shape)]
        + [pltpu.VMEM(shape, BF16) for shape in bf16_shapes]
        + [act((d, 2 * d)), act((d, d)), act((N_LRU_BLOCKS, lb, lb)), act((N_LRU_BLOCKS, lb, lb)),
           act((len(POOL_WINDOWS), gw, gw)),
           pltpu.SemaphoreType.DMA((11,)), pltpu.SemaphoreType.DMA((11,))],
        compiler_params=pltpu.CompilerParams(vmem_limit_bytes=VMEM_LIMIT_BYTES),
        name="sample_step",
    )(xs, sconv, state_lru[0], spool, *matrices, *vectors)

    weights = (row(a_norm[0]), awin, a_conv_w[0], row(a_conv_b[0]), wri, row(a_b_r[0]),
               row(a_b_i[0]), row(a_lam[0]), awout, row(b_norm[0]), bwin, wgrp, row(b_b_grp[0]),
               row(b_scale[0]), bwout, row(final_norm))
    weight_specs = [_whole(w.shape) for w in weights]
    seg_rows = SUBLANES * (tb // SUBLANES + 4)
    nt = tp // tb
    n_blocks = bp * nt
    blk0 = lambda n: jnp.minimum(n, n_blocks - 1)
    blk1 = lambda n: jnp.maximum(n - 1, 0)
    y_p, conv_p, lru_p, pool_p = pl.pallas_call(
        functools.partial(_prompt_kernel, blocks_per_seq=nt),
        grid=(n_blocks + 1,),
        in_specs=[pl.BlockSpec((1, tb, d), lambda n: (blk0(n) // nt, blk0(n) % nt, 0))] + weight_specs,
        out_specs=[pl.BlockSpec((1, tb, d), lambda n: (blk1(n) // nt, blk1(n) % nt, 0)),
                   pl.BlockSpec((1, CONV_W - 1, d), lambda n: (blk0(n) // nt, 0, 0)),
                   pl.BlockSpec((1, 1, d), lambda n: (blk0(n) // nt, 0, 0)),
                   pl.BlockSpec((1, POOL_PAST, d), lambda n: (blk1(n) // nt, 0, 0))],
        out_shape=[jax.ShapeDtypeStruct((bp, tp, d), F32),
                   jax.ShapeDtypeStruct((bp, CONV_W - 1, d), F32),
                   jax.ShapeDtypeStruct((bp, 1, d), F32),
                   jax.ShapeDtypeStruct((bp, POOL_PAST, d), F32)],
        scratch_shapes=[pltpu.VMEM((n_slabs, CONV_PAD + tb, LANES), F32),
                        pltpu.VMEM((n_slabs, seg_rows, LANES), F32),
                        pltpu.VMEM((n_slabs, seg_rows, LANES), F32),
                        pltpu.VMEM((n_slabs, seg_rows, LANES), F32),
                        pltpu.VMEM((1, d), F32),
                        pltpu.VMEM((2, tb, d), F32),
                        pltpu.VMEM((n_slabs, POOL_PAD + tb, LANES), F32),
                        pltpu.VMEM((2, MXU_COLS // LANES, POOL_PAD + tb, LANES), F32),
                        pltpu.VMEM((tb, d), BF16),
                        pltpu.VMEM((CONV_AHEAD, tb, LANES), F32),
                        pltpu.VMEM((CONV_AHEAD, tb, LANES), BF16),
                        pltpu.VMEM((2, tb, 2 * LANES), F32),
                        pltpu.VMEM((n_chunks, tb, MXU_COLS), F32),
                        pltpu.VMEM((tb, d), BF16),
                        pltpu.VMEM((tb, d), BF16),
                        pltpu.VMEM((tb, d), BF16),
                        pltpu.VMEM((tb, d), F32),
                        pltpu.VMEM((n_chunks, tb, MXU_COLS), F32),
                        pltpu.VMEM((tb, d), BF16)],
        compiler_params=pltpu.CompilerParams(
            vmem_limit_bytes=VMEM_LIMIT_BYTES,
            dimension_semantics=("arbitrary",)),
        name="prompt_step",
    )(x_prompt, *weights)

    return (y_p, jnp.swapaxes(y_s, 0, 1),
            conv_p[None], lru_p.reshape(1, bp, d), pool_p[None],
            jnp.swapaxes(conv_s, 0, 1)[None], lru_s[None], jnp.swapaxes(pool_s, 0, 1)[None])
```

```python
import functools

import jax
import jax.numpy as jnp
from jax import lax
from jax.experimental import pallas as pl
from jax.experimental.pallas import tpu as pltpu

PAST_LEN = 16384
N_LRU_BLOCKS = 8
CONV_W = 4
LRU_C = 8.0
POOL_WINDOWS = (2, 4, 8, 16)
POOL_PAST = max(POOL_WINDOWS) - 1
EPS = 1e-6

LANES = 128
SUBLANES = 8
MXU_COLS = 256
CONV_PAD = SUBLANES
POOL_LEVELS = max(POOL_WINDOWS).bit_length() - 1
POOL_PAD = POOL_LEVELS * SUBLANES
PROMPT_BLOCK_ROWS = 512
CONV_AHEAD = 4
VMEM_LIMIT_BYTES = 56 * 1024 * 1024

F32 = jnp.float32
BF16 = jnp.bfloat16


def _rmsnorm(x, g):
    return x * lax.rsqrt(jnp.mean(x * x, axis=-1, keepdims=True) + EPS) * g


def _silu_from_half(hg):
    return hg + hg * jnp.tanh(hg)


def _half_neg_c_softplus_neg(lam):
    z = -lam
    return (-0.5 * LRU_C) * (jnp.maximum(z, 0.0) + jnp.log1p(jnp.exp(-jnp.abs(z))))


def _lru_terms(hx, zr, zi, hbr, hbi, hn):
    log_a = hn + hn * jnp.tanh(zr + hbr)
    a = jnp.exp(log_a)
    y = jnp.tanh(log_a) * (-1.0 - a * a)
    mult = jnp.where(y > 0.0, y * lax.rsqrt(y), 0.0)
    gated = hx + hx * jnp.tanh(zi + hbi)
    return a, mult, gated


def _lane_slab(ref, c):
    return ref[:, c * LANES:(c + 1) * LANES]


def _cols(q):
    return slice(q * MXU_COLS, (q + 1) * MXU_COLS)


def _prompt_stage_order(n_chunks, slabs_per_chunk):
    assert (n_chunks, slabs_per_chunk) == (4, 2)
    return ("aN aX0 bN aX1 aC0 bX0 aC1 aC2 aC3 aR0 aR1 aK0 aX2 aS0 aK1 bX1 aS1 bP0 aR2 aC4 aR3 bM0 aC5 aX3 "
            "aK2 aS2 bX2 aK3 aS3 aR4 bP1 aR5 bM1 aC6 aG0 aC7 aK4 bX3 aS4 aK5 aR6 aR7 aS5 bP2 bM2 aK6 aG1 aS6 "
            "aK7 bG0 aS7 bP3 bM3 aH0 aG2 aH1 bZ0 bG1 aH2 aG3 bZ1 bG2 aH3 bG3 bZ2 bZ3 bO0 bO1 bO2 bO3 aO0 bF "
            "aO1 aO2 aO3").split()


def _prompt_kernel(x_ref, an_ref, awin_ref, cw_ref, cb_ref, wri_ref, br_ref, bi_ref, lam_ref,
                   awout_ref, bn_ref, bwin_ref, wgrp_ref, bgrp_ref, bscale_ref, bwout_ref, fn_ref,
                   y_ref, conv_ref, lru_ref, pool_ref,
                   ext_s, sa_s, sb_s, sh_s, h_s, h1_s, px_s, pt_s,
                   ua_s, hx_s, hxb_s, zr_s, ga_s, hsg_s, ub_s, pooled_s, z_s, gq_s, zg_s, *,
                   blocks_per_seq):
    n = pl.program_id(0)
    n_blocks = pl.num_programs(0) - 1
    ta = jnp.minimum(n, n_blocks - 1) % blocks_per_seq
    tbk = jnp.maximum(n - 1, 0) % blocks_per_seq
    tb, d = x_ref.shape[1], x_ref.shape[2]
    n_slabs = d // LANES
    n_chunks = d // MXU_COLS
    slabs_per_chunk = MXU_COLS // LANES
    assert d // len(POOL_WINDOWS) == MXU_COLS, "one pooling group per matmul column chunk"
    seg = tb // SUBLANES
    pitch = seg + 4
    conv_ahead = hx_s.shape[0]

    @pl.when(n == 0)
    def _():
        h1_s[...] = jnp.zeros_like(h1_s)

    @pl.when(ta == 0)
    def _():
        ext_s[:, 0:CONV_PAD, :] = jnp.zeros((n_slabs, CONV_PAD, LANES), F32)
        h_s[...] = jnp.zeros_like(h_s)

    @pl.when(tbk == 0)
    def _():
        px_s[:, 0:POOL_PAD, :] = jnp.zeros((n_slabs, POOL_PAD, LANES), F32)

    h1_in = h1_s.at[(n + 1) % 2]
    h1_out = h1_s.at[n % 2]
    sub = lax.broadcasted_iota(jnp.int32, (SUBLANES, 1), 0)
    first_token = jnp.logical_and(ta == 0, sub == 0)
    row16 = lax.broadcasted_iota(jnp.int32, (2 * SUBLANES, 1), 0)
    pos16 = tbk * tb + row16
    hn_row = _half_neg_c_softplus_neg(lam_ref[...])
    hcw, hcb = 0.5 * cw_ref[...], 0.5 * cb_ref[...]
    hbr, hbi = 0.5 * br_ref[...], 0.5 * bi_ref[...]
    carries = [None] * n_slabs
    stages = {}

    def a_norm():
        ua_s[...] = _rmsnorm(x_ref[0], an_ref[...]).astype(BF16)

    def a_xb_chunk(q):
        pq = jnp.dot(ua_s[...], awin_ref[:, _cols(q)], preferred_element_type=F32)
        for i in range(slabs_per_chunk):
            ext_s[q * slabs_per_chunk + i, CONV_PAD:CONV_PAD + tb, :] = pq[:, i * LANES:(i + 1) * LANES]

    def a_gate_chunk(q):
        ga_s[q] = jnp.dot(ua_s[...], awin_ref[:, _cols(n_chunks + q)], preferred_element_type=F32)

    def a_conv(c):
        lanes = slice(c * LANES, (c + 1) * LANES)
        for j in range(SUBLANES):
            r0 = CONV_PAD + j * seg
            hx = ext_s[c, r0:r0 + seg, :] * hcw[CONV_W - 1:CONV_W, lanes] + hcb[:, lanes]
            for k in range(CONV_W - 1):
                hx = hx + ext_s[c, pl.ds(r0 - (CONV_W - 1) + k, seg), :] * hcw[k:k + 1, lanes]
            hx_s[c % conv_ahead, j * seg:(j + 1) * seg, :] = hx
            hxb_s[c % conv_ahead, j * seg:(j + 1) * seg, :] = hx.astype(BF16)

    def a_gate_proj(c):
        zr_s[c % 2] = jnp.dot(hxb_s[c % conv_ahead], wri_ref[c], preferred_element_type=F32)

    def a_coeffs(c):
        lanes = slice(c * LANES, (c + 1) * LANES)
        for j in range(SUBLANES):
            rows = slice(j * seg, (j + 1) * seg)
            a, mult, gated = _lru_terms(hx_s[c % conv_ahead, rows, :], zr_s[c % 2, rows, :LANES],
                                        zr_s[c % 2, rows, LANES:], hbr[:, lanes], hbi[:, lanes],
                                        hn_row[:, lanes])
            if j == 0:
                mult = jnp.concatenate(
                    [jnp.where(first_token, 1.0, mult[0:SUBLANES]), mult[SUBLANES:]], axis=0)
            sa_s[c, pl.ds(j * pitch, seg), :] = a
            sb_s[c, pl.ds(j * pitch, seg), :] = mult * gated

    def a_scan(c):
        hl = sb_s[c, pl.ds(0, SUBLANES, stride=pitch), :]
        pp = sa_s[c, pl.ds(0, SUBLANES, stride=pitch), :]
        for i in range(1, seg):
            av = sa_s[c, pl.ds(i, SUBLANES, stride=pitch), :]
            hl = av * hl + sb_s[c, pl.ds(i, SUBLANES, stride=pitch), :]
            pp = av * pp
        h_in = jnp.broadcast_to(_lane_slab(h_s, c), (SUBLANES, LANES))
        cin = h_in
        for _ in range(SUBLANES - 1):
            cin = jnp.where(sub == 0, h_in, pltpu.roll(hl + pp * cin, 1, 0))
        carries[c] = (hl + pp * cin)[SUBLANES - 1:SUBLANES]
        h = cin
        for i in range(seg):
            h = sa_s[c, pl.ds(i, SUBLANES, stride=pitch), :] * h + sb_s[c, pl.ds(i, SUBLANES, stride=pitch), :]
            sh_s[c, pl.ds(i, SUBLANES, stride=pitch), :] = h

    def a_gated_out(q):
        hs = jnp.concatenate(
            [jnp.concatenate([sh_s[c, pl.ds(j * pitch, seg), :] for j in range(SUBLANES)], axis=0)
             for c in range(q * slabs_per_chunk, (q + 1) * slabs_per_chunk)], axis=1)
        hsg_s[:, _cols(q)] = (hs * _silu_from_half(ga_s[q])).astype(BF16)

    def a_out_proj(q):
        y0 = jnp.dot(hsg_s[...], awout_ref[:, _cols(q)], preferred_element_type=F32)
        h1_out[:, _cols(q)] = x_ref[0, :, _cols(q)] + y0

    def b_norm():
        ub_s[...] = _rmsnorm(h1_in[...], bn_ref[...]).astype(BF16)

    def b_xb_chunk(q):
        pq = jnp.dot(ub_s[...], bwin_ref[:, _cols(q)], preferred_element_type=F32)
        for i in range(slabs_per_chunk):
            px_s[q * slabs_per_chunk + i, POOL_PAD:POOL_PAD + tb, :] = pq[:, i * LANES:(i + 1) * LANES]

    def pool_slab(c, w):
        levels = w.bit_length() - 1
        tmp = c % slabs_per_chunk
        src = px_s.at[c]
        s = None
        for k in range(levels):
            back = (levels - 1 - k) * SUBLANES
            rows = tb + back
            s = (src[pl.ds(POOL_PAD - back, rows), :]
                 + src[pl.ds(POOL_PAD - back - (1 << k), rows), :])
            if k < levels - 1:
                pt_s[k % 2, tmp, POOL_PAD - back:POOL_PAD + tb, :] = s
                src = pt_s.at[k % 2, tmp]
        inv_cnt = 1.0 / jnp.minimum(pos16 + 1, w).astype(F32)
        mean = jnp.concatenate([s[0:2 * SUBLANES] * inv_cnt, s[2 * SUBLANES:] * (1.0 / w)], axis=0)
        return mean - px_s[c, POOL_PAD:POOL_PAD + tb, :]

    def b_pool(q):
        pooled_s[:, _cols(q)] = jnp.concatenate(
            [pool_slab(q * slabs_per_chunk + i, POOL_WINDOWS[q]) for i in range(slabs_per_chunk)],
            axis=1).astype(BF16)

    def b_group_proj(q):
        z_s[:, _cols(q)] = jnp.dot(pooled_s[:, _cols(q)], wgrp_ref[q], preferred_element_type=F32)

    def b_gate_chunk(q):
        gq_s[q] = jnp.dot(ub_s[...], bwin_ref[:, _cols(n_chunks + q)], preferred_element_type=F32)

    def b_gated_proj(q):
        z1 = (z_s[:, _cols(q)] + bgrp_ref[:, _cols(q)]) * bscale_ref[:, _cols(q)]
        zg_s[:, _cols(q)] = (z1 * _silu_from_half(gq_s[q])).astype(BF16)

    def b_out_proj(q):
        y1 = jnp.dot(zg_s[...], bwout_ref[:, _cols(q)], preferred_element_type=F32)
        y_ref[0, :, _cols(q)] = h1_in[:, _cols(q)] + y1

    def b_final_norm():
        y_ref[0] = _rmsnorm(y_ref[0], fn_ref[...])

    stages["aN"] = a_norm
    stages["bN"] = b_norm
    stages["bF"] = b_final_norm
    for q in range(n_chunks):
        for name, fn in (("aX", a_xb_chunk), ("aG", a_gate_chunk), ("aH", a_gated_out),
                         ("aO", a_out_proj), ("bX", b_xb_chunk), ("bP", b_pool),
                         ("bM", b_group_proj), ("bG", b_gate_chunk), ("bZ", b_gated_proj),
                         ("bO", b_out_proj)):
            stages[f"{name}{q}"] = functools.partial(fn, q)
    for c in range(n_slabs):
        for name, fn in (("aC", a_conv), ("aR", a_gate_proj), ("aK", a_coeffs), ("aS", a_scan)):
            stages[f"{name}{c}"] = functools.partial(fn, c)

    order = _prompt_stage_order(n_chunks, slabs_per_chunk)
    assert sorted(order) == sorted(stages), "every stage is issued exactly once"
    at = order.index
    for c in range(n_slabs):
        assert c < conv_ahead or at(f"aK{c - conv_ahead}") < at(f"aC{c}")
        assert c < zr_s.shape[0] or at(f"aK{c - zr_s.shape[0]}") < at(f"aR{c}")
    for name in order:
        stages[name]()
    h_s[...] = jnp.concatenate(carries, axis=1)

    @pl.when(jnp.logical_and(ta == blocks_per_seq - 1, n < n_blocks))
    def _():
        for c in range(n_slabs):
            conv_ref[0, :, c * LANES:(c + 1) * LANES] = ext_s[c, tb + CONV_PAD - (CONV_W - 1):tb + CONV_PAD, :]
        lru_ref[0] = h_s[...]

    @pl.when(jnp.logical_and(tbk == blocks_per_seq - 1, n > 0))
    def _():
        for c in range(n_slabs):
            pool_ref[0, :, c * LANES:(c + 1) * LANES] = px_s[c, tb + POOL_PAD - POOL_PAST:tb + POOL_PAD, :]

    ext_s[:, 0:CONV_PAD, :] = ext_s[:, tb:tb + CONV_PAD, :]
    px_s[:, 0:POOL_PAD, :] = px_s[:, tb:tb + POOL_PAD, :]


def _cast_rows(dst, src, scale=None, rows_per_step=128):
    for r in range(0, src.shape[0], rows_per_step):
        v = src[r:r + rows_per_step, :]
        dst[r:r + rows_per_step, :] = (v if scale is None else v * scale).astype(dst.dtype)


def _sample_kernel(x_hbm, sconv_hbm, slru_hbm, spool_hbm,
                   awin_f, wr_f, wi_f, awout_f, bwin_f, wgrp_f, bwout_f,
                   an_ref, cw_ref, cb_ref, br_ref, bi_ref, lam_ref, bn_ref, bgrp_ref, bscale_ref, fn_ref,
                   y_hbm, conv_hbm, lru_hbm, pool_hbm,
                   awin_o, wri_o, awout_o, bwin_o, wgrp_o, bwout_o,
                   x_v, sconv_v, slru_v, spool_v, y_v, conv_v, lru_v, xb1_v,
                   awin_v, wri_v, awout_v, bwin_v, wgrp_v, bwout_v,
                   stage_in, stage_out, stage_r, stage_i, stage_g, load_sem, store_sem):
    ts, nb, d = x_v.shape
    lb = d // N_LRU_BLOCKS
    gw = d // len(POOL_WINDOWS)
    kept = POOL_PAST - ts

    xb_cols, gate_cols = pl.ds(0, d), pl.ds(d, d)
    sources = (("x", x_hbm, x_v), ("awin", awin_f.at[0, :, xb_cols], stage_in.at[:, xb_cols]),
               ("awin_g", awin_f.at[0, :, gate_cols], stage_in.at[:, gate_cols]),
               ("sconv", sconv_hbm, sconv_v),
               ("wr", wr_f.at[0], stage_r), ("wi", wi_f.at[0], stage_i), ("slru", slru_hbm, slru_v),
               ("awout", awout_f.at[0], stage_out), ("spool", spool_hbm, spool_v),
               ("wgrp", wgrp_f.at[0], stage_g),
               ("bwin", bwin_f.at[0], stage_in), ("bwout", bwout_f.at[0], stage_out))
    loads = {name: pltpu.make_async_copy(src, dst, load_sem.at[i])
             for i, (name, src, dst) in enumerate(sources)}
    for name in ("x", "awin", "awin_g", "sconv", "wr", "wi", "slru", "awout", "spool", "wgrp"):
        loads[name].start()
    sinks = (("conv", conv_v, conv_hbm), ("lru", lru_v, lru_hbm),
             ("pool_kept", spool_v.at[pl.ds(ts, kept)], pool_hbm.at[pl.ds(0, kept)]),
             ("pool_new", xb1_v, pool_hbm.at[pl.ds(kept, ts)]), ("y", y_v, y_hbm),
             ("awin", awin_v, awin_o), ("wri", wri_v, wri_o), ("awout", awout_v, awout_o),
             ("bwin", bwin_v, bwin_o), ("wgrp", wgrp_v, wgrp_o), ("bwout", bwout_v, bwout_o))
    stores = {name: pltpu.make_async_copy(src, dst, store_sem.at[i])
              for i, (name, src, dst) in enumerate(sinks)}
    gate_half = jnp.where(lax.broadcasted_iota(jnp.int32, (1, 2 * d), 1) < d, 1.0, 0.5)

    def slab(v, i):
        return v[i * nb:(i + 1) * nb]

    loads["x"].wait()
    x = x_v[...].reshape(ts * nb, d)
    u = _rmsnorm(x, an_ref[...]).astype(BF16)
    loads["awin"].wait()
    _cast_rows(awin_v.at[:, xb_cols], stage_in.at[:, xb_cols])
    xb = jnp.dot(u, awin_v[:, :d], preferred_element_type=F32)
    for name in ("awin_g", "sconv", "wr", "wi", "slru"):
        loads[name].wait()
    _cast_rows(awin_v.at[:, gate_cols], stage_in.at[:, gate_cols], 0.5)
    loads["bwin"].start()
    half_gate = jnp.dot(u, awin_v[:, d:], preferred_element_type=F32)
    for n in range(N_LRU_BLOCKS):
        wri_v[n, :, :lb] = stage_r[n].astype(BF16)
        wri_v[n, :, lb:] = stage_i[n].astype(BF16)
    for name in ("awin", "wri"):
        stores[name].start()
    ext = [sconv_v[k] for k in range(CONV_W - 1)] + [slab(xb, i) for i in range(ts)]
    hcw, hcb = 0.5 * cw_ref[...], 0.5 * cb_ref[...]
    hx_slabs = []
    for i in range(ts):
        acc = hcb
        for k in range(CONV_W):
            acc = acc + ext[i + k] * hcw[k:k + 1, :]
        hx_slabs.append(acc)
    hx = jnp.concatenate(hx_slabs, axis=0)
    hx_bf16 = hx.astype(BF16)
    zr, zi = [], []
    for n in range(N_LRU_BLOCKS):
        z = jnp.dot(hx_bf16[:, n * lb:(n + 1) * lb], wri_v[n], preferred_element_type=F32)
        zr.append(z[:, :lb])
        zi.append(z[:, lb:])
    a, mult, gated = _lru_terms(hx, jnp.concatenate(zr, axis=1), jnp.concatenate(zi, axis=1),
                                0.5 * br_ref[...], 0.5 * bi_ref[...],
                                _half_neg_c_softplus_neg(lam_ref[...]))
    h = slru_v[...]
    hs_slabs = []
    for i in range(ts):
        m = slab(mult, i)
        if PAST_LEN + i == 0:
            m = jnp.ones_like(m)
        h = slab(a, i) * h + m * slab(gated, i)
        hs_slabs.append(h)
    hs = jnp.concatenate(hs_slabs, axis=0)
    loads["awout"].wait()
    _cast_rows(awout_v, stage_out)
    loads["bwout"].start()
    stores["awout"].start()
    y0 = jnp.dot((hs * _silu_from_half(half_gate)).astype(BF16), awout_v[...],
                 preferred_element_type=F32)
    h1 = x + y0
    lru_v[...] = h
    for k in range(CONV_W - 1):
        conv_v[k] = ext[ts + k]
    for name in ("bwin", "spool", "wgrp"):
        loads[name].wait()
    _cast_rows(bwin_v, stage_in, gate_half)
    for g in range(len(POOL_WINDOWS)):
        wgrp_v[g] = stage_g[g].astype(BF16)
    for name in ("conv", "lru", "pool_kept", "bwin", "wgrp"):
        stores[name].start()

    u1 = _rmsnorm(h1, bn_ref[...]).astype(BF16)
    proj1 = jnp.dot(u1, bwin_v[...], preferred_element_type=F32)
    xb1 = proj1[:, :d]
    ext2 = [spool_v[k] for k in range(POOL_PAST)] + [slab(xb1, i) for i in range(ts)]
    pooled_slabs = []
    for i in range(ts):
        means = []
        for gi, w in enumerate(POOL_WINDOWS):
            lanes = slice(gi * gw, (gi + 1) * gw)
            s = ext2[POOL_PAST + i][:, lanes]
            for j in range(1, w):
                s = s + ext2[POOL_PAST + i - j][:, lanes]
            means.append(s * (1.0 / min(PAST_LEN + i + 1, w)))
        pooled_slabs.append(jnp.concatenate(means, axis=1) - ext2[POOL_PAST + i])
    pooled = jnp.concatenate(pooled_slabs, axis=0).astype(BF16)
    z1 = jnp.concatenate(
        [jnp.dot(pooled[:, g * gw:(g + 1) * gw], wgrp_v[g], preferred_element_type=F32)
         for g in range(len(POOL_WINDOWS))], axis=1)
    z1 = (z1 + bgrp_ref[...]) * bscale_ref[...]
    loads["bwout"].wait()
    _cast_rows(bwout_v, stage_out)
    xb1_v[...] = xb1.reshape(ts, nb, d)
    for name in ("bwout", "pool_new"):
        stores[name].start()
    y1 = jnp.dot((z1 * _silu_from_half(proj1[:, d:])).astype(BF16), bwout_v[...],
                 preferred_element_type=F32)
    h2 = h1 + y1
    y_v[...] = _rmsnorm(h2, fn_ref[...]).reshape(ts, nb, d)
    stores["y"].start()
    for store in stores.values():
        store.wait()


def _whole(shape):
    return pl.BlockSpec(shape, lambda *_: (0,) * len(shape), pipeline_mode=pl.Buffered(1))


def kernel(x_prompt, x_sample, state_conv, state_lru, state_pool, a_norm, a_w_in, a_conv_w, a_conv_b, a_w_r, a_b_r, a_w_i, a_b_i, a_lam, a_w_out, b_norm, b_w_in, b_w_grp, b_b_grp, b_scale, b_w_out, final_norm):
    bp, tp, d = x_prompt.shape
    bs, ts, _ = x_sample.shape
    tb = PROMPT_BLOCK_ROWS
    assert a_norm.shape[0] == 1 and b_norm.shape[0] == 1, "one layer of each mixer type"
    assert d % MXU_COLS == 0 and d // N_LRU_BLOCKS == LANES
    assert tp % tb == 0 and tb % (SUBLANES * SUBLANES) == 0 and tb >= 2 * SUBLANES
    assert CONV_W - 1 <= ts <= POOL_PAST

    row = lambda v: v.reshape(1, d)
    n_slabs = d // LANES
    n_chunks = d // MXU_COLS
    lb = d // N_LRU_BLOCKS
    gw = d // len(POOL_WINDOWS)

    xs = jnp.swapaxes(x_sample, 0, 1)
    sconv = jnp.swapaxes(state_conv[0], 0, 1)
    spool = jnp.swapaxes(state_pool[0], 0, 1)
    hbm = pl.BlockSpec(memory_space=pl.ANY)
    act = lambda shape: pltpu.VMEM(shape, F32)
    matrices = (a_w_in, a_w_r, a_w_i, a_w_out, b_w_in, b_w_grp, b_w_out)
    vectors = (row(a_norm[0]), a_conv_w[0], row(a_conv_b[0]), row(a_b_r[0]), row(a_b_i[0]),
               row(a_lam[0]), row(b_norm[0]), row(b_b_grp[0]), row(b_scale[0]), row(final_norm))
    bf16_shapes = ((d, 2 * d), (N_LRU_BLOCKS, lb, 2 * lb), (d, d), (d, 2 * d),
                   (len(POOL_WINDOWS), gw, gw), (d, d))
    (y_s, conv_s, lru_s, pool_s,
     awin, wri, awout, bwin, wgrp, bwout) = pl.pallas_call(
        _sample_kernel,
        in_specs=[hbm] * (4 + len(matrices)) + [_whole(v.shape) for v in vectors],
        out_specs=[hbm] * (4 + len(bf16_shapes)),
        out_shape=[jax.ShapeDtypeStruct(xs.shape, F32),
                   jax.ShapeDtypeStruct(sconv.shape, F32),
                   jax.ShapeDtypeStruct((bs, d), F32),
                   jax.ShapeDtypeStruct(spool.shape, F32)]
        + [jax.ShapeDtypeStruct(shape, BF16) for shape in bf16_shapes],
        scratch_shapes=[act(xs.shape), act(sconv.shape), act((bs, d)), act(spool.shape),
                        act(xs.shape), act(sconv.shape), act((bs, d)), act(xs.shape)]
        + [pltpu.VMEM(shape, BF16) for shape in bf16_shapes]
        + [act((d, 2 * d)), act((d, d)), act((N_LRU_BLOCKS, lb, lb)), act((N_LRU_BLOCKS, lb, lb)),
           act((len(POOL_WINDOWS), gw, gw)),
           pltpu.SemaphoreType.DMA((12,)), pltpu.SemaphoreType.DMA((11,))],
        compiler_params=pltpu.CompilerParams(vmem_limit_bytes=VMEM_LIMIT_BYTES),
        name="sample_step",
    )(xs, sconv, state_lru[0], spool, *matrices, *vectors)

    weights = (row(a_norm[0]), awin, a_conv_w[0], row(a_conv_b[0]), wri, row(a_b_r[0]),
               row(a_b_i[0]), row(a_lam[0]), awout, row(b_norm[0]), bwin, wgrp, row(b_b_grp[0]),
               row(b_scale[0]), bwout, row(final_norm))
    weight_specs = [_whole(w.shape) for w in weights]
    seg_rows = SUBLANES * (tb // SUBLANES + 4)
    nt = tp // tb
    n_blocks = bp * nt
    blk0 = lambda n: jnp.minimum(n, n_blocks - 1)
    blk1 = lambda n: jnp.maximum(n - 1, 0)
    y_p, conv_p, lru_p, pool_p = pl.pallas_call(
        functools.partial(_prompt_kernel, blocks_per_seq=nt),
        grid=(n_blocks + 1,),
        in_specs=[pl.BlockSpec((1, tb, d), lambda n: (blk0(n) // nt, blk0(n) % nt, 0))] + weight_specs,
        out_specs=[pl.BlockSpec((1, tb, d), lambda n: (blk1(n) // nt, blk1(n) % nt, 0)),
                   pl.BlockSpec((1, CONV_W - 1, d), lambda n: (blk0(n) // nt, 0, 0)),
                   pl.BlockSpec((1, 1, d), lambda n: (blk0(n) // nt, 0, 0)),
                   pl.BlockSpec((1, POOL_PAST, d), lambda n: (blk1(n) // nt, 0, 0))],
        out_shape=[jax.ShapeDtypeStruct((bp, tp, d), F32),
                   jax.ShapeDtypeStruct((bp, CONV_W - 1, d), F32),
                   jax.ShapeDtypeStruct((bp, 1, d), F32),
                   jax.ShapeDtypeStruct((bp, POOL_PAST, d), F32)],
        scratch_shapes=[pltpu.VMEM((n_slabs, CONV_PAD + tb, LANES), F32),
                        pltpu.VMEM((n_slabs, seg_rows, LANES), F32),
                        pltpu.VMEM((n_slabs, seg_rows, LANES), F32),
                        pltpu.VMEM((n_slabs, seg_rows, LANES), F32),
                        pltpu.VMEM((1, d), F32),
                        pltpu.VMEM((2, tb, d), F32),
                        pltpu.VMEM((n_slabs, POOL_PAD + tb, LANES), F32),
                        pltpu.VMEM((2, MXU_COLS // LANES, POOL_PAD + tb, LANES), F32),
                        pltpu.VMEM((tb, d), BF16),
                        pltpu.VMEM((CONV_AHEAD, tb, LANES), F32),
                        pltpu.VMEM((CONV_AHEAD, tb, LANES), BF16),
                        pltpu.VMEM((2, tb, 2 * LANES), F32),
                        pltpu.VMEM((n_chunks, tb, MXU_COLS), F32),
                        pltpu.VMEM((tb, d), BF16),
                        pltpu.VMEM((tb, d), BF16),
                        pltpu.VMEM((tb, d), BF16),
                        pltpu.VMEM((tb, d), F32),
                        pltpu.VMEM((n_chunks, tb, MXU_COLS), F32),
                        pltpu.VMEM((tb, d), BF16)],
        compiler_params=pltpu.CompilerParams(
            vmem_limit_bytes=VMEM_LIMIT_BYTES,
            dimension_semantics=("arbitrary",)),
        name="prompt_step",
    )(x_prompt, *weights)

    return (y_p, jnp.swapaxes(y_s, 0, 1),
            conv_p[None], lru_p.reshape(1, bp, d), pool_p[None],
            jnp.swapaxes(conv_s, 0, 1)[None], lru_s[None], jnp.swapaxes(pool_s, 0, 1)[None])
```
